```python
import jax, jax.numpy as jnp
from jax import lax
import numpy as np

D_MODEL = 1024
BATCH = 2
SEQ = 8192
DEPTH = 2

EPS = 1e-6
ROPE_THETA = 10000.0
N_BRANCH = 3
LRU_WIDTH = 512
LRU_BLOCKS = 8
LRU_BLOCK = LRU_WIDTH // LRU_BLOCKS
LRU_CONV = 4
LRU_C = 8.0
RET_HEADS = 8
RET_DK = 64
RET_DV = 64
RET_CHUNK = 128
MLA_HEADS = 8
MLA_Q_RANK = 384
MLA_KV_RANK = 256
MLA_NOPE = 64
MLA_ROPE = 32
MLA_V = 64
ATTN_BLOCK = 128
D_FF = 2816
FFN_CONV = 3

IN_SIZES = (LRU_WIDTH, LRU_WIDTH,
            RET_HEADS * RET_DK, RET_HEADS * RET_DK, RET_HEADS * RET_DV, RET_HEADS * RET_DV,
            MLA_Q_RANK, MLA_KV_RANK, MLA_ROPE,
            N_BRANCH * D_MODEL)
N_IN = (2 * LRU_WIDTH + 2 * RET_HEADS * RET_DK + 2 * RET_HEADS * RET_DV
        + MLA_Q_RANK + MLA_KV_RANK + MLA_ROPE + N_BRANCH * D_MODEL)

kernel_name = "hybrid_lru_retention_mla_convglu"


def rmsnorm(x, g):
    xf = x.astype(jnp.float32)
    y = xf * lax.rsqrt(jnp.mean(xf * xf, axis=-1, keepdims=True) + EPS)
    return (y * g.astype(jnp.float32)).astype(x.dtype)


def modulate(h, shift, scale):
    return h * (1.0 + scale[:, None, :]) + shift[:, None, :]


def causal_dwconv(x, w, b):
    k_w = w.shape[0]
    s = x.shape[1]
    xp = jnp.pad(x, ((0, 0), (k_w - 1, 0), (0, 0)))
    y = xp[:, k_w - 1:k_w - 1 + s] * w[k_w - 1] + b
    for k in range(k_w - 1):
        y = y + xp[:, k:k + s] * w[k]
    return y


def rope_tables(positions, dim):
    inv = ROPE_THETA ** (-jnp.arange(0, dim, 2, dtype=jnp.float32) / dim)
    ang = positions.astype(jnp.float32)[..., None] * inv
    return jnp.cos(ang), jnp.sin(ang)


def apply_rope(x, cos, sin):
    half = x.shape[-1] // 2
    x1, x2 = x[..., :half], x[..., half:]
    y = jnp.concatenate([x1 * cos - x2 * sin, x1 * sin + x2 * cos], axis=-1)
    return y.astype(x.dtype)


def split_cols(t):
    out = []
    start = 0
    for n in IN_SIZES:
        out.append(t[..., start:start + n])
        start += n
    return out


def rglru_branch(xb, gb, conv_w, conv_b, wa, ba, wx, bx, lam):
    bsz, s, _ = xb.shape
    xc = causal_dwconv(xb, conv_w, conv_b)
    xblk = xc.reshape(bsz, s, LRU_BLOCKS, LRU_BLOCK)
    r = jax.nn.sigmoid(jnp.einsum('bsnh,nhk->bsnk', xblk, wa).reshape(bsz, s, LRU_WIDTH) + ba)
    i = jax.nn.sigmoid(jnp.einsum('bsnh,nhk->bsnk', xblk, wx).reshape(bsz, s, LRU_WIDTH) + bx)
    log_a = -LRU_C * r.astype(jnp.float32) * jax.nn.softplus(-lam.astype(jnp.float32))
    a = jnp.exp(log_a)
    mult = jnp.sqrt(-jnp.expm1(2.0 * log_a))
    u = mult * (i * xc).astype(jnp.float32)

    def combine(p, q):
        a1, b1 = p
        a2, b2 = q
        return a1 * a2, a2 * b1 + b2

    _, h = lax.associative_scan(combine, (a, u), axis=1)
    return h.astype(xb.dtype) * jax.nn.gelu(gb)


def retention_branch(q, k, v, g, cos, sin):
    bsz, s, _ = q.shape
    nc = s // RET_CHUNK
    f32 = jnp.float32
    q = apply_rope(q.reshape(bsz, s, RET_HEADS, RET_DK), cos[:, :, None, :], sin[:, :, None, :]).astype(f32)
    k = apply_rope(k.reshape(bsz, s, RET_HEADS, RET_DK), cos[:, :, None, :], sin[:, :, None, :]).astype(f32)
    k = k * (RET_DK ** -0.5)
    v = v.reshape(bsz, s, RET_HEADS, RET_DV).astype(f32)

    log_gamma = jnp.log1p(-(2.0 ** (-5.0 - jnp.arange(RET_HEADS, dtype=f32))))
    idx = jnp.arange(RET_CHUNK, dtype=f32)
    diff = idx[:, None] - idx[None, :]
    causal = diff >= 0
    inner_decay = jnp.where(causal[None], jnp.exp(jnp.where(causal, diff, 0.0)[None] * log_gamma[:, None, None]), 0.0)
    kv_decay = jnp.exp((RET_CHUNK - 1.0 - idx)[None, :] * log_gamma[:, None])
    q_decay = jnp.exp((idx + 1.0)[:, None] * log_gamma[None, :])
    chunk_decay = jnp.exp(RET_CHUNK * log_gamma)

    qc = q.reshape(bsz, nc, RET_CHUNK, RET_HEADS, RET_DK)
    kc = k.reshape(bsz, nc, RET_CHUNK, RET_HEADS, RET_DK)
    vc = v.reshape(bsz, nc, RET_CHUNK, RET_HEADS, RET_DV)

    scores = jnp.einsum('bnchd,bnshd->bnhcs', qc, kc) * inner_decay
    y_inner = jnp.einsum('bnhcs,bnshe->bnche', scores, vc)

    kv = jnp.einsum('bnshd,bnshe,hs->nbhde', kc, vc, kv_decay)

    def step(state, kv_n):
        return state * chunk_decay[None, :, None, None] + kv_n, state

    _, s_prev = lax.scan(step, jnp.zeros((bsz, RET_HEADS, RET_DK, RET_DV), f32), kv)
    y_cross = jnp.einsum('bnchd,nbhde->bnche', qc, s_prev) * q_decay[None, None, :, :, None]

    y = (y_inner + y_cross).reshape(bsz, s, RET_HEADS, RET_DV)
    mu = jnp.mean(y, axis=-1, keepdims=True)
    var = jnp.mean(jnp.square(y - mu), axis=-1, keepdims=True)
    y = ((y - mu) * lax.rsqrt(var + EPS)).reshape(bsz, s, RET_HEADS * RET_DV).astype(g.dtype)
    return jax.nn.silu(g) * y


def mla_branch(cq, ckv, kr, q_norm_g, w_uq, kv_norm_g, w_ukv, cos, sin):
    bsz, s, _ = cq.shape
    dqk = MLA_NOPE + MLA_ROPE
    q = (rmsnorm(cq, q_norm_g) @ w_uq).reshape(bsz, s, MLA_HEADS, dqk)
    q_nope, q_rope = q[..., :MLA_NOPE], q[..., MLA_NOPE:]
    q_rope = apply_rope(q_rope, cos[:, :, None, :], sin[:, :, None, :])
    kvu = (rmsnorm(ckv, kv_norm_g) @ w_ukv).reshape(bsz, s, MLA_HEADS, MLA_NOPE + MLA_V)
    k_nope, v = kvu[..., :MLA_NOPE], kvu[..., MLA_NOPE:]
    k_rope = apply_rope(kr, cos, sin)
    q = jnp.concatenate([q_nope, q_rope], axis=-1)
    k = jnp.concatenate([k_nope, jnp.broadcast_to(k_rope[:, :, None, :], (bsz, s, MLA_HEADS, MLA_ROPE))], axis=-1)
    scale = dqk ** -0.5
    nb = s // ATTN_BLOCK
    qb = q.reshape(bsz, nb, ATTN_BLOCK, MLA_HEADS, dqk).transpose(1, 0, 2, 3, 4)
    key_idx = jnp.arange(s)

    def attend(args):
        q_blk, start = args
        sc = jnp.einsum('bqhd,bkhd->bhqk', q_blk, k).astype(jnp.float32) * scale
        q_idx = start + jnp.arange(ATTN_BLOCK)
        mask = key_idx[None, :] <= q_idx[:, None]
        sc = jnp.where(mask[None, None], sc, -1e30)
        p = jax.nn.softmax(sc, axis=-1).astype(v.dtype)
        return jnp.einsum('bhqk,bkhe->bqhe', p, v)

    o = lax.map(attend, (qb, jnp.arange(nb) * ATTN_BLOCK))
    return o.transpose(1, 0, 2, 3, 4).reshape(bsz, s, MLA_HEADS * MLA_V)


def mixer_sublayer(h, w_in, lru_conv_w, lru_conv_b, lru_wa, lru_ba, lru_wx, lru_bx, lru_lambda, lru_wo,
                   ret_wo, mla_q_norm_g, mla_w_uq, mla_kv_norm_g, mla_w_ukv, mla_wo, w_out,
                   cos_r, sin_r, cos_m, sin_m):
    bsz, s, _ = h.shape
    t = h @ w_in
    lru_x, lru_g, ret_q, ret_k, ret_v, ret_g, mla_cq, mla_ckv, mla_kr, gate_logits = split_cols(t)
    y_lru = rglru_branch(lru_x, lru_g, lru_conv_w, lru_conv_b, lru_wa, lru_ba, lru_wx, lru_bx, lru_lambda) @ lru_wo
    y_ret = retention_branch(ret_q, ret_k, ret_v, ret_g, cos_r, sin_r) @ ret_wo
    y_mla = mla_branch(mla_cq, mla_ckv, mla_kr, mla_q_norm_g, mla_w_uq, mla_kv_norm_g, mla_w_ukv, cos_m, sin_m) @ mla_wo
    gates = jax.nn.sigmoid(gate_logits).reshape(bsz, s, N_BRANCH, D_MODEL)
    merged = gates[:, :, 0] * y_lru + gates[:, :, 1] * y_ret + gates[:, :, 2] * y_mla
    return merged @ w_out


def conv_glu(h, w_up, conv_w, conv_b, w_down):
    up = causal_dwconv(h @ w_up, conv_w, conv_b)
    u, g = up[..., :D_FF], up[..., D_FF:]
    return (jax.nn.gelu(g) * u) @ w_down


def setup_inputs(seed: int = 0) -> dict:
    key = jax.random.key(seed)
    ks = jax.random.split(key, 32)
    f32 = jnp.float32
    L = DEPTH

    def dense(k, shape, fan_in, gain=1.0):
        return jax.random.normal(k, shape, f32) * (gain * fan_in ** -0.5)

    def gain(k, shape):
        return 1.0 + 0.05 * jax.random.normal(k, shape, f32)

    def bias(k, shape):
        return 0.02 * jax.random.normal(k, shape, f32)

    offs = jax.random.randint(ks[2], (BATCH, 1), 0, 4096, dtype=jnp.int32)
    positions = (offs + jnp.arange(SEQ, dtype=jnp.int32)[None, :]).astype(jnp.int32)
    a_c = jax.random.uniform(ks[12], (L, LRU_WIDTH), f32, minval=0.9, maxval=0.999)
    a_base = a_c ** (1.0 / LRU_C)
    lru_lambda = jnp.log(a_base) - jnp.log1p(-a_base)
    return {
        "x": jax.random.normal(ks[0], (BATCH, SEQ, D_MODEL), f32),
        "c": jax.random.normal(ks[1], (BATCH, D_MODEL), f32),
        "positions": positions,
        "ada_w": dense(ks[3], (L, D_MODEL, 6 * D_MODEL), D_MODEL, 0.5),
        "ada_b": bias(ks[4], (L, 6 * D_MODEL)),
        "mix_pre_g": gain(ks[5], (L, D_MODEL)),
        "mix_post_g": gain(ks[6], (L, D_MODEL)),
        "w_in": dense(ks[7], (L, D_MODEL, N_IN), D_MODEL),
        "lru_conv_w": dense(ks[8], (L, LRU_CONV, LRU_WIDTH), LRU_CONV),
        "lru_conv_b": bias(ks[9], (L, LRU_WIDTH)),
        "lru_wa": dense(ks[10], (L, LRU_BLOCKS, LRU_BLOCK, LRU_BLOCK), LRU_BLOCK),
        "lru_ba": bias(ks[11], (L, LRU_WIDTH)),
        "lru_wx": dense(ks[13], (L, LRU_BLOCKS, LRU_BLOCK, LRU_BLOCK), LRU_BLOCK),
        "lru_bx": bias(ks[14], (L, LRU_WIDTH)),
        "lru_lambda": lru_lambda,
        "lru_wo": dense(ks[15], (L, LRU_WIDTH, D_MODEL), LRU_WIDTH),
        "ret_wo": dense(ks[16], (L, RET_HEADS * RET_DV, D_MODEL), RET_HEADS * RET_DV),
        "mla_q_norm_g": gain(ks[17], (L, MLA_Q_RANK)),
        "mla_w_uq": dense(ks[18], (L, MLA_Q_RANK, MLA_HEADS * (MLA_NOPE + MLA_ROPE)), MLA_Q_RANK),
        "mla_kv_norm_g": gain(ks[19], (L, MLA_KV_RANK)),
        "mla_w_ukv": dense(ks[20], (L, MLA_KV_RANK, MLA_HEADS * (MLA_NOPE + MLA_V)), MLA_KV_RANK),
        "mla_wo": dense(ks[21], (L, MLA_HEADS * MLA_V, D_MODEL), MLA_HEADS * MLA_V),
        "w_out": dense(ks[22], (L, D_MODEL, D_MODEL), D_MODEL),
        "ffn_pre_g": gain(ks[23], (L, D_MODEL)),
        "ffn_post_g": gain(ks[24], (L, D_MODEL)),
        "ffn_w_up": dense(ks[25], (L, D_MODEL, 2 * D_FF), D_MODEL),
        "ffn_conv_w": dense(ks[26], (L, FFN_CONV, 2 * D_FF), FFN_CONV),
        "ffn_conv_b": bias(ks[27], (L, 2 * D_FF)),
        "ffn_w_down": dense(ks[28], (L, D_FF, D_MODEL), D_FF),
    }


def reference(x, c, positions, ada_w, ada_b, mix_pre_g, mix_post_g, w_in, lru_conv_w, lru_conv_b,
              lru_wa, lru_ba, lru_wx, lru_bx, lru_lambda, lru_wo, ret_wo, mla_q_norm_g, mla_w_uq,
              mla_kv_norm_g, mla_w_ukv, mla_wo, w_out, ffn_pre_g, ffn_post_g, ffn_w_up, ffn_conv_w,
              ffn_conv_b, ffn_w_down):
    cos_r, sin_r = rope_tables(positions, RET_DK)
    cos_m, sin_m = rope_tables(positions, MLA_ROPE)
    c_act = jax.nn.silu(c)
    for l in range(DEPTH):
        mod = c_act @ ada_w[l] + ada_b[l]
        sh_m, sc_m, g_m, sh_f, sc_f, g_f = jnp.split(mod, 6, axis=-1)
        h = modulate(rmsnorm(x, mix_pre_g[l]), sh_m, sc_m)
        y = mixer_sublayer(h, w_in[l], lru_conv_w[l], lru_conv_b[l], lru_wa[l], lru_ba[l], lru_wx[l],
                           lru_bx[l], lru_lambda[l], lru_wo[l], ret_wo[l], mla_q_norm_g[l], mla_w_uq[l],
                           mla_kv_norm_g[l], mla_w_ukv[l], mla_wo[l], w_out[l],
                           cos_r, sin_r, cos_m, sin_m)
        x = x + g_m[:, None, :] * rmsnorm(y, mix_post_g[l])
        h = modulate(rmsnorm(x, ffn_pre_g[l]), sh_f, sc_f)
        y = conv_glu(h, ffn_w_up[l], ffn_conv_w[l], ffn_conv_b[l], ffn_w_down[l])
        x = x + g_f[:, None, :] * rmsnorm(y, ffn_post_g[l])
    return x
```

```python
import functools

import jax
import jax.numpy as jnp
import numpy as np
from jax import lax
from jax.experimental import pallas as pl
from jax.experimental.pallas import tpu as pltpu

F32 = jnp.float32
BF16 = jnp.bfloat16

D_MODEL = 1024
DEPTH = 2
EPS = 1e-6
ROPE_THETA = 10000.0
LRU_WIDTH = 512
LRU_BLOCKS = 8
LRU_BLOCK = LRU_WIDTH // LRU_BLOCKS
LRU_CONV = 4
LRU_C = 8.0
RET_HEADS = 8
RET_DK = 64
RET_DV = 64
RET_CHUNK = 128
MLA_HEADS = 8
MLA_Q_RANK = 384
MLA_KV_RANK = 256
MLA_NOPE = 64
MLA_ROPE = 32
MLA_V = 64
D_FF = 2816
FFN_CONV = 3

LANES = 128
SUBLANES = 8
VMEM_LIMIT = 56 * 1024 * 1024

COL_LRU = 0
COL_RET = 1024
COL_GATE = 3072
COL_MLA = 6144
N_IN_PACKED = 6912
MLA_PACK = 768
KR_LANE = 64


def _params(*sem):
    return pltpu.CompilerParams(dimension_semantics=sem, vmem_limit_bytes=VMEM_LIMIT)


def _gelu_tanh(x):
    return 0.5 * x * (1.0 + jnp.tanh(0.7978845608028654 * (x + 0.044715 * (x * x * x))))


def _sigmoid(x):
    return 1.0 / (1.0 + jnp.exp(-x))


def _rms(x, g):
    return x * lax.rsqrt(jnp.mean(x * x, axis=-1, keepdims=True) + EPS) * g


def _ada_kernel(c_ref, w_ref, b_ref, o_ref):
    c = c_ref[...]
    ca = c * _sigmoid(c)
    o_ref[...] = jnp.dot(ca, w_ref[...], preferred_element_type=F32,
                         precision=lax.Precision.HIGHEST) + b_ref[...]


def _ada(c_pad, ada_w, ada_b):
    depth, d, n = ada_w.shape
    rows = c_pad.shape[0]
    tn = 1536
    return pl.pallas_call(
        _ada_kernel,
        grid=(depth, n // tn),
        in_specs=[
            pl.BlockSpec((rows, d), lambda l, j: (0, 0)),
            pl.BlockSpec((None, d, tn), lambda l, j: (l, 0, j)),
            pl.BlockSpec((None, 1, tn), lambda l, j: (l, 0, j)),
        ],
        out_specs=pl.BlockSpec((None, rows, tn), lambda l, j: (l, 0, j)),
        out_shape=jax.ShapeDtypeStruct((depth, rows, n), F32),
        compiler_params=_params("parallel", "parallel"),
        name="ada_mod",
    )(c_pad, ada_w, ada_b.reshape(depth, 1, n))


def _rope_kernel(pos_ref, invr_ref, sgnr_ref, invm_ref, sgnm_ref, cr_ref, sr_ref, cm_ref, sm_ref):
    pos = pos_ref[...].astype(F32)
    ang_r = pos * invr_ref[...]
    cr_ref[...] = jnp.cos(ang_r)
    sr_ref[...] = jnp.sin(ang_r) * sgnr_ref[...]
    ang_m = pos * invm_ref[...]
    cm_ref[...] = jnp.cos(ang_m)
    sm_ref[...] = jnp.sin(ang_m) * sgnm_ref[...]


def _rope_tables(positions):
    bsz, s = positions.shape
    ts = 1024
    inv_r = ROPE_THETA ** (-jnp.arange(0, RET_DK, 2, dtype=F32) / RET_DK)
    inv_m = ROPE_THETA ** (-jnp.arange(0, MLA_ROPE, 2, dtype=F32) / MLA_ROPE)
    half_r = RET_DK // 2
    half_m = MLA_ROPE // 2
    invr = jnp.tile(inv_r, LANES // half_r).reshape(1, LANES)
    sgnr = jnp.tile(jnp.concatenate([-jnp.ones(half_r, F32), jnp.ones(half_r, F32)]), LANES // RET_DK)
    zeros = jnp.zeros(MLA_NOPE, F32)
    tail = jnp.zeros(LANES - MLA_NOPE - MLA_ROPE, F32)
    invm = jnp.concatenate([zeros, inv_m, inv_m, tail]).reshape(1, LANES)
    sgnm = jnp.concatenate([zeros, -jnp.ones(half_m, F32), jnp.ones(half_m, F32), tail])
    row = pl.BlockSpec((1, LANES), lambda b, i: (0, 0))
    tab = pl.BlockSpec((None, ts, LANES), lambda b, i: (b, i, 0))
    shp = jax.ShapeDtypeStruct((bsz, s, LANES), F32)
    return pl.pallas_call(
        _rope_kernel,
        grid=(bsz, s // ts),
        in_specs=[pl.BlockSpec((None, ts, 1), lambda b, i: (b, i, 0)), row, row, row, row],
        out_specs=[tab, tab, tab, tab],
        out_shape=[shp, shp, shp, shp],
        compiler_params=_params("parallel", "parallel"),
        name="rope_tables",
    )(positions.reshape(bsz, s, 1), invr, sgnr.reshape(1, LANES), invm, sgnm.reshape(1, LANES))


def _prenorm_matmul_kernel(x_ref, g_ref, sh_ref, sc_ref, w_ref, o_ref, h_ref):
    @pl.when(pl.program_id(2) == 0)
    def _():
        h = _rms(x_ref[...], g_ref[...]) * (1.0 + sc_ref[...]) + sh_ref[...]
        h_ref[...] = h.astype(BF16)

    o_ref[...] = jnp.dot(h_ref[...], w_ref[...], preferred_element_type=F32).astype(o_ref.dtype)


def _prenorm_matmul(x, g, shift, scale, w, tn, name):
    bsz, s, d = x.shape
    n = w.shape[1]
    tm = 1024
    vec = pl.BlockSpec((None, 1, d), lambda b, i, j: (b, 0, 0))
    return pl.pallas_call(
        _prenorm_matmul_kernel,
        grid=(bsz, s // tm, n // tn),
        in_specs=[
            pl.BlockSpec((None, tm, d), lambda b, i, j: (b, i, 0)),
            pl.BlockSpec((1, d), lambda b, i, j: (0, 0)),
            vec, vec,
            pl.BlockSpec((d, tn), lambda b, i, j: (0, j)),
        ],
        out_specs=pl.BlockSpec((None, tm, tn), lambda b, i, j: (b, i, j)),
        out_shape=jax.ShapeDtypeStruct((bsz, s, n), BF16),
        scratch_shapes=[pltpu.VMEM((tm, d), BF16)],
        compiler_params=_params("parallel", "parallel", "arbitrary"),
        name=name,
    )(x, g.reshape(1, d), shift, scale, w)


def _lru_kernel(t_ref, cw_ref, cb_ref, wbd_ref, bb_ref, lam_ref, o_ref, xbuf_ref, hc_ref, *, ts):
    w = LRU_WIDTH

    @pl.when(pl.program_id(1) == 0)
    def _():
        xbuf_ref[0:SUBLANES, :] = jnp.zeros((SUBLANES, w), F32)
        hc_ref[...] = jnp.zeros((1, w), F32)

    xb = t_ref[:, 0:w].astype(F32)
    gb = t_ref[:, w:2 * w].astype(F32)
    xbuf_ref[SUBLANES:SUBLANES + ts, :] = xb
    xc = xb * cw_ref[LRU_CONV - 1:LRU_CONV, :] + cb_ref[...]
    for k in range(LRU_CONV - 1):
        back = LRU_CONV - 1 - k
        xc = xc + xbuf_ref[pl.ds(SUBLANES - back, ts), :] * cw_ref[k:k + 1, :]
    xbuf_ref[0:SUBLANES, :] = xb[ts - SUBLANES:, :]

    z = jnp.dot(xc.astype(BF16), wbd_ref[...], preferred_element_type=F32) + bb_ref[...]
    r = _sigmoid(z[:, :w])
    ig = _sigmoid(z[:, w:])
    nl = -lam_ref[...]
    softplus = jnp.maximum(nl, 0.0) + jnp.log1p(jnp.exp(-jnp.abs(nl)))
    log_a = (-LRU_C) * r * softplus
    a = jnp.exp(log_a)
    th = jnp.tanh(log_a)
    u = jnp.sqrt(-2.0 * th / (1.0 - th)) * (ig * xc)

    row = lax.broadcasted_iota(jnp.int32, (ts, w), 0)
    k = 1
    while k < ts:
        keep = row >= k
        a_prev = jnp.where(keep, pltpu.roll(a, k, 0), 1.0)
        u_prev = jnp.where(keep, pltpu.roll(u, k, 0), 0.0)
        u = a * u_prev + u
        a = a * a_prev
        k *= 2
    h = a * hc_ref[...] + u
    hc_ref[...] = h[ts - 1:ts, :]
    o_ref[...] = (h * _gelu_tanh(gb)).astype(o_ref.dtype)


def _lru(t, conv_w, conv_b, wbd, bb, lam):
    bsz, s, _ = t.shape
    ts = 256
    w = LRU_WIDTH
    const = lambda shape: pl.BlockSpec(shape, lambda b, i: (0,) * len(shape))
    return pl.pallas_call(
        functools.partial(_lru_kernel, ts=ts),
        grid=(bsz, s // ts),
        in_specs=[
            pl.BlockSpec((None, ts, 2 * w), lambda b, i: (b, i, COL_LRU // (2 * w))),
            const((LRU_CONV, w)), const((1, w)), const((w, 2 * w)), const((1, 2 * w)), const((1, w)),
        ],
        out_specs=pl.BlockSpec((None, ts, w), lambda b, i: (b, i, 0)),
        out_shape=jax.ShapeDtypeStruct((bsz, s, w), BF16),
        scratch_shapes=[pltpu.VMEM((ts + SUBLANES, w), F32), pltpu.VMEM((1, w), F32)],
        compiler_params=_params("parallel", "arbitrary"),
        name="lru_mixer",
    )(t, conv_w, conv_b.reshape(1, w), wbd, bb, lam.reshape(1, w))


def _ret_kernel(q_ref, k_ref, v_ref, g_ref, cos_ref, sin_ref, dec_ref, qdec_ref, kvdec_ref, cdec_ref,
                bd_ref, o_ref, st_ref, *, n_chunks):
    c_len = RET_CHUNK
    pairs = RET_HEADS // 2

    @pl.when(pl.program_id(1) == 0)
    def _():
        st_ref[...] = jnp.zeros(st_ref.shape, F32)

    lane = lax.broadcasted_iota(jnp.int32, (c_len, LANES), 1)
    head0 = lane < RET_DK
    first_half = (lane % RET_DK) < (RET_DK // 2)
    inv_n = 1.0 / RET_DV

    def rope(x, cos, sin):
        swapped = jnp.where(first_half, pltpu.roll(x, LANES - RET_DK // 2, 1), pltpu.roll(x, RET_DK // 2, 1))
        return x * cos + swapped * sin

    def head_mean(x):
        m0 = jnp.sum(jnp.where(head0, x, 0.0), axis=-1, keepdims=True) * inv_n
        m1 = jnp.sum(jnp.where(head0, 0.0, x), axis=-1, keepdims=True) * inv_n
        return jnp.where(head0, m0, m1)

    nt = (((1,), (1,)), ((), ()))
    tn = (((0,), (0,)), ((), ()))
    for c in range(n_chunks):
        rows = slice(c * c_len, (c + 1) * c_len)
        cos = cos_ref[rows, :]
        sin = sin_ref[rows, :]
        for p in range(pairs):
            cols = slice(p * LANES, (p + 1) * LANES)
            q = rope(q_ref[rows, cols].astype(F32), cos, sin)
            k = rope(k_ref[rows, cols].astype(F32), cos, sin) * (RET_DK ** -0.5)
            v = v_ref[rows, cols]
            qb = q.astype(BF16)
            kb = k.astype(BF16)
            zero = jnp.zeros_like(qb)
            s0 = lax.dot_general(jnp.where(head0, qb, zero), kb, nt, preferred_element_type=F32)
            s1 = lax.dot_general(jnp.where(head0, zero, qb), kb, nt, preferred_element_type=F32)
            probs = jnp.concatenate([s0 * dec_ref[2 * p], s1 * dec_ref[2 * p + 1]], axis=1).astype(BF16)
            v2 = jnp.concatenate([jnp.where(head0, v, zero), jnp.where(head0, zero, v)], axis=0)
            y = jnp.dot(probs, v2, preferred_element_type=F32)
            state = st_ref[p]
            y = y + jnp.dot(qb, state.astype(BF16), preferred_element_type=F32) * qdec_ref[p]
            vd = (v.astype(F32) * kvdec_ref[p]).astype(BF16)
            kv = lax.dot_general(kb, vd, tn, preferred_element_type=F32)
            st_ref[p] = state * cdec_ref[p] + kv * bd_ref[...]

            d = y - head_mean(y)
            yn = d * lax.rsqrt(head_mean(d * d) + EPS)
            g = g_ref[rows, cols].astype(F32)
            o_ref[rows, cols] = (g * _sigmoid(g) * yn).astype(o_ref.dtype)


def _ret_constants():
    f32 = F32
    log_gamma = jnp.log1p(-(2.0 ** (-5.0 - jnp.arange(RET_HEADS, dtype=f32))))
    idx = jnp.arange(RET_CHUNK, dtype=f32)
    diff = idx[:, None] - idx[None, :]
    causal = diff >= 0
    inner = jnp.where(causal[None], jnp.exp(jnp.where(causal, diff, 0.0)[None] * log_gamma[:, None, None]), 0.0)
    kv_decay = jnp.exp((RET_CHUNK - 1.0 - idx)[None, :] * log_gamma[:, None])
    q_decay = jnp.exp((idx + 1.0)[:, None] * log_gamma[None, :])
    chunk_decay = jnp.exp(RET_CHUNK * log_gamma)
    pairs = RET_HEADS // 2

    def by_lane(per_head):
        rows = per_head.shape[0]
        return jnp.repeat(per_head.reshape(rows, pairs, 2), RET_DK, axis=2).reshape(rows, pairs, LANES).transpose(1, 0, 2)

    qdec = by_lane(q_decay)
    kvdec = by_lane(kv_decay.T)
    cdec = by_lane(chunk_decay[None, :])
    lane_head = jnp.arange(LANES) // RET_DK
    bd = (lane_head[:, None] == lane_head[None, :]).astype(f32)
    return inner, qdec, kvdec, cdec, bd


def _ret(t, cos_r, sin_r):
    bsz, s, _ = t.shape
    tc = 512
    w = RET_HEADS * RET_DK
    inner, qdec, kvdec, cdec, bd = _ret_constants()
    base = COL_RET // w
    col = lambda j: pl.BlockSpec((None, tc, w), lambda b, i: (b, i, base + j))
    tab = pl.BlockSpec((None, tc, LANES), lambda b, i: (b, i, 0))
    const = lambda shape: pl.BlockSpec(shape, lambda b, i: (0,) * len(shape))
    return pl.pallas_call(
        functools.partial(_ret_kernel, n_chunks=tc // RET_CHUNK),
        grid=(bsz, s // tc),
        in_specs=[col(0), col(1), col(2), col(3), tab, tab,
                  const(inner.shape), const(qdec.shape), const(kvdec.shape), const(cdec.shape), const(bd.shape)],
        out_specs=pl.BlockSpec((None, tc, w), lambda b, i: (b, i, 0)),
        out_shape=jax.ShapeDtypeStruct((bsz, s, w), BF16),
        scratch_shapes=[pltpu.VMEM((RET_HEADS // 2, LANES, LANES), F32)],
        compiler_params=_params("parallel", "arbitrary"),
        name="ret_mixer",
    )(t, t, t, t, cos_r, sin_r, inner, qdec, kvdec, cdec, bd)


def _mla_proj_kernel(t_ref, gq_ref, gkv_ref, wq_ref, wkv_ref, cos_ref, sin_ref, q_ref, k_ref, v_ref):
    ts = t_ref.shape[0]
    cos = cos_ref[...]
    sin = sin_ref[...]
    lane = lax.broadcasted_iota(jnp.int32, (ts, LANES), 1)
    low_half = lane < (MLA_NOPE + MLA_ROPE // 2)
    half = MLA_ROPE // 2

    def rope(x):
        swapped = jnp.where(low_half, pltpu.roll(x, LANES - half, 1), pltpu.roll(x, half, 1))
        return x * cos + swapped * sin

    ckv = t_ref[:, 0:MLA_KV_RANK].astype(F32)
    kr = t_ref[:, MLA_KV_RANK:MLA_KV_RANK + LANES].astype(F32)
    cq = t_ref[:, MLA_PACK - MLA_Q_RANK:MLA_PACK].astype(F32)

    q = jnp.dot(_rms(cq, gq_ref[...]).astype(BF16), wq_ref[...], preferred_element_type=F32)
    scale = (MLA_NOPE + MLA_ROPE) ** -0.5
    for h in range(MLA_HEADS):
        cols = slice(h * LANES, (h + 1) * LANES)
        q_ref[:, cols] = (rope(q[:, cols]) * scale).astype(q_ref.dtype)

    kvu = jnp.dot(_rms(ckv, gkv_ref[...]).astype(BF16), wkv_ref[...], preferred_element_type=F32)
    k_rope = rope(kr)
    for h in range(MLA_HEADS):
        cols = slice(h * LANES, (h + 1) * LANES)
        k_ref[:, cols] = (kvu[:, cols] + k_rope).astype(k_ref.dtype)
    v_ref[...] = kvu[:, MLA_HEADS * LANES:].astype(v_ref.dtype)


def _mla_proj(t, gq, gkv, wq, wkv, cos_m, sin_m):
    bsz, s, _ = t.shape
    ts = 512
    hq = MLA_HEADS * LANES
    hv = MLA_HEADS * MLA_V
    const = lambda shape: pl.BlockSpec(shape, lambda b, i: (0,) * len(shape))
    tab = pl.BlockSpec((None, ts, LANES), lambda b, i: (b, i, 0))
    out = lambda n: pl.BlockSpec((None, ts, n), lambda b, i: (b, i, 0))
    return pl.pallas_call(
        _mla_proj_kernel,
        grid=(bsz, s // ts),
        in_specs=[pl.BlockSpec((None, ts, MLA_PACK), lambda b, i: (b, i, COL_MLA // MLA_PACK)),
                  const((1, MLA_Q_RANK)), const((1, MLA_KV_RANK)), const(wq.shape), const(wkv.shape), tab, tab],
        out_specs=[out(hq), out(hq), out(hv)],
        out_shape=[jax.ShapeDtypeStruct((bsz, s, hq), BF16), jax.ShapeDtypeStruct((bsz, s, hq), BF16),
                   jax.ShapeDtypeStruct((bsz, s, hv), BF16)],
        compiler_params=_params("parallel", "parallel"),
        name="mla_proj",
    )(t, gq.reshape(1, -1), gkv.reshape(1, -1), wq, wkv, cos_m, sin_m)


def _flash_kernel(q_ref, k_ref, v_ref, o_ref, acc_ref, *, tq):
    i = pl.program_id(2)
    nt = (((1,), (1,)), ((), ()))
    acc_ref[...] = jnp.zeros(acc_ref.shape, F32)
    row = lax.broadcasted_iota(jnp.int32, (tq, tq), 0)
    col = lax.broadcasted_iota(jnp.int32, (tq, tq), 1)
    causal = col <= row

    def step(j, carry, masked):
        ms, ls = carry
        off = pl.multiple_of(j * tq, tq)
        kb = k_ref[pl.ds(off, tq), :]
        vb = v_ref[pl.ds(off, tq), :]
        new_m, new_l = [], []
        for h in range(2):
            cols = slice(h * LANES, (h + 1) * LANES)
            s = lax.dot_general(q_ref[:, cols], kb[:, cols], nt, preferred_element_type=F32)
            if masked:
                s = jnp.where(causal, s, -1e30)
            m_new = jnp.maximum(ms[h], jnp.max(s, axis=-1, keepdims=True))
            alpha = jnp.exp(ms[h] - m_new)
            p = jnp.exp(s - m_new)
            new_l.append(alpha * ls[h] + jnp.sum(p, axis=-1, keepdims=True))
            new_m.append(m_new)
            acc_ref[h] = alpha * acc_ref[h] + jnp.dot(p.astype(BF16), vb, preferred_element_type=F32)
        return tuple(new_m), tuple(new_l)

    m0 = jnp.full((tq, 1), -1e30, F32)
    l0 = jnp.zeros((tq, 1), F32)
    carry = lax.fori_loop(0, i, functools.partial(step, masked=False), ((m0, m0), (l0, l0)))
    _, ls = step(i, carry, True)
    lane = lax.broadcasted_iota(jnp.int32, (tq, LANES), 1)
    o = jnp.where(lane < MLA_V, acc_ref[0] / ls[0], acc_ref[1] / ls[1])
    o_ref[...] = o.astype(o_ref.dtype)


def _flash(q, k, v):
    bsz, s, _ = q.shape
    tq = 512
    pairs = MLA_HEADS // 2
    return pl.pallas_call(
        functools.partial(_flash_kernel, tq=tq),
        grid=(bsz, pairs, s // tq),
        in_specs=[
            pl.BlockSpec((None, tq, 2 * LANES), lambda b, p, i: (b, i, p)),
            pl.BlockSpec((None, s, 2 * LANES), lambda b, p, i: (b, 0, p)),
            pl.BlockSpec((None, s, LANES), lambda b, p, i: (b, 0, p)),
        ],
        out_specs=pl.BlockSpec((None, tq, LANES), lambda b, p, i: (b, i, p)),
        out_shape=jax.ShapeDtypeStruct((bsz, s, MLA_HEADS * MLA_V), BF16),
        scratch_shapes=[pltpu.VMEM((2, tq, LANES), F32)],
        compiler_params=_params("parallel", "parallel", "arbitrary"),
        name="mla_flash",
    )(q, k, v)


def _merge_kernel(yl_ref, yr_ref, ym_ref, gt_ref, x_ref, wl_ref, wr_ref, wm_ref, wo_ref, gp_ref, gm_ref, o_ref):
    d = D_MODEL
    merged = _sigmoid(gt_ref[:, 0:d].astype(F32)) * jnp.dot(yl_ref[...], wl_ref[...], preferred_element_type=F32)
    merged = merged + _sigmoid(gt_ref[:, d:2 * d].astype(F32)) * jnp.dot(
        yr_ref[...], wr_ref[...], preferred_element_type=F32)
    merged = merged + _sigmoid(gt_ref[:, 2 * d:3 * d].astype(F32)) * jnp.dot(
        ym_ref[...], wm_ref[...], preferred_element_type=F32)
    y = jnp.dot(merged.astype(BF16), wo_ref[...], preferred_element_type=F32)
    o_ref[...] = x_ref[...] + gm_ref[...] * _rms(y, gp_ref[...])


def _merge(y_lru, y_ret, y_mla, t, x, wl, wr, wm, wo, g_post, g_mod):
    bsz, s, d = x.shape
    tm = 512
    w = y_lru.shape[-1]
    br = pl.BlockSpec((None, tm, w), lambda b, i: (b, i, 0))
    const = lambda shape: pl.BlockSpec(shape, lambda b, i: (0,) * len(shape))
    return pl.pallas_call(
        _merge_kernel,
        grid=(bsz, s // tm),
        in_specs=[br, br, br,
                  pl.BlockSpec((None, tm, 3 * d), lambda b, i: (b, i, COL_GATE // (3 * d))),
                  pl.BlockSpec((None, tm, d), lambda b, i: (b, i, 0)),
                  const(wl.shape), const(wr.shape), const(wm.shape), const(wo.shape), const((1, d)),
                  pl.BlockSpec((None, 1, d), lambda b, i: (b, 0, 0))],
        out_specs=pl.BlockSpec((None, tm, d), lambda b, i: (b, i, 0)),
        out_shape=jax.ShapeDtypeStruct((bsz, s, d), F32),
        compiler_params=_params("parallel", "parallel"),
        name="mixer_merge",
    )(y_lru, y_ret, y_mla, t, x, wl, wr, wm, wo, g_post.reshape(1, d), g_mod)


def _ffn_down_kernel(up_ref, halo_ref, cw_ref, cb_ref, wd_ref, x_ref, gp_ref, gf_ref, o_ref, act_ref, *, tm, cw):
    first = pl.program_id(1) == 0
    row = lax.broadcasted_iota(jnp.int32, (tm, cw), 0)
    row0 = row == 0
    row1 = row == 1

    def conv(c0):
        cols = slice(c0, c0 + cw)
        xv = up_ref[:, cols].astype(F32)
        hl = jnp.where(first, 0.0, halo_ref[:, cols].astype(F32))
        h1 = hl[SUBLANES - 1:SUBLANES, :]
        h2 = hl[SUBLANES - 2:SUBLANES - 1, :]
        xm1 = jnp.where(row0, h1, pltpu.roll(xv, 1, 0))
        xm2 = jnp.where(row0, h2, jnp.where(row1, h1, pltpu.roll(xv, 2, 0)))
        y = xv * cw_ref[2:3, cols] + cb_ref[:, cols]
        y = y + xm2 * cw_ref[0:1, cols]
        return y + xm1 * cw_ref[1:2, cols]

    for c in range(D_FF // cw):
        u = conv(c * cw)
        g = conv(D_FF + c * cw)
        act_ref[:, c * cw:(c + 1) * cw] = (_gelu_tanh(g) * u).astype(BF16)
    y = jnp.dot(act_ref[...], wd_ref[...], preferred_element_type=F32)
    o_ref[...] = x_ref[...] + gf_ref[...] * _rms(y, gp_ref[...])


def _ffn_down(up, conv_w, conv_b, wd, x, g_post, g_mod):
    bsz, s, d = x.shape
    tm = 512
    n = up.shape[-1]
    per = tm // SUBLANES
    const = lambda shape: pl.BlockSpec(shape, lambda b, i: (0,) * len(shape))
    return pl.pallas_call(
        functools.partial(_ffn_down_kernel, tm=tm, cw=256),
        grid=(bsz, s // tm),
        in_specs=[pl.BlockSpec((None, tm, n), lambda b, i: (b, i, 0)),
                  pl.BlockSpec((None, SUBLANES, n), lambda b, i: (b, jnp.maximum(i * per - 1, 0), 0)),
                  const((FFN_CONV, n)), const((1, n)), const(wd.shape),
                  pl.BlockSpec((None, tm, d), lambda b, i: (b, i, 0)),
                  const((1, d)),
                  pl.BlockSpec((None, 1, d), lambda b, i: (b, 0, 0))],
        out_specs=pl.BlockSpec((None, tm, d), lambda b, i: (b, i, 0)),
        out_shape=jax.ShapeDtypeStruct((bsz, s, d), F32),
        scratch_shapes=[pltpu.VMEM((tm, D_FF), BF16)],
        compiler_params=_params("parallel", "arbitrary"),
        name="ffn_down",
    )(up, up, conv_w, conv_b.reshape(1, n), wd, x, g_post.reshape(1, d), g_mod)


def _pack_w_in(w):
    d = w.shape[0]
    o_cq = 2 * LRU_WIDTH + 4 * RET_HEADS * RET_DK
    o_ckv = o_cq + MLA_Q_RANK
    o_kr = o_ckv + MLA_KV_RANK
    o_gate = o_kr + MLA_ROPE
    z = lambda n: jnp.zeros((d, n), w.dtype)
    return jnp.concatenate([
        w[:, :o_cq], w[:, o_gate:], w[:, o_ckv:o_kr], z(KR_LANE), w[:, o_kr:o_gate],
        z(LANES - KR_LANE - MLA_ROPE), w[:, o_cq:o_ckv]], axis=1).astype(BF16)


def _pack_w_uq(w):
    r = w.shape[0]
    dqk = MLA_NOPE + MLA_ROPE
    w3 = w.reshape(r, MLA_HEADS, dqk)
    return jnp.pad(w3, ((0, 0), (0, 0), (0, LANES - dqk))).reshape(r, MLA_HEADS * LANES).astype(BF16)


def _pack_w_ukv(w):
    r = w.shape[0]
    w3 = w.reshape(r, MLA_HEADS, MLA_NOPE + MLA_V)
    k = jnp.pad(w3[:, :, :MLA_NOPE], ((0, 0), (0, 0), (0, LANES - MLA_NOPE))).reshape(r, MLA_HEADS * LANES)
    v = w3[:, :, MLA_NOPE:].reshape(r, MLA_HEADS * MLA_V)
    return jnp.concatenate([k, v], axis=1).astype(BF16)


def _block_diag(w):
    nb, n, _ = w.shape
    eye = jnp.eye(nb, dtype=w.dtype)
    return (eye[:, None, :, None] * w[:, :, None, :]).reshape(nb * n, nb * n)


def kernel(x, c, positions, ada_w, ada_b, mix_pre_g, mix_post_g, w_in, lru_conv_w, lru_conv_b, lru_wa, lru_ba, lru_wx, lru_bx, lru_lambda, lru_wo, ret_wo, mla_q_norm_g, mla_w_uq, mla_kv_norm_g, mla_w_ukv, mla_wo, w_out, ffn_pre_g, ffn_post_g, ffn_w_up, ffn_conv_w, ffn_conv_b, ffn_w_down):
    bsz, s, d = x.shape
    c_pad = jnp.pad(c, ((0, SUBLANES - bsz), (0, 0)))
    mod = _ada(c_pad, ada_w, ada_b)[:, :bsz]
    cos_r, sin_r, cos_m, sin_m = _rope_tables(positions)

    for l in range(DEPTH):
        sh_m, sc_m, g_m, sh_f, sc_f, g_f = [mod[l, :, i * d:(i + 1) * d].reshape(bsz, 1, d) for i in range(6)]
        t = _prenorm_matmul(x, mix_pre_g[l], sh_m, sc_m, _pack_w_in(w_in[l]), 2304, "mixer_in_proj")
        wbd = jnp.concatenate([_block_diag(lru_wa[l]), _block_diag(lru_wx[l])], axis=1).astype(BF16)
        bb = jnp.concatenate([lru_ba[l], lru_bx[l]]).reshape(1, 2 * LRU_WIDTH)
        y_lru = _lru(t, lru_conv_w[l], lru_conv_b[l], wbd, bb, lru_lambda[l])
        y_ret = _ret(t, cos_r, sin_r)
        q, k, v = _mla_proj(t, mla_q_norm_g[l], mla_kv_norm_g[l], _pack_w_uq(mla_w_uq[l]),
                            _pack_w_ukv(mla_w_ukv[l]), cos_m, sin_m)
        y_mla = _flash(q, k, v)
        x = _merge(y_lru, y_ret, y_mla, t, x, lru_wo[l].astype(BF16), ret_wo[l].astype(BF16),
                   mla_wo[l].astype(BF16), w_out[l].astype(BF16), mix_post_g[l], g_m)
        up = _prenorm_matmul(x, ffn_pre_g[l], sh_f, sc_f, ffn_w_up[l].astype(BF16), 2816, "ffn_up_proj")
        x = _ffn_down(up, ffn_conv_w[l], ffn_conv_b[l], ffn_w_down[l].astype(BF16), x, ffn_post_g[l], g_f)
    return x
```

```python
import functools

import jax
import jax.numpy as jnp
import numpy as np
from jax import lax
from jax.experimental import pallas as pl
from jax.experimental.pallas import tpu as pltpu

F32 = jnp.float32
BF16 = jnp.bfloat16

D_MODEL = 1024
DEPTH = 2
EPS = 1e-6
ROPE_THETA = 10000.0
LRU_WIDTH = 512
LRU_BLOCKS = 8
LRU_BLOCK = LRU_WIDTH // LRU_BLOCKS
LRU_CONV = 4
LRU_C = 8.0
RET_HEADS = 8
RET_DK = 64
RET_DV = 64
RET_CHUNK = 128
MLA_HEADS = 8
MLA_Q_RANK = 384
MLA_KV_RANK = 256
MLA_NOPE = 64
MLA_ROPE = 32
MLA_V = 64
D_FF = 2816
FFN_CONV = 3

LANES = 128
SUBLANES = 8
VMEM_LIMIT = 56 * 1024 * 1024

COL_LRU = 0
COL_RET = 1024
COL_GATE = 3072
COL_MLA = 6144
N_IN_PACKED = 6912
MLA_PACK = 768
KR_LANE = 64
LOG2_E = 1.4426950408889634
V_ROWS = MLA_V + 16


def _params(*sem):
    return pltpu.CompilerParams(dimension_semantics=sem, vmem_limit_bytes=VMEM_LIMIT)


def _gelu_tanh(x):
    return 0.5 * x * (1.0 + jnp.tanh(0.7978845608028654 * (x + 0.044715 * (x * x * x))))


def _sigmoid(x):
    return 1.0 / (1.0 + jnp.exp(-x))


def _rms(x, g):
    return x * lax.rsqrt(jnp.mean(x * x, axis=-1, keepdims=True) + EPS) * g


def _ada_kernel(c_ref, w_ref, b_ref, o_ref):
    c = c_ref[...]
    ca = c * _sigmoid(c)
    o_ref[...] = jnp.dot(ca, w_ref[...], preferred_element_type=F32,
                         precision=lax.Precision.HIGHEST) + b_ref[...]


def _ada(c_pad, ada_w, ada_b):
    depth, d, n = ada_w.shape
    rows = c_pad.shape[0]
    tn = 1536
    return pl.pallas_call(
        _ada_kernel,
        grid=(depth, n // tn),
        in_specs=[
            pl.BlockSpec((rows, d), lambda l, j: (0, 0)),
            pl.BlockSpec((None, d, tn), lambda l, j: (l, 0, j)),
            pl.BlockSpec((None, 1, tn), lambda l, j: (l, 0, j)),
        ],
        out_specs=pl.BlockSpec((None, rows, tn), lambda l, j: (l, 0, j)),
        out_shape=jax.ShapeDtypeStruct((depth, rows, n), F32),
        compiler_params=_params("parallel", "parallel"),
        name="ada_mod",
    )(c_pad, ada_w, ada_b.reshape(depth, 1, n))


def _rope_kernel(pos_ref, invr_ref, sgnr_ref, invm_ref, sgnm_ref, cr_ref, sr_ref, cm_ref, sm_ref):
    pos = pos_ref[...].astype(F32)
    ang_r = pos * invr_ref[...]
    cr_ref[...] = jnp.cos(ang_r)
    sr_ref[...] = jnp.sin(ang_r) * sgnr_ref[...]
    ang_m = pos * invm_ref[...]
    cm_ref[...] = jnp.cos(ang_m)
    sm_ref[...] = jnp.sin(ang_m) * sgnm_ref[...]


def _rope_tables(positions):
    bsz, s = positions.shape
    ts = 1024
    inv_r = ROPE_THETA ** (-jnp.arange(0, RET_DK, 2, dtype=F32) / RET_DK)
    inv_m = ROPE_THETA ** (-jnp.arange(0, MLA_ROPE, 2, dtype=F32) / MLA_ROPE)
    half_r = RET_DK // 2
    half_m = MLA_ROPE // 2
    invr = jnp.tile(inv_r, LANES // half_r).reshape(1, LANES)
    sgnr = jnp.tile(jnp.concatenate([-jnp.ones(half_r, F32), jnp.ones(half_r, F32)]), LANES // RET_DK)
    zeros = jnp.zeros(MLA_NOPE, F32)
    tail = jnp.zeros(LANES - MLA_NOPE - MLA_ROPE, F32)
    invm = jnp.concatenate([zeros, inv_m, inv_m, tail]).reshape(1, LANES)
    sgnm = jnp.concatenate([zeros, -jnp.ones(half_m, F32), jnp.ones(half_m, F32), tail])
    row = pl.BlockSpec((1, LANES), lambda b, i: (0, 0))
    tab = pl.BlockSpec((None, ts, LANES), lambda b, i: (b, i, 0))
    shp = jax.ShapeDtypeStruct((bsz, s, LANES), F32)
    return pl.pallas_call(
        _rope_kernel,
        grid=(bsz, s // ts),
        in_specs=[pl.BlockSpec((None, ts, 1), lambda b, i: (b, i, 0)), row, row, row, row],
        out_specs=[tab, tab, tab, tab],
        out_shape=[shp, shp, shp, shp],
        compiler_params=_params("parallel", "parallel"),
        name="rope_tables",
    )(positions.reshape(bsz, s, 1), invr, sgnr.reshape(1, LANES), invm, sgnm.reshape(1, LANES))


def _prenorm_matmul_kernel(x_ref, g_ref, sh_ref, sc_ref, w_ref, o_ref, h_ref):
    @pl.when(pl.program_id(2) == 0)
    def _():
        h = _rms(x_ref[...], g_ref[...]) * (1.0 + sc_ref[...]) + sh_ref[...]
        h_ref[...] = h.astype(BF16)

    o_ref[...] = jnp.dot(h_ref[...], w_ref[...], preferred_element_type=F32).astype(o_ref.dtype)


def _prenorm_matmul(x, g, shift, scale, w, tn, name):
    bsz, s, d = x.shape
    n = w.shape[1]
    tm = 1024
    vec = pl.BlockSpec((None, 1, d), lambda b, i, j: (b, 0, 0))
    return pl.pallas_call(
        _prenorm_matmul_kernel,
        grid=(bsz, s // tm, n // tn),
        in_specs=[
            pl.BlockSpec((None, tm, d), lambda b, i, j: (b, i, 0)),
            pl.BlockSpec((1, d), lambda b, i, j: (0, 0)),
            vec, vec,
            pl.BlockSpec((d, tn), lambda b, i, j: (0, j)),
        ],
        out_specs=pl.BlockSpec((None, tm, tn), lambda b, i, j: (b, i, j)),
        out_shape=jax.ShapeDtypeStruct((bsz, s, n), BF16),
        scratch_shapes=[pltpu.VMEM((tm, d), BF16)],
        compiler_params=_params("parallel", "parallel", "arbitrary"),
        name=name,
    )(x, g.reshape(1, d), shift, scale, w)


def _lru_kernel(t_ref, cw_ref, cb_ref, wbd_ref, bb_ref, lam_ref, o_ref, xbuf_ref, hc_ref, *, ts):
    w = LRU_WIDTH

    @pl.when(pl.program_id(1) == 0)
    def _():
        xbuf_ref[0:SUBLANES, :] = jnp.zeros((SUBLANES, w), F32)
        hc_ref[...] = jnp.zeros((1, w), F32)

    xb = t_ref[:, 0:w].astype(F32)
    gb = t_ref[:, w:2 * w].astype(F32)
    xbuf_ref[SUBLANES:SUBLANES + ts, :] = xb
    xc = xb * cw_ref[LRU_CONV - 1:LRU_CONV, :] + cb_ref[...]
    for k in range(LRU_CONV - 1):
        back = LRU_CONV - 1 - k
        xc = xc + xbuf_ref[pl.ds(SUBLANES - back, ts), :] * cw_ref[k:k + 1, :]
    xbuf_ref[0:SUBLANES, :] = xb[ts - SUBLANES:, :]

    z = jnp.dot(xc.astype(BF16), wbd_ref[...], preferred_element_type=F32) + bb_ref[...]
    r = _sigmoid(z[:, :w])
    ig = _sigmoid(z[:, w:])
    nl = -lam_ref[...]
    softplus = jnp.maximum(nl, 0.0) + jnp.log1p(jnp.exp(-jnp.abs(nl)))
    log_a = (-LRU_C) * r * softplus
    a = jnp.exp(log_a)
    th = jnp.tanh(log_a)
    u = jnp.sqrt(-2.0 * th / (1.0 - th)) * (ig * xc)

    row = lax.broadcasted_iota(jnp.int32, (ts, w), 0)
    k = 1
    while k < ts:
        keep = row >= k
        a_prev = jnp.where(keep, pltpu.roll(a, k, 0), 1.0)
        u_prev = jnp.where(keep, pltpu.roll(u, k, 0), 0.0)
        u = a * u_prev + u
        a = a * a_prev
        k *= 2
    h = a * hc_ref[...] + u
    hc_ref[...] = h[ts - 1:ts, :]
    o_ref[...] = (h * _gelu_tanh(gb)).astype(o_ref.dtype)


def _lru(t, conv_w, conv_b, wbd, bb, lam):
    bsz, s, _ = t.shape
    ts = 256
    w = LRU_WIDTH
    const = lambda shape: pl.BlockSpec(shape, lambda b, i: (0,) * len(shape))
    return pl.pallas_call(
        functools.partial(_lru_kernel, ts=ts),
        grid=(bsz, s // ts),
        in_specs=[
            pl.BlockSpec((None, ts, 2 * w), lambda b, i: (b, i, COL_LRU // (2 * w))),
            const((LRU_CONV, w)), const((1, w)), const((w, 2 * w)), const((1, 2 * w)), const((1, w)),
        ],
        out_specs=pl.BlockSpec((None, ts, w), lambda b, i: (b, i, 0)),
        out_shape=jax.ShapeDtypeStruct((bsz, s, w), BF16),
        scratch_shapes=[pltpu.VMEM((ts + SUBLANES, w), F32), pltpu.VMEM((1, w), F32)],
        compiler_params=_params("parallel", "arbitrary"),
        name="lru_mixer",
    )(t, conv_w, conv_b.reshape(1, w), wbd, bb, lam.reshape(1, w))


def _ret_kernel(q_ref, k_ref, v_ref, g_ref, cos_ref, sin_ref, dec_ref, qdec_ref, kvdec_ref, cdec_ref,
                bd_ref, o_ref, st_ref, *, n_chunks):
    c_len = RET_CHUNK
    pairs = RET_HEADS // 2

    @pl.when(pl.program_id(1) == 0)
    def _():
        st_ref[...] = jnp.zeros(st_ref.shape, F32)

    lane = lax.broadcasted_iota(jnp.int32, (c_len, LANES), 1)
    head0 = lane < RET_DK
    first_half = (lane % RET_DK) < (RET_DK // 2)
    inv_n = 1.0 / RET_DV

    def rope(x, cos, sin):
        swapped = jnp.where(first_half, pltpu.roll(x, LANES - RET_DK // 2, 1), pltpu.roll(x, RET_DK // 2, 1))
        return x * cos + swapped * sin

    def head_mean(x):
        m0 = jnp.sum(jnp.where(head0, x, 0.0), axis=-1, keepdims=True) * inv_n
        m1 = jnp.sum(jnp.where(head0, 0.0, x), axis=-1, keepdims=True) * inv_n
        return jnp.where(head0, m0, m1)

    nt = (((1,), (1,)), ((), ()))
    tn = (((0,), (0,)), ((), ()))
    for c in range(n_chunks):
        rows = slice(c * c_len, (c + 1) * c_len)
        cos = cos_ref[rows, :]
        sin = sin_ref[rows, :]
        for p in range(pairs):
            cols = slice(p * LANES, (p + 1) * LANES)
            q = rope(q_ref[rows, cols].astype(F32), cos, sin)
            k = rope(k_ref[rows, cols].astype(F32), cos, sin) * (RET_DK ** -0.5)
            v = v_ref[rows, cols]
            qb = q.astype(BF16)
            kb = k.astype(BF16)
            zero = jnp.zeros_like(qb)
            s0 = lax.dot_general(jnp.where(head0, qb, zero), kb, nt, preferred_element_type=F32)
            s1 = lax.dot_general(jnp.where(head0, zero, qb), kb, nt, preferred_element_type=F32)
            probs = jnp.concatenate([s0 * dec_ref[2 * p], s1 * dec_ref[2 * p + 1]], axis=1).astype(BF16)
            v2 = jnp.concatenate([jnp.where(head0, v, zero), jnp.where(head0, zero, v)], axis=0)
            y = jnp.dot(probs, v2, preferred_element_type=F32)
            state = st_ref[p]
            y = y + jnp.dot(qb, state.astype(BF16), preferred_element_type=F32) * qdec_ref[p]
            vd = (v.astype(F32) * kvdec_ref[p]).astype(BF16)
            kv = lax.dot_general(kb, vd, tn, preferred_element_type=F32)
            st_ref[p] = state * cdec_ref[p] + kv * bd_ref[...]

            d = y - head_mean(y)
            yn = d * lax.rsqrt(head_mean(d * d) + EPS)
            g = g_ref[rows, cols].astype(F32)
            o_ref[rows, cols] = (g * _sigmoid(g) * yn).astype(o_ref.dtype)


def _ret_constants():
    f32 = F32
    log_gamma = jnp.log1p(-(2.0 ** (-5.0 - jnp.arange(RET_HEADS, dtype=f32))))
    idx = jnp.arange(RET_CHUNK, dtype=f32)
    diff = idx[:, None] - idx[None, :]
    causal = diff >= 0
    inner = jnp.where(causal[None], jnp.exp(jnp.where(causal, diff, 0.0)[None] * log_gamma[:, None, None]), 0.0)
    kv_decay = jnp.exp((RET_CHUNK - 1.0 - idx)[None, :] * log_gamma[:, None])
    q_decay = jnp.exp((idx + 1.0)[:, None] * log_gamma[None, :])
    chunk_decay = jnp.exp(RET_CHUNK * log_gamma)
    pairs = RET_HEADS // 2

    def by_lane(per_head):
        rows = per_head.shape[0]
        return jnp.repeat(per_head.reshape(rows, pairs, 2), RET_DK, axis=2).reshape(rows, pairs, LANES).transpose(1, 0, 2)

    qdec = by_lane(q_decay)
    kvdec = by_lane(kv_decay.T)
    cdec = by_lane(chunk_decay[None, :])
    lane_head = jnp.arange(LANES) // RET_DK
    bd = (lane_head[:, None] == lane_head[None, :]).astype(f32)
    return inner, qdec, kvdec, cdec, bd


def _ret(t, cos_r, sin_r):
    bsz, s, _ = t.shape
    tc = 512
    w = RET_HEADS * RET_DK
    inner, qdec, kvdec, cdec, bd = _ret_constants()
    base = COL_RET // w
    col = lambda j: pl.BlockSpec((None, tc, w), lambda b, i: (b, i, base + j))
    tab = pl.BlockSpec((None, tc, LANES), lambda b, i: (b, i, 0))
    const = lambda shape: pl.BlockSpec(shape, lambda b, i: (0,) * len(shape))
    return pl.pallas_call(
        functools.partial(_ret_kernel, n_chunks=tc // RET_CHUNK),
        grid=(bsz, s // tc),
        in_specs=[col(0), col(1), col(2), col(3), tab, tab,
                  const(inner.shape), const(qdec.shape), const(kvdec.shape), const(cdec.shape), const(bd.shape)],
        out_specs=pl.BlockSpec((None, tc, w), lambda b, i: (b, i, 0)),
        out_shape=jax.ShapeDtypeStruct((bsz, s, w), BF16),
        scratch_shapes=[pltpu.VMEM((RET_HEADS // 2, LANES, LANES), F32)],
        compiler_params=_params("parallel", "arbitrary"),
        name="ret_mixer",
    )(t, t, t, t, cos_r, sin_r, inner, qdec, kvdec, cdec, bd)


def _mla_proj_kernel(t_ref, gq_ref, gkv_ref, wq_ref, wkv_ref, cos_ref, sin_ref, qt_ref, k_ref, vt_ref):
    ts = t_ref.shape[0]
    cos = cos_ref[...]
    sin = sin_ref[...]
    lane = lax.broadcasted_iota(jnp.int32, (ts, LANES), 1)
    low_half = lane < (MLA_NOPE + MLA_ROPE // 2)
    half = MLA_ROPE // 2

    def rope(x):
        swapped = jnp.where(low_half, pltpu.roll(x, LANES - half, 1), pltpu.roll(x, half, 1))
        return x * cos + swapped * sin

    ckv = t_ref[:, 0:MLA_KV_RANK].astype(F32)
    kr = t_ref[:, MLA_KV_RANK:MLA_KV_RANK + LANES].astype(F32)
    cq = t_ref[:, MLA_PACK - MLA_Q_RANK:MLA_PACK].astype(F32)

    q = jnp.dot(_rms(cq, gq_ref[...]).astype(BF16), wq_ref[...], preferred_element_type=F32)
    scale = (MLA_NOPE + MLA_ROPE) ** -0.5 * LOG2_E
    for h in range(MLA_HEADS):
        cols = slice(h * LANES, (h + 1) * LANES)
        qt_ref[cols, :] = (rope(q[:, cols]) * scale).T.astype(qt_ref.dtype)

    kvu = jnp.dot(_rms(ckv, gkv_ref[...]).astype(BF16), wkv_ref[...], preferred_element_type=F32)
    k_rope = rope(kr)
    ones_lane = (lane == MLA_V).astype(F32)
    for h in range(MLA_HEADS):
        cols = slice(h * LANES, (h + 1) * LANES)
        k_ref[:, cols] = (kvu[:, cols] + k_rope).astype(k_ref.dtype)
        vcols = slice((MLA_HEADS + h) * LANES, (MLA_HEADS + h + 1) * LANES)
        vt_ref[cols, :] = (kvu[:, vcols] + ones_lane).T.astype(vt_ref.dtype)


def _mla_proj(t, gq, gkv, wq, wkv, cos_m, sin_m):
    bsz, s, _ = t.shape
    ts = 512
    hq = MLA_HEADS * LANES
    const = lambda shape: pl.BlockSpec(shape, lambda b, i: (0,) * len(shape))
    tab = pl.BlockSpec((None, ts, LANES), lambda b, i: (b, i, 0))
    rowmajor = pl.BlockSpec((None, ts, hq), lambda b, i: (b, i, 0))
    transposed = pl.BlockSpec((None, hq, ts), lambda b, i: (b, 0, i))
    return pl.pallas_call(
        _mla_proj_kernel,
        grid=(bsz, s // ts),
        in_specs=[pl.BlockSpec((None, ts, MLA_PACK), lambda b, i: (b, i, COL_MLA // MLA_PACK)),
                  const((1, MLA_Q_RANK)), const((1, MLA_KV_RANK)), const(wq.shape), const(wkv.shape), tab, tab],
        out_specs=[transposed, rowmajor, transposed],
        out_shape=[jax.ShapeDtypeStruct((bsz, hq, s), BF16), jax.ShapeDtypeStruct((bsz, s, hq), BF16),
                   jax.ShapeDtypeStruct((bsz, hq, s), BF16)],
        compiler_params=_params("parallel", "parallel"),
        name="mla_proj",
    )(t, gq.reshape(1, -1), gkv.reshape(1, -1), wq, wkv, cos_m, sin_m)


def _flash_kernel(qt_ref, k_ref, vt_ref, o_ref, acc_ref, sa_ref, sb_ref, *, tq):
    i = pl.program_id(2)
    acc_ref[...] = jnp.zeros(acc_ref.shape, F32)

    def qk(j, dst_ref, masked):
        off = pl.multiple_of(j * tq, tq)
        for h in range(2):
            rows = slice(h * LANES, (h + 1) * LANES)
            s = jnp.dot(k_ref[pl.ds(off, tq), rows], qt_ref[rows, :], preferred_element_type=F32)
            if masked:
                key = lax.broadcasted_iota(jnp.int32, (tq, tq), 0)
                qry = lax.broadcasted_iota(jnp.int32, (tq, tq), 1)
                s = jnp.where(key <= qry, s, -1e30)
            dst_ref[h] = s

    def softmax_pv(j, src_ref, ms):
        off = pl.multiple_of(j * tq, tq)
        new_m = []
        for h in range(2):
            s = src_ref[h]
            m_new = jnp.maximum(ms[h], jnp.max(s, axis=0, keepdims=True))
            alpha = jnp.exp2(ms[h] - m_new)
            p = jnp.exp2(s - m_new).astype(BF16)
            vt = vt_ref[h * LANES:h * LANES + V_ROWS, pl.ds(off, tq)]
            acc_ref[h] = alpha * acc_ref[h] + jnp.dot(vt, p, preferred_element_type=F32)
            new_m.append(m_new)
        return tuple(new_m)

    n_pairs = i // 2
    qk(i, sa_ref, True)

    def pair(jj, ms):
        qk(2 * jj, sb_ref, False)
        ms = softmax_pv(jnp.where(jj == 0, i, 2 * jj - 1), sa_ref, ms)
        qk(2 * jj + 1, sa_ref, False)
        return softmax_pv(2 * jj, sb_ref, ms)

    m0 = jnp.full((1, tq), -1e30, F32)
    ms = lax.fori_loop(0, n_pairs, pair, (m0, m0))
    in_a = jnp.where(n_pairs == 0, i, 2 * n_pairs - 1)

    @pl.when(i % 2 == 1)
    def _():
        qk(i - 1, sb_ref, False)
        softmax_pv(i - 1, sb_ref, softmax_pv(in_a, sa_ref, ms))

    @pl.when(i % 2 == 0)
    def _():
        softmax_pv(in_a, sa_ref, ms)

    outs = []
    for h in range(2):
        acc = acc_ref[h]
        outs.append(acc[0:MLA_V, :] / acc[MLA_V:MLA_V + 1, :])
    o_ref[...] = jnp.concatenate(outs, axis=0).T.astype(o_ref.dtype)


def _flash(qt, k, vt):
    bsz, s, _ = k.shape
    tq = 512
    pairs = MLA_HEADS // 2
    return pl.pallas_call(
        functools.partial(_flash_kernel, tq=tq),
        grid=(bsz, pairs, s // tq),
        in_specs=[
            pl.BlockSpec((None, 2 * LANES, tq), lambda b, p, i: (b, p, i)),
            pl.BlockSpec((None, s, 2 * LANES), lambda b, p, i: (b, 0, p)),
            pl.BlockSpec((None, 2 * LANES, s), lambda b, p, i: (b, p, 0)),
        ],
        out_specs=pl.BlockSpec((None, tq, LANES), lambda b, p, i: (b, i, p)),
        out_shape=jax.ShapeDtypeStruct((bsz, s, MLA_HEADS * MLA_V), BF16),
        scratch_shapes=[pltpu.VMEM((2, V_ROWS, tq), F32), pltpu.VMEM((2, tq, tq), F32),
                        pltpu.VMEM((2, tq, tq), F32)],
        compiler_params=_params("parallel", "parallel", "arbitrary"),
        name="mla_flash",
    )(qt, k, vt)


def _merge_kernel(yl_ref, yr_ref, ym_ref, gt_ref, x_ref, wl_ref, wr_ref, wm_ref, wo_ref, gp_ref, gm_ref, o_ref):
    d = D_MODEL
    merged = _sigmoid(gt_ref[:, 0:d].astype(F32)) * jnp.dot(yl_ref[...], wl_ref[...], preferred_element_type=F32)
    merged = merged + _sigmoid(gt_ref[:, d:2 * d].astype(F32)) * jnp.dot(
        yr_ref[...], wr_ref[...], preferred_element_type=F32)
    merged = merged + _sigmoid(gt_ref[:, 2 * d:3 * d].astype(F32)) * jnp.dot(
        ym_ref[...], wm_ref[...], preferred_element_type=F32)
    y = jnp.dot(merged.astype(BF16), wo_ref[...], preferred_element_type=F32)
    o_ref[...] = x_ref[...] + gm_ref[...] * _rms(y, gp_ref[...])


def _merge(y_lru, y_ret, y_mla, t, x, wl, wr, wm, wo, g_post, g_mod):
    bsz, s, d = x.shape
    tm = 512
    w = y_lru.shape[-1]
    br = pl.BlockSpec((None, tm, w), lambda b, i: (b, i, 0))
    const = lambda shape: pl.BlockSpec(shape, lambda b, i: (0,) * len(shape))
    return pl.pallas_call(
        _merge_kernel,
        grid=(bsz, s // tm),
        in_specs=[br, br, br,
                  pl.BlockSpec((None, tm, 3 * d), lambda b, i: (b, i, COL_GATE // (3 * d))),
                  pl.BlockSpec((None, tm, d), lambda b, i: (b, i, 0)),
                  const(wl.shape), const(wr.shape), const(wm.shape), const(wo.shape), const((1, d)),
                  pl.BlockSpec((None, 1, d), lambda b, i: (b, 0, 0))],
        out_specs=pl.BlockSpec((None, tm, d), lambda b, i: (b, i, 0)),
        out_shape=jax.ShapeDtypeStruct((bsz, s, d), F32),
        compiler_params=_params("parallel", "parallel"),
        name="mixer_merge",
    )(y_lru, y_ret, y_mla, t, x, wl, wr, wm, wo, g_post.reshape(1, d), g_mod)


def _ffn_down_kernel(up_ref, halo_ref, cw_ref, cb_ref, wd_ref, x_ref, gp_ref, gf_ref, o_ref, act_ref, *, tm, cw):
    first = pl.program_id(1) == 0
    row = lax.broadcasted_iota(jnp.int32, (tm, cw), 0)
    row0 = row == 0
    row1 = row == 1

    def conv(c0):
        cols = slice(c0, c0 + cw)
        xv = up_ref[:, cols].astype(F32)
        hl = jnp.where(first, 0.0, halo_ref[:, cols].astype(F32))
        h1 = hl[SUBLANES - 1:SUBLANES, :]
        h2 = hl[SUBLANES - 2:SUBLANES - 1, :]
        xm1 = jnp.where(row0, h1, pltpu.roll(xv, 1, 0))
        xm2 = jnp.where(row0, h2, jnp.where(row1, h1, pltpu.roll(xv, 2, 0)))
        y = xv * cw_ref[2:3, cols] + cb_ref[:, cols]
        y = y + xm2 * cw_ref[0:1, cols]
        return y + xm1 * cw_ref[1:2, cols]

    for c in range(D_FF // cw):
        u = conv(c * cw)
        g = conv(D_FF + c * cw)
        act_ref[:, c * cw:(c + 1) * cw] = (_gelu_tanh(g) * u).astype(BF16)
    y = jnp.dot(act_ref[...], wd_ref[...], preferred_element_type=F32)
    o_ref[...] = x_ref[...] + gf_ref[...] * _rms(y, gp_ref[...])


def _ffn_down(up, conv_w, conv_b, wd, x, g_post, g_mod):
    bsz, s, d = x.shape
    tm = 512
    n = up.shape[-1]
    per = tm // SUBLANES
    const = lambda shape: pl.BlockSpec(shape, lambda b, i: (0,) * len(shape))
    return pl.pallas_call(
        functools.partial(_ffn_down_kernel, tm=tm, cw=256),
        grid=(bsz, s // tm),
        in_specs=[pl.BlockSpec((None, tm, n), lambda b, i: (b, i, 0)),
                  pl.BlockSpec((None, SUBLANES, n), lambda b, i: (b, jnp.maximum(i * per - 1, 0), 0)),
                  const((FFN_CONV, n)), const((1, n)), const(wd.shape),
                  pl.BlockSpec((None, tm, d), lambda b, i: (b, i, 0)),
                  const((1, d)),
                  pl.BlockSpec((None, 1, d), lambda b, i: (b, 0, 0))],
        out_specs=pl.BlockSpec((None, tm, d), lambda b, i: (b, i, 0)),
        out_shape=jax.ShapeDtypeStruct((bsz, s, d), F32),
        scratch_shapes=[pltpu.VMEM((tm, D_FF), BF16)],
        compiler_params=_params("parallel", "arbitrary"),
        name="ffn_down",
    )(up, up, conv_w, conv_b.reshape(1, n), wd, x, g_post.reshape(1, d), g_mod)


def _pack_w_in(w):
    d = w.shape[0]
    o_cq = 2 * LRU_WIDTH + 4 * RET_HEADS * RET_DK
    o_ckv = o_cq + MLA_Q_RANK
    o_kr = o_ckv + MLA_KV_RANK
    o_gate = o_kr + MLA_ROPE
    z = lambda n: jnp.zeros((d, n), w.dtype)
    return jnp.concatenate([
        w[:, :o_cq], w[:, o_gate:], w[:, o_ckv:o_kr], z(KR_LANE), w[:, o_kr:o_gate],
        z(LANES - KR_LANE - MLA_ROPE), w[:, o_cq:o_ckv]], axis=1).astype(BF16)


def _pack_w_uq(w):
    r = w.shape[0]
    dqk = MLA_NOPE + MLA_ROPE
    w3 = w.reshape(r, MLA_HEADS, dqk)
    return jnp.pad(w3, ((0, 0), (0, 0), (0, LANES - dqk))).reshape(r, MLA_HEADS * LANES).astype(BF16)


def _pack_w_ukv(w):
    r = w.shape[0]
    w3 = w.reshape(r, MLA_HEADS, MLA_NOPE + MLA_V)
    k = jnp.pad(w3[:, :, :MLA_NOPE], ((0, 0), (0, 0), (0, LANES - MLA_NOPE))).reshape(r, MLA_HEADS * LANES)
    v = jnp.pad(w3[:, :, MLA_NOPE:], ((0, 0), (0, 0), (0, LANES - MLA_V))).reshape(r, MLA_HEADS * LANES)
    return jnp.concatenate([k, v], axis=1).astype(BF16)


def _block_diag(w):
    nb, n, _ = w.shape
    eye = jnp.eye(nb, dtype=w.dtype)
    return (eye[:, None, :, None] * w[:, :, None, :]).reshape(nb * n, nb * n)


def kernel(x, c, positions, ada_w, ada_b, mix_pre_g, mix_post_g, w_in, lru_conv_w, lru_conv_b, lru_wa, lru_ba, lru_wx, lru_bx, lru_lambda, lru_wo, ret_wo, mla_q_norm_g, mla_w_uq, mla_kv_norm_g, mla_w_ukv, mla_wo, w_out, ffn_pre_g, ffn_post_g, ffn_w_up, ffn_conv_w, ffn_conv_b, ffn_w_down):
    bsz, s, d = x.shape
    c_pad = jnp.pad(c, ((0, SUBLANES - bsz), (0, 0)))
    mod = _ada(c_pad, ada_w, ada_b)[:, :bsz]
    cos_r, sin_r, cos_m, sin_m = _rope_tables(positions)

    for l in range(DEPTH):
        sh_m, sc_m, g_m, sh_f, sc_f, g_f = [mod[l, :, i * d:(i + 1) * d].reshape(bsz, 1, d) for i in range(6)]
        t = _prenorm_matmul(x, mix_pre_g[l], sh_m, sc_m, _pack_w_in(w_in[l]), 2304, "mixer_in_proj")
        wbd = jnp.concatenate([_block_diag(lru_wa[l]), _block_diag(lru_wx[l])], axis=1).astype(BF16)
        bb = jnp.concatenate([lru_ba[l], lru_bx[l]]).reshape(1, 2 * LRU_WIDTH)
        y_lru = _lru(t, lru_conv_w[l], lru_conv_b[l], wbd, bb, lru_lambda[l])
        y_ret = _ret(t, cos_r, sin_r)
        qt, k, vt = _mla_proj(t, mla_q_norm_g[l], mla_kv_norm_g[l], _pack_w_uq(mla_w_uq[l]),
                              _pack_w_ukv(mla_w_ukv[l]), cos_m, sin_m)
        y_mla = _flash(qt, k, vt)
        x = _merge(y_lru, y_ret, y_mla, t, x, lru_wo[l].astype(BF16), ret_wo[l].astype(BF16),
                   mla_wo[l].astype(BF16), w_out[l].astype(BF16), mix_post_g[l], g_m)
        up = _prenorm_matmul(x, ffn_pre_g[l], sh_f, sc_f, ffn_w_up[l].astype(BF16), 2816, "ffn_up_proj")
        x = _ffn_down(up, ffn_conv_w[l], ffn_conv_b[l], ffn_w_down[l].astype(BF16), x, ffn_post_g[l], g_f)
    return x
```

```python
import functools

import jax
import jax.numpy as jnp
import numpy as np
from jax import lax
from jax.experimental import pallas as pl
from jax.experimental.pallas import tpu as pltpu

F32 = jnp.float32
BF16 = jnp.bfloat16

D_MODEL = 1024
DEPTH = 2
EPS = 1e-6
ROPE_THETA = 10000.0
LRU_WIDTH = 512
LRU_BLOCKS = 8
LRU_BLOCK = LRU_WIDTH // LRU_BLOCKS
LRU_CONV = 4
LRU_C = 8.0
RET_HEADS = 8
RET_DK = 64
RET_DV = 64
RET_CHUNK = 128
MLA_HEADS = 8
MLA_Q_RANK = 384
MLA_KV_RANK = 256
MLA_NOPE = 64
MLA_ROPE = 32
MLA_V = 64
D_FF = 2816
FFN_CONV = 3

LANES = 128
SUBLANES = 8
VMEM_LIMIT = 56 * 1024 * 1024

COL_LRU = 0
COL_RET = 1024
COL_GATE = 3072
COL_MLA = 6144
N_IN_PACKED = 6912
MLA_PACK = 768
KR_LANE = 64
LOG2_E = 1.4426950408889634
V_ROWS = MLA_V + 16


def _params(*sem):
    return pltpu.CompilerParams(dimension_semantics=sem, vmem_limit_bytes=VMEM_LIMIT)


def _gelu_tanh(x):
    return 0.5 * x * (1.0 + jnp.tanh(0.7978845608028654 * (x + 0.044715 * (x * x * x))))


def _sigmoid(x):
    return 1.0 / (1.0 + jnp.exp(-x))


def _rms(x, g):
    return x * lax.rsqrt(jnp.mean(x * x, axis=-1, keepdims=True) + EPS) * g


def _ada_kernel(c_ref, w_ref, b_ref, o_ref):
    c = c_ref[...]
    ca = c * _sigmoid(c)
    o_ref[...] = jnp.dot(ca, w_ref[...], preferred_element_type=F32,
                         precision=lax.Precision.HIGHEST) + b_ref[...]


def _ada(c_pad, ada_w, ada_b):
    depth, d, n = ada_w.shape
    rows = c_pad.shape[0]
    tn = 1536
    return pl.pallas_call(
        _ada_kernel,
        grid=(depth, n // tn),
        in_specs=[
            pl.BlockSpec((rows, d), lambda l, j: (0, 0)),
            pl.BlockSpec((None, d, tn), lambda l, j: (l, 0, j)),
            pl.BlockSpec((None, 1, tn), lambda l, j: (l, 0, j)),
        ],
        out_specs=pl.BlockSpec((None, rows, tn), lambda l, j: (l, 0, j)),
        out_shape=jax.ShapeDtypeStruct((depth, rows, n), F32),
        compiler_params=_params("parallel", "parallel"),
        name="ada_mod",
    )(c_pad, ada_w, ada_b.reshape(depth, 1, n))


def _rope_kernel(pos_ref, invr_ref, sgnr_ref, invm_ref, sgnm_ref, cr_ref, sr_ref, cm_ref, sm_ref):
    pos = pos_ref[...].astype(F32)
    ang_r = pos * invr_ref[...]
    cr_ref[...] = jnp.cos(ang_r)
    sr_ref[...] = jnp.sin(ang_r) * sgnr_ref[...]
    ang_m = pos * invm_ref[...]
    cm_ref[...] = jnp.cos(ang_m)
    sm_ref[...] = jnp.sin(ang_m) * sgnm_ref[...]


def _rope_tables(positions):
    bsz, s = positions.shape
    ts = 1024
    inv_r = ROPE_THETA ** (-jnp.arange(0, RET_DK, 2, dtype=F32) / RET_DK)
    inv_m = ROPE_THETA ** (-jnp.arange(0, MLA_ROPE, 2, dtype=F32) / MLA_ROPE)
    half_r = RET_DK // 2
    half_m = MLA_ROPE // 2
    invr = jnp.tile(inv_r, LANES // half_r).reshape(1, LANES)
    sgnr = jnp.tile(jnp.concatenate([-jnp.ones(half_r, F32), jnp.ones(half_r, F32)]), LANES // RET_DK)
    zeros = jnp.zeros(MLA_NOPE, F32)
    tail = jnp.zeros(LANES - MLA_NOPE - MLA_ROPE, F32)
    invm = jnp.concatenate([zeros, inv_m, inv_m, tail]).reshape(1, LANES)
    sgnm = jnp.concatenate([zeros, -jnp.ones(half_m, F32), jnp.ones(half_m, F32), tail])
    row = pl.BlockSpec((1, LANES), lambda b, i: (0, 0))
    tab = pl.BlockSpec((None, ts, LANES), lambda b, i: (b, i, 0))
    shp = jax.ShapeDtypeStruct((bsz, s, LANES), F32)
    return pl.pallas_call(
        _rope_kernel,
        grid=(bsz, s // ts),
        in_specs=[pl.BlockSpec((None, ts, 1), lambda b, i: (b, i, 0)), row, row, row, row],
        out_specs=[tab, tab, tab, tab],
        out_shape=[shp, shp, shp, shp],
        compiler_params=_params("parallel", "parallel"),
        name="rope_tables",
    )(positions.reshape(bsz, s, 1), invr, sgnr.reshape(1, LANES), invm, sgnm.reshape(1, LANES))


def _prenorm_matmul_kernel(x_ref, g_ref, sh_ref, sc_ref, w_ref, o_ref, h_ref):
    @pl.when(pl.program_id(2) == 0)
    def _():
        h = _rms(x_ref[...], g_ref[...]) * (1.0 + sc_ref[...]) + sh_ref[...]
        h_ref[...] = h.astype(BF16)

    o_ref[...] = jnp.dot(h_ref[...], w_ref[...], preferred_element_type=F32).astype(o_ref.dtype)


def _prenorm_matmul(x, g, shift, scale, w, tn, name):
    bsz, s, d = x.shape
    n = w.shape[1]
    tm = 1024
    vec = pl.BlockSpec((None, 1, d), lambda b, i, j: (b, 0, 0))
    return pl.pallas_call(
        _prenorm_matmul_kernel,
        grid=(bsz, s // tm, n // tn),
        in_specs=[
            pl.BlockSpec((None, tm, d), lambda b, i, j: (b, i, 0)),
            pl.BlockSpec((1, d), lambda b, i, j: (0, 0)),
            vec, vec,
            pl.BlockSpec((d, tn), lambda b, i, j: (0, j)),
        ],
        out_specs=pl.BlockSpec((None, tm, tn), lambda b, i, j: (b, i, j)),
        out_shape=jax.ShapeDtypeStruct((bsz, s, n), BF16),
        scratch_shapes=[pltpu.VMEM((tm, d), BF16)],
        compiler_params=_params("parallel", "parallel", "arbitrary"),
        name=name,
    )(x, g.reshape(1, d), shift, scale, w)


def _lru_kernel(t_ref, cw_ref, cb_ref, wbd_ref, bb_ref, lam_ref, o_ref, xbuf_ref, hc_ref, h_ref, *, ts):
    w = LRU_WIDTH

    @pl.when(pl.program_id(1) == 0)
    def _():
        xbuf_ref[0:SUBLANES, :] = jnp.zeros((SUBLANES, w), F32)
        hc_ref[...] = jnp.zeros((1, w), F32)

    xb = t_ref[:, 0:w].astype(F32)
    gb = t_ref[:, w:2 * w].astype(F32)
    xbuf_ref[SUBLANES:SUBLANES + ts, :] = xb
    xc = xb * cw_ref[LRU_CONV - 1:LRU_CONV, :] + cb_ref[...]
    for k in range(LRU_CONV - 1):
        back = LRU_CONV - 1 - k
        xc = xc + xbuf_ref[pl.ds(SUBLANES - back, ts), :] * cw_ref[k:k + 1, :]
    xbuf_ref[0:SUBLANES, :] = xb[ts - SUBLANES:, :]

    z = jnp.dot(xc.astype(BF16), wbd_ref[...], preferred_element_type=F32) + bb_ref[...]
    r = _sigmoid(z[:, :w])
    ig = _sigmoid(z[:, w:])
    nl = -lam_ref[...]
    softplus = jnp.maximum(nl, 0.0) + jnp.log1p(jnp.exp(-jnp.abs(nl)))
    log_a = (-LRU_C) * r * softplus
    a = jnp.exp(log_a)
    th = jnp.tanh(log_a)
    u = jnp.sqrt(-2.0 * th / (1.0 - th)) * (ig * xc)

    groups = ts // SUBLANES
    a = a.reshape(groups, SUBLANES, w)
    u = u.reshape(groups, SUBLANES, w)
    sub = lax.broadcasted_iota(jnp.int32, (groups, SUBLANES, w), 1)
    k = 1
    while k < SUBLANES:
        keep = sub >= k
        a_prev = jnp.where(keep, pltpu.roll(a, k, 1), 1.0)
        u_prev = jnp.where(keep, pltpu.roll(u, k, 1), 0.0)
        u = a * u_prev + u
        a = a * a_prev
        k *= 2
    h_prev = hc_ref[...]
    for r in range(groups):
        hb = a[r] * h_prev + u[r]
        h_ref[r * SUBLANES:(r + 1) * SUBLANES, :] = hb
        h_prev = hb[SUBLANES - 1:SUBLANES, :]
    hc_ref[...] = h_prev
    o_ref[...] = (h_ref[...] * _gelu_tanh(gb)).astype(o_ref.dtype)


def _lru(t, conv_w, conv_b, wbd, bb, lam):
    bsz, s, _ = t.shape
    ts = 256
    w = LRU_WIDTH
    const = lambda shape: pl.BlockSpec(shape, lambda b, i: (0,) * len(shape))
    return pl.pallas_call(
        functools.partial(_lru_kernel, ts=ts),
        grid=(bsz, s // ts),
        in_specs=[
            pl.BlockSpec((None, ts, 2 * w), lambda b, i: (b, i, COL_LRU // (2 * w))),
            const((LRU_CONV, w)), const((1, w)), const((w, 2 * w)), const((1, 2 * w)), const((1, w)),
        ],
        out_specs=pl.BlockSpec((None, ts, w), lambda b, i: (b, i, 0)),
        out_shape=jax.ShapeDtypeStruct((bsz, s, w), BF16),
        scratch_shapes=[pltpu.VMEM((ts + SUBLANES, w), F32), pltpu.VMEM((1, w), F32), pltpu.VMEM((ts, w), F32)],
        compiler_params=_params("parallel", "arbitrary"),
        name="lru_mixer",
    )(t, conv_w, conv_b.reshape(1, w), wbd, bb, lam.reshape(1, w))


def _ret_kernel(q_ref, k_ref, v_ref, g_ref, cos_ref, sin_ref, dec_ref, qdec_ref, kvdec_ref, cdec_ref,
                bd_ref, o_ref, st_ref, *, n_chunks):
    c_len = RET_CHUNK
    pairs = RET_HEADS // 2

    @pl.when(pl.program_id(1) == 0)
    def _():
        st_ref[...] = jnp.zeros(st_ref.shape, F32)

    lane = lax.broadcasted_iota(jnp.int32, (c_len, LANES), 1)
    head0 = lane < RET_DK
    first_half = (lane % RET_DK) < (RET_DK // 2)
    inv_n = 1.0 / RET_DV

    def rope(x, cos, sin):
        swapped = jnp.where(first_half, pltpu.roll(x, LANES - RET_DK // 2, 1), pltpu.roll(x, RET_DK // 2, 1))
        return x * cos + swapped * sin

    def head_mean(x):
        m0 = jnp.sum(jnp.where(head0, x, 0.0), axis=-1, keepdims=True) * inv_n
        m1 = jnp.sum(jnp.where(head0, 0.0, x), axis=-1, keepdims=True) * inv_n
        return jnp.where(head0, m0, m1)

    nt = (((1,), (1,)), ((), ()))
    tn = (((0,), (0,)), ((), ()))
    for c in range(n_chunks):
        rows = slice(c * c_len, (c + 1) * c_len)
        cos = cos_ref[rows, :]
        sin = sin_ref[rows, :]
        for p in range(pairs):
            cols = slice(p * LANES, (p + 1) * LANES)
            q = rope(q_ref[rows, cols].astype(F32), cos, sin)
            k = rope(k_ref[rows, cols].astype(F32), cos, sin) * (RET_DK ** -0.5)
            v = v_ref[rows, cols]
            qb = q.astype(BF16)
            kb = k.astype(BF16)
            zero = jnp.zeros_like(qb)
            s0 = lax.dot_general(jnp.where(head0, qb, zero), kb, nt, preferred_element_type=F32)
            s1 = lax.dot_general(jnp.where(head0, zero, qb), kb, nt, preferred_element_type=F32)
            probs = jnp.concatenate([s0 * dec_ref[2 * p], s1 * dec_ref[2 * p + 1]], axis=1).astype(BF16)
            v2 = jnp.concatenate([jnp.where(head0, v, zero), jnp.where(head0, zero, v)], axis=0)
            y = jnp.dot(probs, v2, preferred_element_type=F32)
            state = st_ref[p]
            y = y + jnp.dot(qb, state.astype(BF16), preferred_element_type=F32) * qdec_ref[p]
            vd = (v.astype(F32) * kvdec_ref[p]).astype(BF16)
            kv = lax.dot_general(kb, vd, tn, preferred_element_type=F32)
            st_ref[p] = state * cdec_ref[p] + kv * bd_ref[...]

            d = y - head_mean(y)
            yn = d * lax.rsqrt(head_mean(d * d) + EPS)
            g = g_ref[rows, cols].astype(F32)
            o_ref[rows, cols] = (g * _sigmoid(g) * yn).astype(o_ref.dtype)


def _ret_constants():
    f32 = F32
    log_gamma = jnp.log1p(-(2.0 ** (-5.0 - jnp.arange(RET_HEADS, dtype=f32))))
    idx = jnp.arange(RET_CHUNK, dtype=f32)
    diff = idx[:, None] - idx[None, :]
    causal = diff >= 0
    inner = jnp.where(causal[None], jnp.exp(jnp.where(causal, diff, 0.0)[None] * log_gamma[:, None, None]), 0.0)
    kv_decay = jnp.exp((RET_CHUNK - 1.0 - idx)[None, :] * log_gamma[:, None])
    q_decay = jnp.exp((idx + 1.0)[:, None] * log_gamma[None, :])
    chunk_decay = jnp.exp(RET_CHUNK * log_gamma)
    pairs = RET_HEADS // 2

    def by_lane(per_head):
        rows = per_head.shape[0]
        return jnp.repeat(per_head.reshape(rows, pairs, 2), RET_DK, axis=2).reshape(rows, pairs, LANES).transpose(1, 0, 2)

    qdec = by_lane(q_decay)
    kvdec = by_lane(kv_decay.T)
    cdec = by_lane(chunk_decay[None, :])
    lane_head = jnp.arange(LANES) // RET_DK
    bd = (lane_head[:, None] == lane_head[None, :]).astype(f32)
    return inner, qdec, kvdec, cdec, bd


def _ret(t, cos_r, sin_r):
    bsz, s, _ = t.shape
    tc = 512
    w = RET_HEADS * RET_DK
    inner, qdec, kvdec, cdec, bd = _ret_constants()
    base = COL_RET // w
    col = lambda j: pl.BlockSpec((None, tc, w), lambda b, i: (b, i, base + j))
    tab = pl.BlockSpec((None, tc, LANES), lambda b, i: (b, i, 0))
    const = lambda shape: pl.BlockSpec(shape, lambda b, i: (0,) * len(shape))
    return pl.pallas_call(
        functools.partial(_ret_kernel, n_chunks=tc // RET_CHUNK),
        grid=(bsz, s // tc),
        in_specs=[col(0), col(1), col(2), col(3), tab, tab,
                  const(inner.shape), const(qdec.shape), const(kvdec.shape), const(cdec.shape), const(bd.shape)],
        out_specs=pl.BlockSpec((None, tc, w), lambda b, i: (b, i, 0)),
        out_shape=jax.ShapeDtypeStruct((bsz, s, w), BF16),
        scratch_shapes=[pltpu.VMEM((RET_HEADS // 2, LANES, LANES), F32)],
        compiler_params=_params("parallel", "arbitrary"),
        name="ret_mixer",
    )(t, t, t, t, cos_r, sin_r, inner, qdec, kvdec, cdec, bd)


def _mla_proj_kernel(t_ref, gq_ref, gkv_ref, wq_ref, wkv_ref, cos_ref, sin_ref, qt_ref, k_ref, vt_ref):
    ts = t_ref.shape[0]
    cos = cos_ref[...]
    sin = sin_ref[...]
    lane = lax.broadcasted_iota(jnp.int32, (ts, LANES), 1)
    low_half = lane < (MLA_NOPE + MLA_ROPE // 2)
    half = MLA_ROPE // 2

    def rope(x):
        swapped = jnp.where(low_half, pltpu.roll(x, LANES - half, 1), pltpu.roll(x, half, 1))
        return x * cos + swapped * sin

    ckv = t_ref[:, 0:MLA_KV_RANK].astype(F32)
    kr = t_ref[:, MLA_KV_RANK:MLA_KV_RANK + LANES].astype(F32)
    cq = t_ref[:, MLA_PACK - MLA_Q_RANK:MLA_PACK].astype(F32)

    q = jnp.dot(_rms(cq, gq_ref[...]).astype(BF16), wq_ref[...], preferred_element_type=F32)
    scale = (MLA_NOPE + MLA_ROPE) ** -0.5 * LOG2_E
    for h in range(MLA_HEADS):
        cols = slice(h * LANES, (h + 1) * LANES)
        qt_ref[cols, :] = (rope(q[:, cols]) * scale).T.astype(qt_ref.dtype)

    kvu = jnp.dot(_rms(ckv, gkv_ref[...]).astype(BF16), wkv_ref[...], preferred_element_type=F32)
    k_rope = rope(kr)
    ones_lane = (lane == MLA_V).astype(F32)
    for h in range(MLA_HEADS):
        cols = slice(h * LANES, (h + 1) * LANES)
        k_ref[:, cols] = (kvu[:, cols] + k_rope).astype(k_ref.dtype)
        vcols = slice((MLA_HEADS + h) * LANES, (MLA_HEADS + h + 1) * LANES)
        vt_ref[cols, :] = (kvu[:, vcols] + ones_lane).T.astype(vt_ref.dtype)


def _mla_proj(t, gq, gkv, wq, wkv, cos_m, sin_m):
    bsz, s, _ = t.shape
    ts = 512
    hq = MLA_HEADS * LANES
    const = lambda shape: pl.BlockSpec(shape, lambda b, i: (0,) * len(shape))
    tab = pl.BlockSpec((None, ts, LANES), lambda b, i: (b, i, 0))
    rowmajor = pl.BlockSpec((None, ts, hq), lambda b, i: (b, i, 0))
    transposed = pl.BlockSpec((None, hq, ts), lambda b, i: (b, 0, i))
    return pl.pallas_call(
        _mla_proj_kernel,
        grid=(bsz, s // ts),
        in_specs=[pl.BlockSpec((None, ts, MLA_PACK), lambda b, i: (b, i, COL_MLA // MLA_PACK)),
                  const((1, MLA_Q_RANK)), const((1, MLA_KV_RANK)), const(wq.shape), const(wkv.shape), tab, tab],
        out_specs=[transposed, rowmajor, transposed],
        out_shape=[jax.ShapeDtypeStruct((bsz, hq, s), BF16), jax.ShapeDtypeStruct((bsz, s, hq), BF16),
                   jax.ShapeDtypeStruct((bsz, hq, s), BF16)],
        compiler_params=_params("parallel", "parallel"),
        name="mla_proj",
    )(t, gq.reshape(1, -1), gkv.reshape(1, -1), wq, wkv, cos_m, sin_m)


def _flash_kernel(qt_ref, k_ref, vt_ref, o_ref, acc_ref, sa_ref, sb_ref, *, tq):
    i = pl.program_id(2)
    acc_ref[...] = jnp.zeros(acc_ref.shape, F32)

    def qk(j, dst_ref, masked):
        off = pl.multiple_of(j * tq, tq)
        for h in range(2):
            rows = slice(h * LANES, (h + 1) * LANES)
            s = jnp.dot(k_ref[pl.ds(off, tq), rows], qt_ref[rows, :], preferred_element_type=F32)
            if masked:
                key = lax.broadcasted_iota(jnp.int32, (tq, tq), 0)
                qry = lax.broadcasted_iota(jnp.int32, (tq, tq), 1)
                s = jnp.where(key <= qry, s, -1e30)
            dst_ref[h] = s

    def softmax_pv(j, src_ref, ms):
        off = pl.multiple_of(j * tq, tq)
        new_m = []
        for h in range(2):
            s = src_ref[h]
            m_new = jnp.maximum(ms[h], jnp.max(s, axis=0, keepdims=True))
            alpha = jnp.exp2(ms[h] - m_new)
            p = jnp.exp2(s - m_new).astype(BF16)
            vt = vt_ref[h * LANES:h * LANES + V_ROWS, pl.ds(off, tq)]
            acc_ref[h] = alpha * acc_ref[h] + jnp.dot(vt, p, preferred_element_type=F32)
            new_m.append(m_new)
        return tuple(new_m)

    n_pairs = i // 2
    qk(i, sa_ref, True)

    def pair(jj, ms):
        qk(2 * jj, sb_ref, False)
        ms = softmax_pv(jnp.where(jj == 0, i, 2 * jj - 1), sa_ref, ms)
        qk(2 * jj + 1, sa_ref, False)
        return softmax_pv(2 * jj, sb_ref, ms)

    m0 = jnp.full((1, tq), -1e30, F32)
    ms = lax.fori_loop(0, n_pairs, pair, (m0, m0))
    in_a = jnp.where(n_pairs == 0, i, 2 * n_pairs - 1)

    @pl.when(i % 2 == 1)
    def _():
        qk(i - 1, sb_ref, False)
        softmax_pv(i - 1, sb_ref, softmax_pv(in_a, sa_ref, ms))

    @pl.when(i % 2 == 0)
    def _():
        softmax_pv(in_a, sa_ref, ms)

    outs = []
    for h in range(2):
        acc = acc_ref[h]
        outs.append(acc[0:MLA_V, :] / acc[MLA_V:MLA_V + 1, :])
    o_ref[...] = jnp.concatenate(outs, axis=0).T.astype(o_ref.dtype)


def _flash(qt, k, vt):
    bsz, s, _ = k.shape
    tq = 512
    pairs = MLA_HEADS // 2
    return pl.pallas_call(
        functools.partial(_flash_kernel, tq=tq),
        grid=(bsz, pairs, s // tq),
        in_specs=[
            pl.BlockSpec((None, 2 * LANES, tq), lambda b, p, i: (b, p, i)),
            pl.BlockSpec((None, s, 2 * LANES), lambda b, p, i: (b, 0, p)),
            pl.BlockSpec((None, 2 * LANES, s), lambda b, p, i: (b, p, 0)),
        ],
        out_specs=pl.BlockSpec((None, tq, LANES), lambda b, p, i: (b, i, p)),
        out_shape=jax.ShapeDtypeStruct((bsz, s, MLA_HEADS * MLA_V), BF16),
        scratch_shapes=[pltpu.VMEM((2, V_ROWS, tq), F32), pltpu.VMEM((2, tq, tq), F32),
                        pltpu.VMEM((2, tq, tq), F32)],
        compiler_params=_params("parallel", "parallel", "arbitrary"),
        name="mla_flash",
    )(qt, k, vt)


def _merge_kernel(yl_ref, yr_ref, ym_ref, gt_ref, x_ref, wl_ref, wr_ref, wm_ref, wo_ref, gp_ref, gm_ref, o_ref):
    d = D_MODEL
    merged = _sigmoid(gt_ref[:, 0:d].astype(F32)) * jnp.dot(yl_ref[...], wl_ref[...], preferred_element_type=F32)
    merged = merged + _sigmoid(gt_ref[:, d:2 * d].astype(F32)) * jnp.dot(
        yr_ref[...], wr_ref[...], preferred_element_type=F32)
    merged = merged + _sigmoid(gt_ref[:, 2 * d:3 * d].astype(F32)) * jnp.dot(
        ym_ref[...], wm_ref[...], preferred_element_type=F32)
    y = jnp.dot(merged.astype(BF16), wo_ref[...], preferred_element_type=F32)
    o_ref[...] = x_ref[...] + gm_ref[...] * _rms(y, gp_ref[...])


def _merge(y_lru, y_ret, y_mla, t, x, wl, wr, wm, wo, g_post, g_mod):
    bsz, s, d = x.shape
    tm = 512
    w = y_lru.shape[-1]
    br = pl.BlockSpec((None, tm, w), lambda b, i: (b, i, 0))
    const = lambda shape: pl.BlockSpec(shape, lambda b, i: (0,) * len(shape))
    return pl.pallas_call(
        _merge_kernel,
        grid=(bsz, s // tm),
        in_specs=[br, br, br,
                  pl.BlockSpec((None, tm, 3 * d), lambda b, i: (b, i, COL_GATE // (3 * d))),
                  pl.BlockSpec((None, tm, d), lambda b, i: (b, i, 0)),
                  const(wl.shape), const(wr.shape), const(wm.shape), const(wo.shape), const((1, d)),
                  pl.BlockSpec((None, 1, d), lambda b, i: (b, 0, 0))],
        out_specs=pl.BlockSpec((None, tm, d), lambda b, i: (b, i, 0)),
        out_shape=jax.ShapeDtypeStruct((bsz, s, d), F32),
        compiler_params=_params("parallel", "parallel"),
        name="mixer_merge",
    )(y_lru, y_ret, y_mla, t, x, wl, wr, wm, wo, g_post.reshape(1, d), g_mod)


def _ffn_kernel(x_ref, gpre_ref, sh_ref, sc_ref, wu_ref, cw_ref, cb_ref, wd_ref, gp_ref, gf_ref, o_ref,
                act_ref, halo_ref, *, tm, cw):
    @pl.when(pl.program_id(1) == 0)
    def _():
        halo_ref[...] = jnp.zeros(halo_ref.shape, F32)

    x = x_ref[...]
    h = (_rms(x, gpre_ref[...]) * (1.0 + sc_ref[...]) + sh_ref[...]).astype(BF16)
    row = lax.broadcasted_iota(jnp.int32, (SUBLANES, cw), 0)

    def conv(c0):
        cols = slice(c0, c0 + cw)
        xv = jnp.dot(h, wu_ref[:, cols], preferred_element_type=F32)
        h1 = halo_ref[SUBLANES - 1:SUBLANES, cols]
        h2 = halo_ref[SUBLANES - 2:SUBLANES - 1, cols]
        halo_ref[:, cols] = xv[tm - SUBLANES:, :]
        r1 = pltpu.roll(xv, 1, 0)
        r2 = pltpu.roll(xv, 2, 0)
        xm1 = jnp.concatenate([jnp.where(row == 0, h1, r1[:SUBLANES]), r1[SUBLANES:]], axis=0)
        top2 = jnp.where(row == 0, h2, jnp.where(row == 1, h1, r2[:SUBLANES]))
        xm2 = jnp.concatenate([top2, r2[SUBLANES:]], axis=0)
        y = xv * cw_ref[2:3, cols] + cb_ref[:, cols]
        y = y + xm2 * cw_ref[0:1, cols]
        return y + xm1 * cw_ref[1:2, cols]

    for c in range(D_FF // cw):
        u = conv(c * cw)
        g = conv(D_FF + c * cw)
        act_ref[:, c * cw:(c + 1) * cw] = (_gelu_tanh(g) * u).astype(BF16)
    y = jnp.dot(act_ref[...], wd_ref[...], preferred_element_type=F32)
    o_ref[...] = x + gf_ref[...] * _rms(y, gp_ref[...])


def _ffn(x, g_pre, shift, scale, wu, conv_w, conv_b, wd, g_post, g_mod):
    bsz, s, d = x.shape
    tm = 512
    n = wu.shape[-1]
    const = lambda shape: pl.BlockSpec(shape, lambda b, i: (0,) * len(shape))
    resident = lambda shape: pl.BlockSpec(shape, lambda b, i: (0,) * len(shape), pipeline_mode=pl.Buffered(1))
    vec = pl.BlockSpec((None, 1, d), lambda b, i: (b, 0, 0))
    return pl.pallas_call(
        functools.partial(_ffn_kernel, tm=tm, cw=256),
        grid=(bsz, s // tm),
        in_specs=[pl.BlockSpec((None, tm, d), lambda b, i: (b, i, 0)),
                  const((1, d)), vec, vec,
                  resident(wu.shape), const((FFN_CONV, n)), const((1, n)), resident(wd.shape),
                  const((1, d)), vec],
        out_specs=pl.BlockSpec((None, tm, d), lambda b, i: (b, i, 0)),
        out_shape=jax.ShapeDtypeStruct((bsz, s, d), F32),
        scratch_shapes=[pltpu.VMEM((tm, D_FF), BF16), pltpu.VMEM((SUBLANES, n), F32)],
        compiler_params=_params("parallel", "arbitrary"),
        name="ffn_fused",
    )(x, g_pre.reshape(1, d), shift, scale, wu, conv_w, conv_b.reshape(1, n), wd, g_post.reshape(1, d), g_mod)


def _pack_w_in(w):
    d = w.shape[0]
    o_cq = 2 * LRU_WIDTH + 4 * RET_HEADS * RET_DK
    o_ckv = o_cq + MLA_Q_RANK
    o_kr = o_ckv + MLA_KV_RANK
    o_gate = o_kr + MLA_ROPE
    z = lambda n: jnp.zeros((d, n), BF16)
    c = lambda a, b: w[:, a:b].astype(BF16)
    return jnp.concatenate([
        c(0, o_cq), c(o_gate, w.shape[1]), c(o_ckv, o_kr), z(KR_LANE), c(o_kr, o_gate),
        z(LANES - KR_LANE - MLA_ROPE), c(o_cq, o_ckv)], axis=1)


def _pack_w_uq(w):
    r = w.shape[0]
    dqk = MLA_NOPE + MLA_ROPE
    w3 = w.reshape(r, MLA_HEADS, dqk)
    return jnp.pad(w3, ((0, 0), (0, 0), (0, LANES - dqk))).reshape(r, MLA_HEADS * LANES).astype(BF16)


def _pack_w_ukv(w):
    r = w.shape[0]
    w3 = w.reshape(r, MLA_HEADS, MLA_NOPE + MLA_V)
    k = jnp.pad(w3[:, :, :MLA_NOPE], ((0, 0), (0, 0), (0, LANES - MLA_NOPE))).reshape(r, MLA_HEADS * LANES)
    v = jnp.pad(w3[:, :, MLA_NOPE:], ((0, 0), (0, 0), (0, LANES - MLA_V))).reshape(r, MLA_HEADS * LANES)
    return jnp.concatenate([k, v], axis=1).astype(BF16)


def _block_diag(w):
    nb, n, _ = w.shape
    eye = jnp.eye(nb, dtype=w.dtype)
    return (eye[:, None, :, None] * w[:, :, None, :]).reshape(nb * n, nb * n)


def kernel(x, c, positions, ada_w, ada_b, mix_pre_g, mix_post_g, w_in, lru_conv_w, lru_conv_b, lru_wa, lru_ba, lru_wx, lru_bx, lru_lambda, lru_wo, ret_wo, mla_q_norm_g, mla_w_uq, mla_kv_norm_g, mla_w_ukv, mla_wo, w_out, ffn_pre_g, ffn_post_g, ffn_w_up, ffn_conv_w, ffn_conv_b, ffn_w_down):
    bsz, s, d = x.shape
    c_pad = jnp.pad(c, ((0, SUBLANES - bsz), (0, 0)))
    mod = _ada(c_pad, ada_w, ada_b)[:, :bsz]
    cos_r, sin_r, cos_m, sin_m = _rope_tables(positions)

    for l in range(DEPTH):
        sh_m, sc_m, g_m, sh_f, sc_f, g_f = [mod[l, :, i * d:(i + 1) * d].reshape(bsz, 1, d) for i in range(6)]
        t = _prenorm_matmul(x, mix_pre_g[l], sh_m, sc_m, _pack_w_in(w_in[l]), 2304, "mixer_in_proj")
        wbd = jnp.concatenate([_block_diag(lru_wa[l]), _block_diag(lru_wx[l])], axis=1).astype(BF16)
        bb = jnp.concatenate([lru_ba[l], lru_bx[l]]).reshape(1, 2 * LRU_WIDTH)
        y_lru = _lru(t, lru_conv_w[l], lru_conv_b[l], wbd, bb, lru_lambda[l])
        y_ret = _ret(t, cos_r, sin_r)
        qt, k, vt = _mla_proj(t, mla_q_norm_g[l], mla_kv_norm_g[l], _pack_w_uq(mla_w_uq[l]),
                              _pack_w_ukv(mla_w_ukv[l]), cos_m, sin_m)
        y_mla = _flash(qt, k, vt)
        x = _merge(y_lru, y_ret, y_mla, t, x, lru_wo[l].astype(BF16), ret_wo[l].astype(BF16),
                   mla_wo[l].astype(BF16), w_out[l].astype(BF16), mix_post_g[l], g_m)
        x = _ffn(x, ffn_pre_g[l], sh_f, sc_f, ffn_w_up[l].astype(BF16), ffn_conv_w[l], ffn_conv_b[l],
                 ffn_w_down[l].astype(BF16), ffn_post_g[l], g_f)
    return x
```

```python
import functools

import jax
import jax.numpy as jnp
import numpy as np
from jax import lax
from jax.experimental import pallas as pl
from jax.experimental.pallas import tpu as pltpu

F32 = jnp.float32
BF16 = jnp.bfloat16

D_MODEL = 1024
DEPTH = 2
EPS = 1e-6
ROPE_THETA = 10000.0
LRU_WIDTH = 512
LRU_BLOCKS = 8
LRU_BLOCK = LRU_WIDTH // LRU_BLOCKS
LRU_CONV = 4
LRU_C = 8.0
RET_HEADS = 8
RET_DK = 64
RET_DV = 64
RET_CHUNK = 128
MLA_HEADS = 8
MLA_Q_RANK = 384
MLA_KV_RANK = 256
MLA_NOPE = 64
MLA_ROPE = 32
MLA_V = 64
D_FF = 2816
FFN_CONV = 3

LANES = 128
SUBLANES = 8
VMEM_LIMIT = 56 * 1024 * 1024

COL_LRU = 0
COL_RET = 1024
COL_GATE = 3072
COL_MLA = 6144
N_IN_PACKED = 6912
MLA_PACK = 768
KR_LANE = 64
LOG2_E = 1.4426950408889634
V_ROWS = MLA_V + 16


def _params(*sem):
    return pltpu.CompilerParams(dimension_semantics=sem, vmem_limit_bytes=VMEM_LIMIT)


def _of_layer(l, *tail, **kw):
    return pl.BlockSpec((None,) + tail, lambda *_: (l,) + (0,) * len(tail), **kw)


def _mod_spec(l, k, d):
    return pl.BlockSpec((None, None, None, 1, d), lambda b, *_: (l, b, k, 0, 0))


GELU_C1 = 0.7978845608028654
GELU_C2 = GELU_C1 * 0.044715


def _gelu_tanh(x):
    return 0.5 * x * (1.0 + jnp.tanh(x * (GELU_C1 + GELU_C2 * (x * x))))


def _sigmoid(x):
    return 1.0 / (1.0 + jnp.exp(-x))


def _rms(x, g):
    return x * lax.rsqrt(jnp.mean(x * x, axis=-1, keepdims=True) + EPS) * g


def _ada_kernel(c_ref, w_ref, b_ref, o_ref):
    c = c_ref[...]
    ca = c * _sigmoid(c)
    o_ref[...] = jnp.dot(ca, w_ref[...], preferred_element_type=F32,
                         precision=lax.Precision.HIGHEST) + b_ref[...]


def _ada(c_pad, ada_w, ada_b):
    depth, d, n = ada_w.shape
    rows = c_pad.shape[0]
    tn = 1536
    return pl.pallas_call(
        _ada_kernel,
        grid=(depth, n // tn),
        in_specs=[
            pl.BlockSpec((rows, d), lambda l, j: (0, 0)),
            pl.BlockSpec((None, d, tn), lambda l, j: (l, 0, j)),
            pl.BlockSpec((None, 1, tn), lambda l, j: (l, 0, j)),
        ],
        out_specs=pl.BlockSpec((None, rows, tn), lambda l, j: (l, 0, j)),
        out_shape=jax.ShapeDtypeStruct((depth, rows, n), F32),
        compiler_params=_params("parallel", "parallel"),
        name="ada_mod",
    )(c_pad, ada_w, ada_b.reshape(depth, 1, n))


def _rope_kernel(pos_ref, inv_ref, cr_ref, sr_ref, cm_ref, sm_ref):
    half_r = RET_DK // 2
    half_m = MLA_ROPE // 2
    ang = pos_ref[...].astype(F32) * inv_ref[...]
    c = jnp.cos(ang)
    s = jnp.sin(ang)
    c_r, s_r = c[:, :half_r], s[:, :half_r]
    c_m, s_m = c[:, half_r:half_r + half_m], s[:, half_r:half_r + half_m]
    reps = LANES // RET_DK
    cr_ref[...] = jnp.concatenate([c_r, c_r] * reps, axis=1)
    sr_ref[...] = jnp.concatenate([-s_r, s_r] * reps, axis=1)
    ts = ang.shape[0]
    tail = LANES - MLA_NOPE - MLA_ROPE
    cm_ref[...] = jnp.concatenate([jnp.ones((ts, MLA_NOPE), F32), c_m, c_m, jnp.ones((ts, tail), F32)], axis=1)
    sm_ref[...] = jnp.concatenate([jnp.zeros((ts, MLA_NOPE), F32), -s_m, s_m, jnp.zeros((ts, tail), F32)], axis=1)


def _rope_tables(positions):
    bsz, s = positions.shape
    ts = 1024
    inv_r = ROPE_THETA ** (-jnp.arange(0, RET_DK, 2, dtype=F32) / RET_DK)
    inv_m = ROPE_THETA ** (-jnp.arange(0, MLA_ROPE, 2, dtype=F32) / MLA_ROPE)
    inv = jnp.concatenate([inv_r, inv_m, jnp.zeros(LANES - inv_r.size - inv_m.size, F32)]).reshape(1, LANES)
    tab = pl.BlockSpec((None, ts, LANES), lambda b, i: (b, i, 0))
    shp = jax.ShapeDtypeStruct((bsz, s, LANES), F32)
    return pl.pallas_call(
        _rope_kernel,
        grid=(bsz, s // ts),
        in_specs=[pl.BlockSpec((None, ts, 1), lambda b, i: (b, i, 0)), pl.BlockSpec((1, LANES), lambda b, i: (0, 0))],
        out_specs=[tab, tab, tab, tab],
        out_shape=[shp, shp, shp, shp],
        compiler_params=_params("parallel", "parallel"),
        name="rope_tables",
    )(positions.reshape(bsz, s, 1), inv)


def _prenorm_matmul_kernel(x_ref, g_ref, sh_ref, sc_ref, w_ref, o_ref, h_ref):
    @pl.when(pl.program_id(2) == 0)
    def _():
        h = _rms(x_ref[...], g_ref[...]) * (1.0 + sc_ref[...]) + sh_ref[...]
        h_ref[...] = h.astype(BF16)

    o_ref[...] = jnp.dot(h_ref[...], w_ref[...], preferred_element_type=F32).astype(o_ref.dtype)


def _prenorm_matmul(l, x, g, mod, k_shift, k_scale, w, tn, name):
    bsz, s, d = x.shape
    n = w.shape[-1]
    tm = 1024
    return pl.pallas_call(
        _prenorm_matmul_kernel,
        grid=(bsz, s // tm, n // tn),
        in_specs=[
            pl.BlockSpec((None, tm, d), lambda b, i, j: (b, i, 0)),
            _of_layer(l, 1, d),
            _mod_spec(l, k_shift, d), _mod_spec(l, k_scale, d),
            pl.BlockSpec((None, d, tn), lambda b, i, j: (l, 0, j)),
        ],
        out_specs=pl.BlockSpec((None, tm, tn), lambda b, i, j: (b, i, j)),
        out_shape=jax.ShapeDtypeStruct((bsz, s, n), BF16),
        scratch_shapes=[pltpu.VMEM((tm, d), BF16)],
        compiler_params=_params("parallel", "parallel", "arbitrary"),
        name=name,
    )(x, g, mod, mod, w)


def _lru_kernel(t_ref, cw_ref, cb_ref, wbd_ref, bb_ref, lam_ref, o_ref, xbuf_ref, hc_ref, h_ref, *, ts):
    w = LRU_WIDTH

    @pl.when(pl.program_id(1) == 0)
    def _():
        xbuf_ref[0:SUBLANES, :] = jnp.zeros((SUBLANES, w), F32)
        hc_ref[...] = jnp.zeros((1, w), F32)

    xb = t_ref[:, 0:w].astype(F32)
    gb = t_ref[:, w:2 * w].astype(F32)
    xbuf_ref[SUBLANES:SUBLANES + ts, :] = xb
    xc = xb * cw_ref[LRU_CONV - 1:LRU_CONV, :] + cb_ref[...]
    for k in range(LRU_CONV - 1):
        back = LRU_CONV - 1 - k
        xc = xc + xbuf_ref[pl.ds(SUBLANES - back, ts), :] * cw_ref[k:k + 1, :]
    xbuf_ref[0:SUBLANES, :] = xb[ts - SUBLANES:, :]

    z = jnp.dot(xc.astype(BF16), wbd_ref[...], preferred_element_type=F32) + bb_ref[...]
    r = _sigmoid(z[:, :w])
    ig = _sigmoid(z[:, w:])
    nl = -lam_ref[...]
    softplus = jnp.maximum(nl, 0.0) + jnp.log1p(jnp.exp(-jnp.abs(nl)))
    log_a = (-LRU_C) * r * softplus
    a = jnp.exp(log_a)
    th = jnp.tanh(log_a)
    u = jnp.sqrt(-2.0 * th / (1.0 - th)) * (ig * xc)

    groups = ts // SUBLANES
    a = a.reshape(groups, SUBLANES, w)
    u = u.reshape(groups, SUBLANES, w)
    sub = lax.broadcasted_iota(jnp.int32, (groups, SUBLANES, w), 1)
    k = 1
    while k < SUBLANES:
        keep = sub >= k
        a_prev = jnp.where(keep, pltpu.roll(a, k, 1), 1.0)
        u_prev = jnp.where(keep, pltpu.roll(u, k, 1), 0.0)
        u = a * u_prev + u
        a = a * a_prev
        k *= 2
    h_prev = hc_ref[...]
    for r in range(groups):
        hb = a[r] * h_prev + u[r]
        h_ref[r * SUBLANES:(r + 1) * SUBLANES, :] = hb
        h_prev = hb[SUBLANES - 1:SUBLANES, :]
    hc_ref[...] = h_prev
    o_ref[...] = (h_ref[...] * _gelu_tanh(gb)).astype(o_ref.dtype)


def _lru(l, t, conv_w, conv_b, wbd, bb, lam):
    bsz, s, _ = t.shape
    ts = 256
    w = LRU_WIDTH
    return pl.pallas_call(
        functools.partial(_lru_kernel, ts=ts),
        grid=(bsz, s // ts),
        in_specs=[
            pl.BlockSpec((None, ts, 2 * w), lambda b, i: (b, i, COL_LRU // (2 * w))),
            _of_layer(l, LRU_CONV, w), _of_layer(l, 1, w), _of_layer(l, w, 2 * w), _of_layer(l, 1, 2 * w),
            _of_layer(l, 1, w),
        ],
        out_specs=pl.BlockSpec((None, ts, w), lambda b, i: (b, i, 0)),
        out_shape=jax.ShapeDtypeStruct((bsz, s, w), BF16),
        scratch_shapes=[pltpu.VMEM((ts + SUBLANES, w), F32), pltpu.VMEM((1, w), F32), pltpu.VMEM((ts, w), F32)],
        compiler_params=_params("parallel", "arbitrary"),
        name="lru_mixer",
    )(t, conv_w, conv_b, wbd, bb, lam)


def _ret_kernel(q_ref, k_ref, v_ref, g_ref, cos_ref, sin_ref, dec_ref, qdec_ref, kvdec_ref, cdec_ref,
                bd_ref, o_ref, st_ref, *, n_chunks):
    c_len = RET_CHUNK
    pairs = RET_HEADS // 2

    @pl.when(pl.program_id(1) == 0)
    def _():
        st_ref[...] = jnp.zeros(st_ref.shape, F32)

    lane = lax.broadcasted_iota(jnp.int32, (c_len, LANES), 1)
    head0 = lane < RET_DK
    first_half = (lane % RET_DK) < (RET_DK // 2)
    inv_n = 1.0 / RET_DV

    def rope(x, cos, sin):
        swapped = jnp.where(first_half, pltpu.roll(x, LANES - RET_DK // 2, 1), pltpu.roll(x, RET_DK // 2, 1))
        return x * cos + swapped * sin

    def head_mean(x):
        m0 = jnp.sum(jnp.where(head0, x, 0.0), axis=-1, keepdims=True) * inv_n
        m1 = jnp.sum(jnp.where(head0, 0.0, x), axis=-1, keepdims=True) * inv_n
        return jnp.where(head0, m0, m1)

    nt = (((1,), (1,)), ((), ()))
    tn = (((0,), (0,)), ((), ()))
    for c in range(n_chunks):
        rows = slice(c * c_len, (c + 1) * c_len)
        cos = cos_ref[rows, :]
        sin = sin_ref[rows, :]
        for p in range(pairs):
            cols = slice(p * LANES, (p + 1) * LANES)
            q = rope(q_ref[rows, cols].astype(F32), cos, sin)
            k = rope(k_ref[rows, cols].astype(F32), cos, sin) * (RET_DK ** -0.5)
            v = v_ref[rows, cols]
            qb = q.astype(BF16)
            kb = k.astype(BF16)
            zero = jnp.zeros_like(qb)
            s0 = lax.dot_general(jnp.where(head0, qb, zero), kb, nt, preferred_element_type=F32)
            s1 = lax.dot_general(jnp.where(head0, zero, qb), kb, nt, preferred_element_type=F32)
            probs = jnp.concatenate([s0 * dec_ref[2 * p], s1 * dec_ref[2 * p + 1]], axis=1).astype(BF16)
            v2 = jnp.concatenate([jnp.where(head0, v, zero), jnp.where(head0, zero, v)], axis=0)
            y = jnp.dot(probs, v2, preferred_element_type=F32)
            state = st_ref[p]
            y = y + jnp.dot(qb, state.astype(BF16), preferred_element_type=F32) * qdec_ref[p]
            vd = (v.astype(F32) * kvdec_ref[p]).astype(BF16)
            kv = lax.dot_general(kb, vd, tn, preferred_element_type=F32)
            st_ref[p] = state * cdec_ref[p] + kv * bd_ref[...]

            d = y - head_mean(y)
            yn = d * lax.rsqrt(head_mean(d * d) + EPS)
            g = g_ref[rows, cols].astype(F32)
            o_ref[rows, cols] = (g * _sigmoid(g) * yn).astype(o_ref.dtype)


def _ret_constants():
    f32 = F32
    log_gamma = jnp.log1p(-(2.0 ** (-5.0 - jnp.arange(RET_HEADS, dtype=f32))))
    idx = jnp.arange(RET_CHUNK, dtype=f32)
    diff = idx[:, None] - idx[None, :]
    causal = diff >= 0
    inner = jnp.where(causal[None], jnp.exp(jnp.where(causal, diff, 0.0)[None] * log_gamma[:, None, None]), 0.0)
    kv_decay = jnp.exp((RET_CHUNK - 1.0 - idx)[None, :] * log_gamma[:, None])
    q_decay = jnp.exp((idx + 1.0)[:, None] * log_gamma[None, :])
    chunk_decay = jnp.exp(RET_CHUNK * log_gamma)
    pairs = RET_HEADS // 2

    def by_lane(per_head):
        rows = per_head.shape[0]
        return jnp.repeat(per_head.reshape(rows, pairs, 2), RET_DK, axis=2).reshape(rows, pairs, LANES).transpose(1, 0, 2)

    qdec = by_lane(q_decay)
    kvdec = by_lane(kv_decay.T)
    cdec = by_lane(chunk_decay[None, :])
    lane_head = jnp.arange(LANES) // RET_DK
    bd = (lane_head[:, None] == lane_head[None, :]).astype(f32)
    return inner, qdec, kvdec, cdec, bd


def _ret(t, cos_r, sin_r):
    bsz, s, _ = t.shape
    tc = 512
    w = RET_HEADS * RET_DK
    inner, qdec, kvdec, cdec, bd = _ret_constants()
    base = COL_RET // w
    col = lambda j: pl.BlockSpec((None, tc, w), lambda b, i: (b, i, base + j))
    tab = pl.BlockSpec((None, tc, LANES), lambda b, i: (b, i, 0))
    const = lambda shape: pl.BlockSpec(shape, lambda b, i: (0,) * len(shape))
    return pl.pallas_call(
        functools.partial(_ret_kernel, n_chunks=tc // RET_CHUNK),
        grid=(bsz, s // tc),
        in_specs=[col(0), col(1), col(2), col(3), tab, tab,
                  const(inner.shape), const(qdec.shape), const(kvdec.shape), const(cdec.shape), const(bd.shape)],
        out_specs=pl.BlockSpec((None, tc, w), lambda b, i: (b, i, 0)),
        out_shape=jax.ShapeDtypeStruct((bsz, s, w), BF16),
        scratch_shapes=[pltpu.VMEM((RET_HEADS // 2, LANES, LANES), F32)],
        compiler_params=_params("parallel", "arbitrary"),
        name="ret_mixer",
    )(t, t, t, t, cos_r, sin_r, inner, qdec, kvdec, cdec, bd)


def _mla_proj_kernel(t_ref, gq_ref, gkv_ref, wq_ref, wkv_ref, cos_ref, sin_ref, qt_ref, k_ref, vt_ref):
    ts = t_ref.shape[0]
    cos = cos_ref[...]
    sin = sin_ref[...]
    lane = lax.broadcasted_iota(jnp.int32, (ts, LANES), 1)
    low_half = lane < (MLA_NOPE + MLA_ROPE // 2)
    half = MLA_ROPE // 2

    def rope(x):
        swapped = jnp.where(low_half, pltpu.roll(x, LANES - half, 1), pltpu.roll(x, half, 1))
        return x * cos + swapped * sin

    ckv = t_ref[:, 0:MLA_KV_RANK].astype(F32)
    kr = t_ref[:, MLA_KV_RANK:MLA_KV_RANK + LANES].astype(F32)
    cq = t_ref[:, MLA_PACK - MLA_Q_RANK:MLA_PACK].astype(F32)

    q = jnp.dot(_rms(cq, gq_ref[...]).astype(BF16), wq_ref[...], preferred_element_type=F32)
    scale = (MLA_NOPE + MLA_ROPE) ** -0.5 * LOG2_E
    for h in range(MLA_HEADS):
        cols = slice(h * LANES, (h + 1) * LANES)
        qt_ref[cols, :] = (rope(q[:, cols]) * scale).T.astype(qt_ref.dtype)

    kvu = jnp.dot(_rms(ckv, gkv_ref[...]).astype(BF16), wkv_ref[...], preferred_element_type=F32)
    k_rope = rope(kr)
    ones_lane = (lane == MLA_V).astype(F32)
    for h in range(MLA_HEADS):
        cols = slice(h * LANES, (h + 1) * LANES)
        k_ref[:, cols] = (kvu[:, cols] + k_rope).astype(k_ref.dtype)
        vcols = slice((MLA_HEADS + h) * LANES, (MLA_HEADS + h + 1) * LANES)
        vt_ref[cols, :] = (kvu[:, vcols] + ones_lane).T.astype(vt_ref.dtype)


def _mla_proj(l, t, gq, gkv, wq, wkv, cos_m, sin_m):
    bsz, s, _ = t.shape
    ts = 512
    hq = MLA_HEADS * LANES
    tab = pl.BlockSpec((None, ts, LANES), lambda b, i: (b, i, 0))
    rowmajor = pl.BlockSpec((None, ts, hq), lambda b, i: (b, i, 0))
    transposed = pl.BlockSpec((None, hq, ts), lambda b, i: (b, 0, i))
    return pl.pallas_call(
        _mla_proj_kernel,
        grid=(bsz, s // ts),
        in_specs=[pl.BlockSpec((None, ts, MLA_PACK), lambda b, i: (b, i, COL_MLA // MLA_PACK)),
                  _of_layer(l, 1, MLA_Q_RANK), _of_layer(l, 1, MLA_KV_RANK), _of_layer(l, *wq.shape[1:]),
                  _of_layer(l, *wkv.shape[1:]), tab, tab],
        out_specs=[transposed, rowmajor, transposed],
        out_shape=[jax.ShapeDtypeStruct((bsz, hq, s), BF16), jax.ShapeDtypeStruct((bsz, s, hq), BF16),
                   jax.ShapeDtypeStruct((bsz, hq, s), BF16)],
        compiler_params=_params("parallel", "parallel"),
        name="mla_proj",
    )(t, gq, gkv, wq, wkv, cos_m, sin_m)


def _flash_kernel(qt_ref, k_ref, vt_ref, o_ref, acc_ref, sa_ref, sb_ref, *, tq):
    i = pl.program_id(2)
    acc_ref[...] = jnp.zeros(acc_ref.shape, F32)

    def qk(j, dst_ref, masked):
        off = pl.multiple_of(j * tq, tq)
        for h in range(2):
            rows = slice(h * LANES, (h + 1) * LANES)
            s = jnp.dot(k_ref[pl.ds(off, tq), rows], qt_ref[rows, :], preferred_element_type=F32)
            if masked:
                key = lax.broadcasted_iota(jnp.int32, (tq, tq), 0)
                qry = lax.broadcasted_iota(jnp.int32, (tq, tq), 1)
                s = jnp.where(key <= qry, s, -1e30)
            dst_ref[h] = s

    def softmax_pv(j, src_ref, ms):
        off = pl.multiple_of(j * tq, tq)
        new_m = []
        for h in range(2):
            s = src_ref[h]
            m_new = jnp.maximum(ms[h], jnp.max(s, axis=0, keepdims=True))
            alpha = jnp.exp2(ms[h] - m_new)
            p = jnp.exp2(s - m_new).astype(BF16)
            vt = vt_ref[h * LANES:h * LANES + V_ROWS, pl.ds(off, tq)]
            acc_ref[h] = alpha * acc_ref[h] + jnp.dot(vt, p, preferred_element_type=F32)
            new_m.append(m_new)
        return tuple(new_m)

    n_pairs = i // 2
    qk(i, sa_ref, True)

    def pair(jj, ms):
        qk(2 * jj, sb_ref, False)
        ms = softmax_pv(jnp.where(jj == 0, i, 2 * jj - 1), sa_ref, ms)
        qk(2 * jj + 1, sa_ref, False)
        return softmax_pv(2 * jj, sb_ref, ms)

    m0 = jnp.full((1, tq), -1e30, F32)
    ms = lax.fori_loop(0, n_pairs, pair, (m0, m0))
    in_a = jnp.where(n_pairs == 0, i, 2 * n_pairs - 1)

    @pl.when(i % 2 == 1)
    def _():
        qk(i - 1, sb_ref, False)
        softmax_pv(i - 1, sb_ref, softmax_pv(in_a, sa_ref, ms))

    @pl.when(i % 2 == 0)
    def _():
        softmax_pv(in_a, sa_ref, ms)

    outs = []
    for h in range(2):
        acc = acc_ref[h]
        outs.append(acc[0:MLA_V, :] / acc[MLA_V:MLA_V + 1, :])
    o_ref[...] = jnp.concatenate(outs, axis=0).T.astype(o_ref.dtype)


def _flash(qt, k, vt):
    bsz, s, _ = k.shape
    tq = 512
    pairs = MLA_HEADS // 2
    return pl.pallas_call(
        functools.partial(_flash_kernel, tq=tq),
        grid=(bsz, pairs, s // tq),
        in_specs=[
            pl.BlockSpec((None, 2 * LANES, tq), lambda b, p, i: (b, p, i)),
            pl.BlockSpec((None, s, 2 * LANES), lambda b, p, i: (b, 0, p)),
            pl.BlockSpec((None, 2 * LANES, s), lambda b, p, i: (b, p, 0)),
        ],
        out_specs=pl.BlockSpec((None, tq, LANES), lambda b, p, i: (b, i, p)),
        out_shape=jax.ShapeDtypeStruct((bsz, s, MLA_HEADS * MLA_V), BF16),
        scratch_shapes=[pltpu.VMEM((2, V_ROWS, tq), F32), pltpu.VMEM((2, tq, tq), F32),
                        pltpu.VMEM((2, tq, tq), F32)],
        compiler_params=_params("parallel", "parallel", "arbitrary"),
        name="mla_flash",
    )(qt, k, vt)


def _merge_kernel(yl_ref, yr_ref, ym_ref, gt_ref, x_ref, wl_ref, wr_ref, wm_ref, wo_ref, gp_ref, gm_ref, o_ref):
    d = D_MODEL
    merged = _sigmoid(gt_ref[:, 0:d].astype(F32)) * jnp.dot(yl_ref[...], wl_ref[...], preferred_element_type=F32)
    merged = merged + _sigmoid(gt_ref[:, d:2 * d].astype(F32)) * jnp.dot(
        yr_ref[...], wr_ref[...], preferred_element_type=F32)
    merged = merged + _sigmoid(gt_ref[:, 2 * d:3 * d].astype(F32)) * jnp.dot(
        ym_ref[...], wm_ref[...], preferred_element_type=F32)
    y = jnp.dot(merged.astype(BF16), wo_ref[...], preferred_element_type=F32)
    o_ref[...] = x_ref[...] + gm_ref[...] * _rms(y, gp_ref[...])


def _merge(l, y_lru, y_ret, y_mla, t, x, wl, wr, wm, wo, g_post, mod, k_gate):
    bsz, s, d = x.shape
    tm = 512
    w = y_lru.shape[-1]
    br = pl.BlockSpec((None, tm, w), lambda b, i: (b, i, 0))
    return pl.pallas_call(
        _merge_kernel,
        grid=(bsz, s // tm),
        in_specs=[br, br, br,
                  pl.BlockSpec((None, tm, 3 * d), lambda b, i: (b, i, COL_GATE // (3 * d))),
                  pl.BlockSpec((None, tm, d), lambda b, i: (b, i, 0)),
                  _of_layer(l, w, d), _of_layer(l, w, d), _of_layer(l, w, d), _of_layer(l, d, d),
                  _of_layer(l, 1, d), _mod_spec(l, k_gate, d)],
        out_specs=pl.BlockSpec((None, tm, d), lambda b, i: (b, i, 0)),
        out_shape=jax.ShapeDtypeStruct((bsz, s, d), F32),
        compiler_params=_params("parallel", "parallel"),
        name="mixer_merge",
    )(y_lru, y_ret, y_mla, t, x, wl, wr, wm, wo, g_post, mod)


def _ffn_kernel(x_ref, gpre_ref, sh_ref, sc_ref, wu_ref, cw_ref, cb_ref, wd_ref, gp_ref, gf_ref, o_ref,
                act_ref, halo_ref, *, tm, cw):
    @pl.when(pl.program_id(1) == 0)
    def _():
        halo_ref[...] = jnp.zeros(halo_ref.shape, F32)

    x = x_ref[...]
    h = (_rms(x, gpre_ref[...]) * (1.0 + sc_ref[...]) + sh_ref[...]).astype(BF16)
    row = lax.broadcasted_iota(jnp.int32, (SUBLANES, cw), 0)

    def conv(c0, gain):
        cols = slice(c0, c0 + cw)
        taps = cw_ref[:, cols] * gain
        bias = cb_ref[:, cols] * gain
        xv = jnp.dot(h, wu_ref[:, cols], preferred_element_type=F32)
        h1 = halo_ref[SUBLANES - 1:SUBLANES, cols]
        h2 = halo_ref[SUBLANES - 2:SUBLANES - 1, cols]
        halo_ref[:, cols] = xv[tm - SUBLANES:, :]
        r1 = pltpu.roll(xv, 1, 0)
        r2 = pltpu.roll(xv, 2, 0)
        xm1 = jnp.concatenate([jnp.where(row == 0, h1, r1[:SUBLANES]), r1[SUBLANES:]], axis=0)
        top2 = jnp.where(row == 0, h2, jnp.where(row == 1, h1, r2[:SUBLANES]))
        xm2 = jnp.concatenate([top2, r2[SUBLANES:]], axis=0)
        y = xv * taps[2:3] + bias
        y = y + xm2 * taps[0:1]
        return y + xm1 * taps[1:2]

    for c in range(D_FF // cw):
        hu = conv(c * cw, 0.5)
        g = conv(D_FF + c * cw, 1.0)
        th = jnp.tanh(g * (GELU_C1 + GELU_C2 * (g * g)))
        act_ref[:, c * cw:(c + 1) * cw] = ((hu * g) * (1.0 + th)).astype(BF16)
    y = jnp.dot(act_ref[...], wd_ref[...], preferred_element_type=F32)
    o_ref[...] = x + gf_ref[...] * _rms(y, gp_ref[...])


def _ffn(l, x, g_pre, mod, k_shift, k_scale, k_gate, wu, conv_w, conv_b, wd, g_post):
    bsz, s, d = x.shape
    tm = 512
    n = wu.shape[-1]
    single = dict(pipeline_mode=pl.Buffered(1))
    return pl.pallas_call(
        functools.partial(_ffn_kernel, tm=tm, cw=256),
        grid=(bsz, s // tm),
        in_specs=[pl.BlockSpec((None, tm, d), lambda b, i: (b, i, 0)),
                  _of_layer(l, 1, d), _mod_spec(l, k_shift, d), _mod_spec(l, k_scale, d),
                  _of_layer(l, d, n, **single), _of_layer(l, FFN_CONV, n), _of_layer(l, 1, n),
                  _of_layer(l, D_FF, d, **single), _of_layer(l, 1, d), _mod_spec(l, k_gate, d)],
        out_specs=pl.BlockSpec((None, tm, d), lambda b, i: (b, i, 0)),
        out_shape=jax.ShapeDtypeStruct((bsz, s, d), F32),
        scratch_shapes=[pltpu.VMEM((tm, D_FF), BF16), pltpu.VMEM((SUBLANES, n), F32)],
        compiler_params=_params("parallel", "arbitrary"),
        name="ffn_fused",
    )(x, g_pre, mod, mod, wu, conv_w, conv_b, wd, g_post, mod)


def _pack_w_in_kernel(w_ref, o_ref):
    o_cq = 2 * LRU_WIDTH + 4 * RET_HEADS * RET_DK
    o_ckv = o_cq + MLA_Q_RANK
    o_kr = o_ckv + MLA_KV_RANK
    o_gate = o_kr + MLA_ROPE
    n_gate = w_ref.shape[1] - o_gate
    rows = w_ref.shape[0]
    o_ref[:, 0:o_cq] = w_ref[:, 0:o_cq].astype(BF16)
    o_ref[:, COL_GATE:COL_GATE + n_gate] = w_ref[:, o_gate:o_gate + n_gate].astype(BF16)
    o_ref[:, COL_MLA:COL_MLA + MLA_KV_RANK] = w_ref[:, o_ckv:o_kr].astype(BF16)
    kr = jnp.concatenate([jnp.zeros((rows, KR_LANE), F32), w_ref[:, o_kr:o_gate],
                          jnp.zeros((rows, LANES - KR_LANE - MLA_ROPE), F32)], axis=1)
    o_ref[:, COL_MLA + MLA_KV_RANK:COL_MLA + MLA_KV_RANK + LANES] = kr.astype(BF16)
    o_ref[:, N_IN_PACKED - MLA_Q_RANK:N_IN_PACKED] = w_ref[:, o_cq:o_ckv].astype(BF16)


def _pack_w_in(w):
    depth, d, n = w.shape
    tr = 256
    return pl.pallas_call(
        _pack_w_in_kernel,
        grid=(depth, d // tr),
        in_specs=[pl.BlockSpec((None, tr, n), lambda l, i: (l, i, 0))],
        out_specs=pl.BlockSpec((None, tr, N_IN_PACKED), lambda l, i: (l, i, 0)),
        out_shape=jax.ShapeDtypeStruct((depth, d, N_IN_PACKED), BF16),
        compiler_params=_params("parallel", "parallel"),
        name="pack_w_in",
    )(w)


def _pad_heads(w, width):
    depth, r, _ = w.shape
    w4 = w.reshape(depth, r, MLA_HEADS, width)
    return jnp.pad(w4, ((0, 0), (0, 0), (0, 0), (0, LANES - width))).reshape(depth, r, MLA_HEADS * LANES)


def _pack_w_ukv(w):
    depth, r, _ = w.shape
    w4 = w.reshape(depth, r, MLA_HEADS, MLA_NOPE + MLA_V)
    k = _pad_heads(w4[..., :MLA_NOPE].reshape(depth, r, -1), MLA_NOPE)
    v = _pad_heads(w4[..., MLA_NOPE:].reshape(depth, r, -1), MLA_V)
    return jnp.concatenate([k, v], axis=-1).astype(BF16)


def _block_diag(w):
    depth, nb, n, _ = w.shape
    eye = jnp.eye(nb, dtype=w.dtype)
    return (eye[None, :, None, :, None] * w[:, :, :, None, :]).reshape(depth, nb * n, nb * n)


def kernel(x, c, positions, ada_w, ada_b, mix_pre_g, mix_post_g, w_in, lru_conv_w, lru_conv_b, lru_wa, lru_ba, lru_wx, lru_bx, lru_lambda, lru_wo, ret_wo, mla_q_norm_g, mla_w_uq, mla_kv_norm_g, mla_w_ukv, mla_wo, w_out, ffn_pre_g, ffn_post_g, ffn_w_up, ffn_conv_w, ffn_conv_b, ffn_w_down):
    bsz, s, d = x.shape
    depth = w_in.shape[0]
    row = lambda a: a.reshape(depth, 1, a.shape[-1])
    c_pad = jnp.pad(c, ((0, SUBLANES - bsz), (0, 0)))
    mod = _ada(c_pad, ada_w, ada_b).reshape(depth, SUBLANES, 6, 1, d)
    cos_r, sin_r, cos_m, sin_m = _rope_tables(positions)

    w_in_p = _pack_w_in(w_in)
    wbd = jnp.concatenate([_block_diag(lru_wa), _block_diag(lru_wx)], axis=-1).astype(BF16)
    bb = jnp.concatenate([lru_ba, lru_bx], axis=-1).reshape(depth, 1, 2 * LRU_WIDTH)
    w_uq = _pad_heads(mla_w_uq, MLA_NOPE + MLA_ROPE).astype(BF16)
    w_ukv = _pack_w_ukv(mla_w_ukv)
    wl, wr, wm, wo = (a.astype(BF16) for a in (lru_wo, ret_wo, mla_wo, w_out))
    wu, wd = ffn_w_up.astype(BF16), ffn_w_down.astype(BF16)

    for l in range(depth):
        t = _prenorm_matmul(l, x, row(mix_pre_g), mod, 0, 1, w_in_p, 2304, "mixer_in_proj")
        y_lru = _lru(l, t, lru_conv_w, row(lru_conv_b), wbd, bb, row(lru_lambda))
        y_ret = _ret(t, cos_r, sin_r)
        qt, k, vt = _mla_proj(l, t, row(mla_q_norm_g), row(mla_kv_norm_g), w_uq, w_ukv, cos_m, sin_m)
        y_mla = _flash(qt, k, vt)
        x = _merge(l, y_lru, y_ret, y_mla, t, x, wl, wr, wm, wo, row(mix_post_g), mod, 2)
        x = _ffn(l, x, row(ffn_pre_g), mod, 3, 4, 5, wu, ffn_conv_w, row(ffn_conv_b), wd, row(ffn_post_g))
    return x
```

```python
import functools

import jax
import jax.numpy as jnp
import numpy as np
from jax import lax
from jax.experimental import pallas as pl
from jax.experimental.pallas import tpu as pltpu

F32 = jnp.float32
BF16 = jnp.bfloat16

D_MODEL = 1024
DEPTH = 2
EPS = 1e-6
ROPE_THETA = 10000.0
LRU_WIDTH = 512
LRU_BLOCKS = 8
LRU_BLOCK = LRU_WIDTH // LRU_BLOCKS
LRU_CONV = 4
LRU_C = 8.0
RET_HEADS = 8
RET_DK = 64
RET_DV = 64
RET_CHUNK = 128
MLA_HEADS = 8
MLA_Q_RANK = 384
MLA_KV_RANK = 256
MLA_NOPE = 64
MLA_ROPE = 32
MLA_V = 64
D_FF = 2816
FFN_CONV = 3

LANES = 128
SUBLANES = 8
VMEM_LIMIT = 56 * 1024 * 1024

COL_LRU = 0
COL_RET = 1024
COL_GATE = 3072
COL_MLA = 6144
N_IN_PACKED = 6912
MLA_PACK = 768
KR_LANE = 64
LOG2_E = 1.4426950408889634
FLASH_HEADS = 4
V_ROWS = MLA_V + 16


def _params(*sem):
    return pltpu.CompilerParams(dimension_semantics=sem, vmem_limit_bytes=VMEM_LIMIT)


def _of_layer(l, *tail, **kw):
    return pl.BlockSpec((None,) + tail, lambda *_: (l,) + (0,) * len(tail), **kw)


def _mod_spec(l, k, d):
    return pl.BlockSpec((None, None, None, 1, d), lambda b, *_: (l, b, k, 0, 0))


GELU_C1 = 0.7978845608028654
GELU_C2 = GELU_C1 * 0.044715


def _gelu_tanh(x):
    return 0.5 * x * (1.0 + jnp.tanh(x * (GELU_C1 + GELU_C2 * (x * x))))


def _sigmoid(x):
    return 1.0 / (1.0 + jnp.exp(-x))


def _rms(x, g):
    return x * lax.rsqrt(jnp.mean(x * x, axis=-1, keepdims=True) + EPS) * g


def _ada_kernel(c_ref, w_ref, b_ref, o_ref):
    c = c_ref[...]
    ca = c * _sigmoid(c)
    o_ref[...] = jnp.dot(ca, w_ref[...], preferred_element_type=F32,
                         precision=lax.Precision.HIGHEST) + b_ref[...]


def _ada(c_pad, ada_w, ada_b):
    depth, d, n = ada_w.shape
    rows = c_pad.shape[0]
    tn = 1536
    return pl.pallas_call(
        _ada_kernel,
        grid=(depth, n // tn),
        in_specs=[
            pl.BlockSpec((rows, d), lambda l, j: (0, 0)),
            pl.BlockSpec((None, d, tn), lambda l, j: (l, 0, j)),
            pl.BlockSpec((None, 1, tn), lambda l, j: (l, 0, j)),
        ],
        out_specs=pl.BlockSpec((None, rows, tn), lambda l, j: (l, 0, j)),
        out_shape=jax.ShapeDtypeStruct((depth, rows, n), F32),
        compiler_params=_params("parallel", "parallel"),
        name="ada_mod",
    )(c_pad, ada_w, ada_b.reshape(depth, 1, n))


def _rope_kernel(pos_ref, inv_ref, cr_ref, sr_ref, cm_ref, sm_ref, cmt_ref, smt_ref):
    half_r = RET_DK // 2
    half_m = MLA_ROPE // 2
    ang = pos_ref[...].astype(F32) * inv_ref[...]
    c = jnp.cos(ang)
    s = jnp.sin(ang)
    c_r, s_r = c[:, :half_r], s[:, :half_r]
    c_m, s_m = c[:, half_r:half_r + half_m], s[:, half_r:half_r + half_m]
    reps = LANES // RET_DK
    cr_ref[...] = jnp.concatenate([c_r, c_r] * reps, axis=1)
    sr_ref[...] = jnp.concatenate([-s_r, s_r] * reps, axis=1)
    ts = ang.shape[0]
    tail = LANES - MLA_NOPE - MLA_ROPE
    cm_ref[...] = jnp.concatenate([jnp.ones((ts, MLA_NOPE), F32), c_m, c_m, jnp.ones((ts, tail), F32)], axis=1)
    sm_ref[...] = jnp.concatenate([jnp.zeros((ts, MLA_NOPE), F32), -s_m, s_m, jnp.zeros((ts, tail), F32)], axis=1)
    cmt_ref[...] = c.T[half_r:half_r + half_m]
    smt_ref[...] = s.T[half_r:half_r + half_m]


def _rope_tables(positions):
    bsz, s = positions.shape
    ts = 1024
    inv_r = ROPE_THETA ** (-jnp.arange(0, RET_DK, 2, dtype=F32) / RET_DK)
    inv_m = ROPE_THETA ** (-jnp.arange(0, MLA_ROPE, 2, dtype=F32) / MLA_ROPE)
    inv = jnp.concatenate([inv_r, inv_m, jnp.zeros(LANES - inv_r.size - inv_m.size, F32)]).reshape(1, LANES)
    tab = pl.BlockSpec((None, ts, LANES), lambda b, i: (b, i, 0))
    shp = jax.ShapeDtypeStruct((bsz, s, LANES), F32)
    tab_t = pl.BlockSpec((None, inv_m.size, ts), lambda b, i: (b, 0, i))
    shp_t = jax.ShapeDtypeStruct((bsz, inv_m.size, s), F32)
    return pl.pallas_call(
        _rope_kernel,
        grid=(bsz, s // ts),
        in_specs=[pl.BlockSpec((None, ts, 1), lambda b, i: (b, i, 0)), pl.BlockSpec((1, LANES), lambda b, i: (0, 0))],
        out_specs=[tab, tab, tab, tab, tab_t, tab_t],
        out_shape=[shp, shp, shp, shp, shp_t, shp_t],
        compiler_params=_params("parallel", "parallel"),
        name="rope_tables",
    )(positions.reshape(bsz, s, 1), inv)


def _prenorm_matmul_kernel(x_ref, g_ref, sh_ref, sc_ref, w_ref, o_ref, *, tn):
    h = (_rms(x_ref[...], g_ref[...]) * (1.0 + sc_ref[...]) + sh_ref[...]).astype(BF16)
    for j in range(w_ref.shape[1] // tn):
        cols = slice(j * tn, (j + 1) * tn)
        o_ref[:, cols] = jnp.dot(h, w_ref[:, cols], preferred_element_type=F32).astype(o_ref.dtype)


def _prenorm_matmul(l, x, g, mod, k_shift, k_scale, w, tn, name):
    bsz, s, d = x.shape
    n = w.shape[-1]
    tm = 512
    return pl.pallas_call(
        functools.partial(_prenorm_matmul_kernel, tn=tn),
        grid=(bsz, s // tm),
        in_specs=[
            pl.BlockSpec((None, tm, d), lambda b, i: (b, i, 0)),
            _of_layer(l, 1, d),
            _mod_spec(l, k_shift, d), _mod_spec(l, k_scale, d),
            _of_layer(l, d, n, pipeline_mode=pl.Buffered(1)),
        ],
        out_specs=pl.BlockSpec((None, tm, n), lambda b, i: (b, i, 0)),
        out_shape=jax.ShapeDtypeStruct((bsz, s, n), BF16),
        compiler_params=_params("parallel", "parallel"),
        name=name,
    )(x, g, mod, mod, w)


def _lru_kernel(t_ref, cw_ref, cb_ref, wbd_ref, bb_ref, lam_ref, o_ref, xbuf_ref, hc_ref, h_ref, *, ts):
    w = LRU_WIDTH

    @pl.when(pl.program_id(1) == 0)
    def _():
        xbuf_ref[0:SUBLANES, :] = jnp.zeros((SUBLANES, w), F32)
        hc_ref[...] = jnp.zeros((1, w), F32)

    xb = t_ref[:, 0:w].astype(F32)
    gb = t_ref[:, w:2 * w].astype(F32)
    xbuf_ref[SUBLANES:SUBLANES + ts, :] = xb
    xc = xb * cw_ref[LRU_CONV - 1:LRU_CONV, :] + cb_ref[...]
    for k in range(LRU_CONV - 1):
        back = LRU_CONV - 1 - k
        xc = xc + xbuf_ref[pl.ds(SUBLANES - back, ts), :] * cw_ref[k:k + 1, :]
    xbuf_ref[0:SUBLANES, :] = xb[ts - SUBLANES:, :]

    z = jnp.dot(xc.astype(BF16), wbd_ref[...], preferred_element_type=F32) + bb_ref[...]
    r = _sigmoid(z[:, :w])
    ig = _sigmoid(z[:, w:])
    nl = -lam_ref[...]
    softplus = jnp.maximum(nl, 0.0) + jnp.log1p(jnp.exp(-jnp.abs(nl)))
    log_a = (-LRU_C) * r * softplus
    a = jnp.exp(log_a)
    th = jnp.tanh(log_a)
    u = jnp.sqrt(-2.0 * th / (1.0 - th)) * (ig * xc)

    groups = ts // SUBLANES
    a = a.reshape(groups, SUBLANES, w)
    u = u.reshape(groups, SUBLANES, w)
    sub = lax.broadcasted_iota(jnp.int32, (groups, SUBLANES, w), 1)
    k = 1
    while k < SUBLANES:
        keep = sub >= k
        a_prev = jnp.where(keep, pltpu.roll(a, k, 1), 1.0)
        u_prev = jnp.where(keep, pltpu.roll(u, k, 1), 0.0)
        u = a * u_prev + u
        a = a * a_prev
        k *= 2
    h_prev = hc_ref[...]
    for r in range(groups):
        hb = a[r] * h_prev + u[r]
        h_ref[r * SUBLANES:(r + 1) * SUBLANES, :] = hb
        h_prev = hb[SUBLANES - 1:SUBLANES, :]
    hc_ref[...] = h_prev
    o_ref[...] = (h_ref[...] * _gelu_tanh(gb)).astype(o_ref.dtype)


def _lru(l, t, conv_w, conv_b, wbd, bb, lam):
    bsz, s, _ = t.shape
    ts = 256
    w = LRU_WIDTH
    return pl.pallas_call(
        functools.partial(_lru_kernel, ts=ts),
        grid=(bsz, s // ts),
        in_specs=[
            pl.BlockSpec((None, ts, 2 * w), lambda b, i: (b, i, COL_LRU // (2 * w))),
            _of_layer(l, LRU_CONV, w), _of_layer(l, 1, w), _of_layer(l, w, 2 * w), _of_layer(l, 1, 2 * w),
            _of_layer(l, 1, w),
        ],
        out_specs=pl.BlockSpec((None, ts, w), lambda b, i: (b, i, 0)),
        out_shape=jax.ShapeDtypeStruct((bsz, s, w), BF16),
        scratch_shapes=[pltpu.VMEM((ts + SUBLANES, w), F32), pltpu.VMEM((1, w), F32), pltpu.VMEM((ts, w), F32)],
        compiler_params=_params("parallel", "arbitrary"),
        name="lru_mixer",
    )(t, conv_w, conv_b, wbd, bb, lam)


def _ret_kernel(q_ref, k_ref, v_ref, g_ref, cos_ref, sin_ref, dec_ref, qdec_ref, kvdec_ref, cdec_ref,
                bd_ref, o_ref, st_ref, *, n_chunks):
    c_len = RET_CHUNK
    pairs = RET_HEADS // 2

    @pl.when(pl.program_id(1) == 0)
    def _():
        st_ref[...] = jnp.zeros(st_ref.shape, F32)

    lane = lax.broadcasted_iota(jnp.int32, (c_len, LANES), 1)
    head0 = lane < RET_DK
    first_half = (lane % RET_DK) < (RET_DK // 2)
    inv_n = 1.0 / RET_DV

    def rope(x, cos, sin):
        swapped = jnp.where(first_half, pltpu.roll(x, LANES - RET_DK // 2, 1), pltpu.roll(x, RET_DK // 2, 1))
        return x * cos + swapped * sin

    def head_mean(x):
        m0 = jnp.sum(jnp.where(head0, x, 0.0), axis=-1, keepdims=True) * inv_n
        m1 = jnp.sum(jnp.where(head0, 0.0, x), axis=-1, keepdims=True) * inv_n
        return jnp.where(head0, m0, m1)

    nt = (((1,), (1,)), ((), ()))
    tn = (((0,), (0,)), ((), ()))
    for c in range(n_chunks):
        rows = slice(c * c_len, (c + 1) * c_len)
        cos = cos_ref[rows, :]
        sin = sin_ref[rows, :]
        for p in range(pairs):
            cols = slice(p * LANES, (p + 1) * LANES)
            q = rope(q_ref[rows, cols].astype(F32), cos, sin)
            k = rope(k_ref[rows, cols].astype(F32), cos, sin) * (RET_DK ** -0.5)
            v = v_ref[rows, cols]
            qb = q.astype(BF16)
            kb = k.astype(BF16)
            zero = jnp.zeros_like(qb)
            s0 = lax.dot_general(jnp.where(head0, qb, zero), kb, nt, preferred_element_type=F32)
            s1 = lax.dot_general(jnp.where(head0, zero, qb), kb, nt, preferred_element_type=F32)
            probs = jnp.concatenate([s0 * dec_ref[2 * p], s1 * dec_ref[2 * p + 1]], axis=1).astype(BF16)
            v2 = jnp.concatenate([jnp.where(head0, v, zero), jnp.where(head0, zero, v)], axis=0)
            y = jnp.dot(probs, v2, preferred_element_type=F32)
            state = st_ref[p]
            y = y + jnp.dot(qb, state.astype(BF16), preferred_element_type=F32) * qdec_ref[p]
            vd = (v.astype(F32) * kvdec_ref[p]).astype(BF16)
            kv = lax.dot_general(kb, vd, tn, preferred_element_type=F32)
            st_ref[p] = state * cdec_ref[p] + kv * bd_ref[...]

            d = y - head_mean(y)
            yn = d * lax.rsqrt(head_mean(d * d) + EPS)
            g = g_ref[rows, cols].astype(F32)
            o_ref[rows, cols] = (g * _sigmoid(g) * yn).astype(o_ref.dtype)


def _ret_constants():
    f32 = F32
    log_gamma = jnp.log1p(-(2.0 ** (-5.0 - jnp.arange(RET_HEADS, dtype=f32))))
    idx = jnp.arange(RET_CHUNK, dtype=f32)
    diff = idx[:, None] - idx[None, :]
    causal = diff >= 0
    inner = jnp.where(causal[None], jnp.exp(jnp.where(causal, diff, 0.0)[None] * log_gamma[:, None, None]), 0.0)
    kv_decay = jnp.exp((RET_CHUNK - 1.0 - idx)[None, :] * log_gamma[:, None])
    q_decay = jnp.exp((idx + 1.0)[:, None] * log_gamma[None, :])
    chunk_decay = jnp.exp(RET_CHUNK * log_gamma)
    pairs = RET_HEADS // 2

    def by_lane(per_head):
        rows = per_head.shape[0]
        return jnp.repeat(per_head.reshape(rows, pairs, 2), RET_DK, axis=2).reshape(rows, pairs, LANES).transpose(1, 0, 2)

    qdec = by_lane(q_decay)
    kvdec = by_lane(kv_decay.T)
    cdec = by_lane(chunk_decay[None, :])
    lane_head = jnp.arange(LANES) // RET_DK
    bd = (lane_head[:, None] == lane_head[None, :]).astype(f32)
    return inner, qdec, kvdec, cdec, bd


def _ret(t, cos_r, sin_r):
    bsz, s, _ = t.shape
    tc = 512
    w = RET_HEADS * RET_DK
    inner, qdec, kvdec, cdec, bd = _ret_constants()
    base = COL_RET // w
    col = lambda j: pl.BlockSpec((None, tc, w), lambda b, i: (b, i, base + j))
    tab = pl.BlockSpec((None, tc, LANES), lambda b, i: (b, i, 0))
    const = lambda shape: pl.BlockSpec(shape, lambda b, i: (0,) * len(shape))
    return pl.pallas_call(
        functools.partial(_ret_kernel, n_chunks=tc // RET_CHUNK),
        grid=(bsz, s // tc),
        in_specs=[col(0), col(1), col(2), col(3), tab, tab,
                  const(inner.shape), const(qdec.shape), const(kvdec.shape), const(cdec.shape), const(bd.shape)],
        out_specs=pl.BlockSpec((None, tc, w), lambda b, i: (b, i, 0)),
        out_shape=jax.ShapeDtypeStruct((bsz, s, w), BF16),
        scratch_shapes=[pltpu.VMEM((RET_HEADS // 2, LANES, LANES), F32)],
        compiler_params=_params("parallel", "arbitrary"),
        name="ret_mixer",
    )(t, t, t, t, cos_r, sin_r, inner, qdec, kvdec, cdec, bd)


def _mla_proj_kernel(t_ref, gq_ref, gkv_ref, wqt_ref, wk_ref, wvt_ref, cos_ref, sin_ref, cost_ref, sint_ref,
                     qt_ref, k_ref, vt_ref):
    ts = t_ref.shape[0]
    half = MLA_ROPE // 2
    nt = (((1,), (1,)), ((), ()))
    ckv = t_ref[:, 0:MLA_KV_RANK].astype(F32)
    kr = t_ref[:, MLA_KV_RANK:MLA_KV_RANK + LANES].astype(F32)
    cq = t_ref[:, MLA_PACK - MLA_Q_RANK:MLA_PACK].astype(F32)
    cqn = _rms(cq, gq_ref[...]).astype(BF16)
    ckvn = _rms(ckv, gkv_ref[...]).astype(BF16)

    qt = lax.dot_general(wqt_ref[...], cqn, nt, preferred_element_type=F32)
    cos_t = cost_ref[...]
    sin_t = sint_ref[...]
    scale = (MLA_NOPE + MLA_ROPE) ** -0.5 * LOG2_E
    for h in range(MLA_HEADS):
        base = h * LANES
        x1 = qt[base + MLA_NOPE:base + MLA_NOPE + half]
        x2 = qt[base + MLA_NOPE + half:base + MLA_NOPE + MLA_ROPE]
        blk = jnp.concatenate([qt[base:base + MLA_NOPE], x1 * cos_t - x2 * sin_t, x1 * sin_t + x2 * cos_t,
                               qt[base + MLA_NOPE + MLA_ROPE:base + LANES]], axis=0)
        qt_ref[base:base + LANES, :] = (blk * scale).astype(qt_ref.dtype)

    lane = lax.broadcasted_iota(jnp.int32, (ts, LANES), 1)
    swapped = jnp.where(lane < MLA_NOPE + half, pltpu.roll(kr, LANES - half, 1), pltpu.roll(kr, half, 1))
    k_rope = kr * cos_ref[...] + swapped * sin_ref[...]
    kn = jnp.dot(ckvn, wk_ref[...], preferred_element_type=F32)
    for h in range(MLA_HEADS):
        cols = slice(h * LANES, (h + 1) * LANES)
        k_ref[:, cols] = (kn[:, cols] + k_rope).astype(k_ref.dtype)

    vt = lax.dot_general(wvt_ref[...], ckvn, nt, preferred_element_type=F32)
    row = lax.broadcasted_iota(jnp.int32, vt.shape, 0)
    vt_ref[...] = jnp.where(row % LANES == MLA_V, 1.0, vt).astype(vt_ref.dtype)


def _mla_proj(l, t, gq, gkv, wqt, wk, wvt, cos_m, sin_m, cos_mt, sin_mt):
    bsz, s, _ = t.shape
    ts = 512
    hq = MLA_HEADS * LANES
    half = MLA_ROPE // 2
    tab = pl.BlockSpec((None, ts, LANES), lambda b, i: (b, i, 0))
    tab_t = pl.BlockSpec((None, half, ts), lambda b, i: (b, 0, i))
    rowmajor = pl.BlockSpec((None, ts, hq), lambda b, i: (b, i, 0))
    transposed = pl.BlockSpec((None, hq, ts), lambda b, i: (b, 0, i))
    return pl.pallas_call(
        _mla_proj_kernel,
        grid=(bsz, s // ts),
        in_specs=[pl.BlockSpec((None, ts, MLA_PACK), lambda b, i: (b, i, COL_MLA // MLA_PACK)),
                  _of_layer(l, 1, MLA_Q_RANK), _of_layer(l, 1, MLA_KV_RANK), _of_layer(l, *wqt.shape[1:]),
                  _of_layer(l, *wk.shape[1:]), _of_layer(l, *wvt.shape[1:]), tab, tab, tab_t, tab_t],
        out_specs=[transposed, rowmajor, transposed],
        out_shape=[jax.ShapeDtypeStruct((bsz, hq, s), BF16), jax.ShapeDtypeStruct((bsz, s, hq), BF16),
                   jax.ShapeDtypeStruct((bsz, hq, s), BF16)],
        compiler_params=_params("parallel", "parallel"),
        name="mla_proj",
    )(t, gq, gkv, wqt, wk, wvt, cos_m, sin_m, cos_mt, sin_mt)


def _flash_kernel(qt_ref, k_ref, vt_ref, o_ref, acc_ref, sa_ref, sb_ref, *, tq):
    i = pl.program_id(2)
    acc_ref[...] = jnp.zeros(acc_ref.shape, F32)

    def qk(j, dst_ref, masked):
        off = pl.multiple_of(j * tq, tq)
        for h in range(FLASH_HEADS):
            rows = slice(h * LANES, (h + 1) * LANES)
            s = jnp.dot(k_ref[pl.ds(off, tq), rows], qt_ref[rows, :], preferred_element_type=F32)
            if masked:
                key = lax.broadcasted_iota(jnp.int32, (tq, tq), 0)
                qry = lax.broadcasted_iota(jnp.int32, (tq, tq), 1)
                s = jnp.where(key <= qry, s, -1e30)
            dst_ref[h] = s

    def softmax_pv(j, src_ref, ms):
        off = pl.multiple_of(j * tq, tq)
        new_m = []
        for h in range(FLASH_HEADS):
            s = src_ref[h]
            m_new = jnp.maximum(ms[h], jnp.max(s, axis=0, keepdims=True))
            alpha = jnp.exp2(ms[h] - m_new)
            p = jnp.exp2(s - m_new).astype(BF16)
            vt = vt_ref[h * LANES:h * LANES + V_ROWS, pl.ds(off, tq)]
            acc_ref[h] = alpha * acc_ref[h] + jnp.dot(vt, p, preferred_element_type=F32)
            new_m.append(m_new)
        return tuple(new_m)

    n_pairs = i // 2
    qk(i, sa_ref, True)

    def pair(jj, ms):
        qk(2 * jj, sb_ref, False)
        ms = softmax_pv(jnp.where(jj == 0, i, 2 * jj - 1), sa_ref, ms)
        qk(2 * jj + 1, sa_ref, False)
        return softmax_pv(2 * jj, sb_ref, ms)

    m0 = jnp.full((1, tq), -1e30, F32)
    ms = lax.fori_loop(0, n_pairs, pair, (m0,) * FLASH_HEADS)
    in_a = jnp.where(n_pairs == 0, i, 2 * n_pairs - 1)

    @pl.when(i % 2 == 1)
    def _():
        qk(i - 1, sb_ref, False)
        softmax_pv(i - 1, sb_ref, softmax_pv(in_a, sa_ref, ms))

    @pl.when(i % 2 == 0)
    def _():
        softmax_pv(in_a, sa_ref, ms)

    for g in range(FLASH_HEADS // 2):
        outs = []
        for h in (2 * g, 2 * g + 1):
            acc = acc_ref[h]
            outs.append(acc[0:MLA_V, :] / acc[MLA_V:MLA_V + 1, :])
        o_ref[:, g * LANES:(g + 1) * LANES] = jnp.concatenate(outs, axis=0).T.astype(o_ref.dtype)


def _flash(qt, k, vt):
    bsz, s, _ = k.shape
    tq = 512
    hs = FLASH_HEADS
    return pl.pallas_call(
        functools.partial(_flash_kernel, tq=tq),
        grid=(bsz, MLA_HEADS // hs, s // tq),
        in_specs=[
            pl.BlockSpec((None, hs * LANES, tq), lambda b, p, i: (b, p, i)),
            pl.BlockSpec((None, s, hs * LANES), lambda b, p, i: (b, 0, p)),
            pl.BlockSpec((None, hs * LANES, s), lambda b, p, i: (b, p, 0)),
        ],
        out_specs=pl.BlockSpec((None, tq, hs * MLA_V), lambda b, p, i: (b, i, p)),
        out_shape=jax.ShapeDtypeStruct((bsz, s, MLA_HEADS * MLA_V), BF16),
        scratch_shapes=[pltpu.VMEM((hs, V_ROWS, tq), F32), pltpu.VMEM((hs, tq, tq), F32),
                        pltpu.VMEM((hs, tq, tq), F32)],
        compiler_params=_params("parallel", "parallel", "arbitrary"),
        name="mla_flash",
    )(qt, k, vt)


def _merge_kernel(yl_ref, yr_ref, ym_ref, gt_ref, x_ref, wl_ref, wr_ref, wm_ref, wo_ref, gp_ref, gm_ref, o_ref):
    d = D_MODEL
    merged = _sigmoid(gt_ref[:, 0:d].astype(F32)) * jnp.dot(yl_ref[...], wl_ref[...], preferred_element_type=F32)
    merged = merged + _sigmoid(gt_ref[:, d:2 * d].astype(F32)) * jnp.dot(
        yr_ref[...], wr_ref[...], preferred_element_type=F32)
    merged = merged + _sigmoid(gt_ref[:, 2 * d:3 * d].astype(F32)) * jnp.dot(
        ym_ref[...], wm_ref[...], preferred_element_type=F32)
    y = jnp.dot(merged.astype(BF16), wo_ref[...], preferred_element_type=F32)
    o_ref[...] = x_ref[...] + gm_ref[...] * _rms(y, gp_ref[...])


def _merge(l, y_lru, y_ret, y_mla, t, x, wl, wr, wm, wo, g_post, mod, k_gate):
    bsz, s, d = x.shape
    tm = 512
    w = y_lru.shape[-1]
    br = pl.BlockSpec((None, tm, w), lambda b, i: (b, i, 0))
    return pl.pallas_call(
        _merge_kernel,
        grid=(bsz, s // tm),
        in_specs=[br, br, br,
                  pl.BlockSpec((None, tm, 3 * d), lambda b, i: (b, i, COL_GATE // (3 * d))),
                  pl.BlockSpec((None, tm, d), lambda b, i: (b, i, 0)),
                  _of_layer(l, w, d), _of_layer(l, w, d), _of_layer(l, w, d), _of_layer(l, d, d),
                  _of_layer(l, 1, d), _mod_spec(l, k_gate, d)],
        out_specs=pl.BlockSpec((None, tm, d), lambda b, i: (b, i, 0)),
        out_shape=jax.ShapeDtypeStruct((bsz, s, d), F32),
        compiler_params=_params("parallel", "parallel"),
        name="mixer_merge",
    )(y_lru, y_ret, y_mla, t, x, wl, wr, wm, wo, g_post, mod)


def _ffn_kernel(x_ref, gpre_ref, sh_ref, sc_ref, wu_ref, cw_ref, cb_ref, wd_ref, gp_ref, gf_ref, o_ref,
                act_ref, halo_ref, *, tm, cw):
    @pl.when(pl.program_id(1) == 0)
    def _():
        halo_ref[...] = jnp.zeros(halo_ref.shape, F32)

    x = x_ref[...]
    h = (_rms(x, gpre_ref[...]) * (1.0 + sc_ref[...]) + sh_ref[...]).astype(BF16)
    row = lax.broadcasted_iota(jnp.int32, (SUBLANES, cw), 0)

    def conv(c0, gain):
        cols = slice(c0, c0 + cw)
        taps = cw_ref[:, cols] * gain
        bias = cb_ref[:, cols] * gain
        xv = jnp.dot(h, wu_ref[:, cols], preferred_element_type=F32)
        h1 = halo_ref[SUBLANES - 1:SUBLANES, cols]
        h2 = halo_ref[SUBLANES - 2:SUBLANES - 1, cols]
        halo_ref[:, cols] = xv[tm - SUBLANES:, :]
        r1 = pltpu.roll(xv, 1, 0)
        r2 = pltpu.roll(xv, 2, 0)
        xm1 = jnp.concatenate([jnp.where(row == 0, h1, r1[:SUBLANES]), r1[SUBLANES:]], axis=0)
        top2 = jnp.where(row == 0, h2, jnp.where(row == 1, h1, r2[:SUBLANES]))
        xm2 = jnp.concatenate([top2, r2[SUBLANES:]], axis=0)
        y = xv * taps[2:3] + bias
        y = y + xm2 * taps[0:1]
        return y + xm1 * taps[1:2]

    for c in range(D_FF // cw):
        hu = conv(c * cw, 0.5)
        g = conv(D_FF + c * cw, 1.0)
        th = jnp.tanh(g * (GELU_C1 + GELU_C2 * (g * g)))
        act_ref[:, c * cw:(c + 1) * cw] = ((hu * g) * (1.0 + th)).astype(BF16)
    y = jnp.dot(act_ref[...], wd_ref[...], preferred_element_type=F32)
    o_ref[...] = x + gf_ref[...] * _rms(y, gp_ref[...])


def _ffn(l, x, g_pre, mod, k_shift, k_scale, k_gate, wu, conv_w, conv_b, wd, g_post):
    bsz, s, d = x.shape
    tm = 512
    n = wu.shape[-1]
    single = dict(pipeline_mode=pl.Buffered(1))
    return pl.pallas_call(
        functools.partial(_ffn_kernel, tm=tm, cw=256),
        grid=(bsz, s // tm),
        in_specs=[pl.BlockSpec((None, tm, d), lambda b, i: (b, i, 0)),
                  _of_layer(l, 1, d), _mod_spec(l, k_shift, d), _mod_spec(l, k_scale, d),
                  _of_layer(l, d, n, **single), _of_layer(l, FFN_CONV, n), _of_layer(l, 1, n),
                  _of_layer(l, D_FF, d, **single), _of_layer(l, 1, d), _mod_spec(l, k_gate, d)],
        out_specs=pl.BlockSpec((None, tm, d), lambda b, i: (b, i, 0)),
        out_shape=jax.ShapeDtypeStruct((bsz, s, d), F32),
        scratch_shapes=[pltpu.VMEM((tm, D_FF), BF16), pltpu.VMEM((SUBLANES, n), F32)],
        compiler_params=_params("parallel", "arbitrary"),
        name="ffn_fused",
    )(x, g_pre, mod, mod, wu, conv_w, conv_b, wd, g_post, mod)


def _pack_w_in_kernel(w_ref, o_ref):
    o_cq = 2 * LRU_WIDTH + 4 * RET_HEADS * RET_DK
    o_ckv = o_cq + MLA_Q_RANK
    o_kr = o_ckv + MLA_KV_RANK
    o_gate = o_kr + MLA_ROPE
    n_all = w_ref.shape[1]
    n_gate = n_all - o_gate
    rows = w_ref.shape[0]
    o_ref[:, 0:o_cq] = w_ref[:, 0:o_cq]
    o_ref[:, COL_MLA:COL_MLA + MLA_KV_RANK] = w_ref[:, o_ckv:o_kr]
    o_ref[:, N_IN_PACKED - MLA_Q_RANK:N_IN_PACKED] = w_ref[:, o_cq:o_ckv]
    tail = w_ref[:, o_kr:n_all].astype(F32)
    o_ref[:, COL_GATE:COL_GATE + n_gate] = tail[:, MLA_ROPE:].astype(BF16)
    kr = jnp.concatenate([jnp.zeros((rows, KR_LANE), F32), tail[:, :MLA_ROPE],
                          jnp.zeros((rows, LANES - KR_LANE - MLA_ROPE), F32)], axis=1)
    o_ref[:, COL_MLA + MLA_KV_RANK:COL_MLA + MLA_KV_RANK + LANES] = kr.astype(BF16)


def _pack_w_in(w):
    depth, d, n = w.shape
    tr = 256
    return pl.pallas_call(
        _pack_w_in_kernel,
        grid=(depth, d // tr),
        in_specs=[pl.BlockSpec((None, tr, n), lambda l, i: (l, i, 0))],
        out_specs=pl.BlockSpec((None, tr, N_IN_PACKED), lambda l, i: (l, i, 0)),
        out_shape=jax.ShapeDtypeStruct((depth, d, N_IN_PACKED), BF16),
        compiler_params=_params("parallel", "parallel"),
        name="pack_w_in",
    )(w)


def _pad_heads(w, width):
    depth, r, _ = w.shape
    w4 = w.reshape(depth, r, MLA_HEADS, width)
    return jnp.pad(w4, ((0, 0), (0, 0), (0, 0), (0, LANES - width))).reshape(depth, r, MLA_HEADS * LANES)


def _pack_w_ukv(w):
    depth, r, _ = w.shape
    w4 = w.reshape(depth, r, MLA_HEADS, MLA_NOPE + MLA_V)
    k = _pad_heads(w4[..., :MLA_NOPE].reshape(depth, r, -1), MLA_NOPE)
    v = _pad_heads(w4[..., MLA_NOPE:].reshape(depth, r, -1), MLA_V)
    return k.astype(BF16), jnp.swapaxes(v, 1, 2).astype(BF16)


def _block_diag(w):
    depth, nb, n, _ = w.shape
    eye = jnp.eye(nb, dtype=w.dtype)
    return (eye[None, :, None, :, None] * w[:, :, :, None, :]).reshape(depth, nb * n, nb * n)


def kernel(x, c, positions, ada_w, ada_b, mix_pre_g, mix_post_g, w_in, lru_conv_w, lru_conv_b, lru_wa, lru_ba, lru_wx, lru_bx, lru_lambda, lru_wo, ret_wo, mla_q_norm_g, mla_w_uq, mla_kv_norm_g, mla_w_ukv, mla_wo, w_out, ffn_pre_g, ffn_post_g, ffn_w_up, ffn_conv_w, ffn_conv_b, ffn_w_down):
    bsz, s, d = x.shape
    depth = w_in.shape[0]
    row = lambda a: a.reshape(depth, 1, a.shape[-1])
    c_pad = jnp.pad(c, ((0, SUBLANES - bsz), (0, 0)))
    mod = _ada(c_pad, ada_w, ada_b).reshape(depth, SUBLANES, 6, 1, d)
    cos_r, sin_r, cos_m, sin_m, cos_mt, sin_mt = _rope_tables(positions)

    w_in_p = _pack_w_in(w_in.astype(BF16))
    wbd = jnp.concatenate([_block_diag(lru_wa), _block_diag(lru_wx)], axis=-1).astype(BF16)
    bb = jnp.concatenate([lru_ba, lru_bx], axis=-1).reshape(depth, 1, 2 * LRU_WIDTH)
    w_uqt = jnp.swapaxes(_pad_heads(mla_w_uq, MLA_NOPE + MLA_ROPE), 1, 2).astype(BF16)
    w_uk, w_uvt = _pack_w_ukv(mla_w_ukv)
    wl, wr, wm, wo = (a.astype(BF16) for a in (lru_wo, ret_wo, mla_wo, w_out))
    wu, wd = ffn_w_up.astype(BF16), ffn_w_down.astype(BF16)

    for l in range(depth):
        t = _prenorm_matmul(l, x, row(mix_pre_g), mod, 0, 1, w_in_p, 2304, "mixer_in_proj")
        y_lru = _lru(l, t, lru_conv_w, row(lru_conv_b), wbd, bb, row(lru_lambda))
        y_ret = _ret(t, cos_r, sin_r)
        qt, k, vt = _mla_proj(l, t, row(mla_q_norm_g), row(mla_kv_norm_g), w_uqt, w_uk, w_uvt,
                              cos_m, sin_m, cos_mt, sin_mt)
        y_mla = _flash(qt, k, vt)
        x = _merge(l, y_lru, y_ret, y_mla, t, x, wl, wr, wm, wo, row(mix_post_g), mod, 2)
        x = _ffn(l, x, row(ffn_pre_g), mod, 3, 4, 5, wu, ffn_conv_w, row(ffn_conv_b), wd, row(ffn_post_g))
    return x
```

```python
import functools

import jax
import jax.numpy as jnp
import numpy as np
from jax import lax
from jax.experimental import pallas as pl
from jax.experimental.pallas import tpu as pltpu

F32 = jnp.float32
BF16 = jnp.bfloat16

D_MODEL = 1024
DEPTH = 2
EPS = 1e-6
ROPE_THETA = 10000.0
LRU_WIDTH = 512
LRU_BLOCKS = 8
LRU_BLOCK = LRU_WIDTH // LRU_BLOCKS
LRU_CONV = 4
LRU_C = 8.0
RET_HEADS = 8
RET_DK = 64
RET_DV = 64
RET_CHUNK = 128
MLA_HEADS = 8
MLA_Q_RANK = 384
MLA_KV_RANK = 256
MLA_NOPE = 64
MLA_ROPE = 32
MLA_V = 64
D_FF = 2816
FFN_CONV = 3

LANES = 128
SUBLANES = 8
VMEM_LIMIT = 56 * 1024 * 1024

COL_LRU = 0
COL_RET = 1024
COL_GATE = 3072
COL_MLA = 6144
N_IN_PACKED = 6912
MLA_PACK = 768
KR_LANE = 64
LOG2_E = 1.4426950408889634
FLASH_HEADS = 4
V_ROWS = MLA_V + 16


def _params(*sem):
    return pltpu.CompilerParams(dimension_semantics=sem, vmem_limit_bytes=VMEM_LIMIT)


def _of_layer(l, *tail, **kw):
    return pl.BlockSpec((None,) + tail, lambda *_: (l,) + (0,) * len(tail), **kw)


def _mod_spec(l, k, d):
    return pl.BlockSpec((None, None, None, 1, d), lambda b, *_: (l, b, k, 0, 0))


GELU_C1 = 0.7978845608028654
GELU_C2 = GELU_C1 * 0.044715


def _gelu_tanh(x):
    return 0.5 * x * (1.0 + jnp.tanh(x * (GELU_C1 + GELU_C2 * (x * x))))


def _sigmoid(x):
    return 1.0 / (1.0 + jnp.exp(-x))


def _rms(x, g):
    return x * lax.rsqrt(jnp.mean(x * x, axis=-1, keepdims=True) + EPS) * g


def _ada_kernel(ct_ref, w_ref, b_ref, o_ref, *, n_batch):
    ct = ct_ref[...]
    cat = ct * _sigmoid(ct)
    w = w_ref[...]
    o_ref[...] = jnp.zeros(o_ref.shape, F32)
    for r in range(n_batch):
        o_ref[r:r + 1, :] = jnp.sum(w * cat[:, r:r + 1], axis=0, keepdims=True) + b_ref[...]


def _ada(c, ada_w, ada_b):
    depth, d, n = ada_w.shape
    bsz = c.shape[0]
    rows = SUBLANES
    tn = 3072
    return pl.pallas_call(
        functools.partial(_ada_kernel, n_batch=bsz),
        grid=(depth, n // tn),
        in_specs=[
            pl.BlockSpec((d, bsz), lambda l, j: (0, 0)),
            pl.BlockSpec((None, d, tn), lambda l, j: (l, 0, j)),
            pl.BlockSpec((None, 1, tn), lambda l, j: (l, 0, j)),
        ],
        out_specs=pl.BlockSpec((None, rows, tn), lambda l, j: (l, 0, j)),
        out_shape=jax.ShapeDtypeStruct((depth, rows, n), F32),
        compiler_params=_params("parallel", "parallel"),
        name="ada_mod",
    )(c.T, ada_w, ada_b.reshape(depth, 1, n))


def _rope_kernel(pos_ref, inv_ref, cr_ref, sr_ref, cm_ref, sm_ref, cmt_ref, smt_ref):
    half_r = RET_DK // 2
    half_m = MLA_ROPE // 2
    half_l = LANES // 2
    rows = pos_ref.shape[0] // 2
    pos = pos_ref[...].astype(F32)
    lane = lax.broadcasted_iota(jnp.int32, (rows, LANES), 1)
    ang = jnp.where(lane < half_l, pos[:rows], pos[rows:]) * inv_ref[...]
    c2 = jnp.cos(ang)
    s2 = jnp.sin(ang)

    def unpack(x, lo, n):
        return jnp.concatenate([x[:, lo:lo + n], x[:, half_l + lo:half_l + lo + n]], axis=0)

    c_r, s_r = unpack(c2, 0, half_r), unpack(s2, 0, half_r)
    c_m, s_m = unpack(c2, half_r, half_m), unpack(s2, half_r, half_m)
    reps = LANES // RET_DK
    cr_ref[...] = jnp.concatenate([c_r, c_r] * reps, axis=1)
    sr_ref[...] = jnp.concatenate([-s_r, s_r] * reps, axis=1)
    ts = 2 * rows
    tail = LANES - MLA_NOPE - MLA_ROPE
    cm_ref[...] = jnp.concatenate([jnp.ones((ts, MLA_NOPE), F32), c_m, c_m, jnp.ones((ts, tail), F32)], axis=1)
    sm_ref[...] = jnp.concatenate([jnp.zeros((ts, MLA_NOPE), F32), -s_m, s_m, jnp.zeros((ts, tail), F32)], axis=1)
    c2t = c2.T
    s2t = s2.T
    cmt_ref[...] = jnp.concatenate([c2t[half_r:half_r + half_m], c2t[half_l + half_r:half_l + half_r + half_m]], axis=1)
    smt_ref[...] = jnp.concatenate([s2t[half_r:half_r + half_m], s2t[half_l + half_r:half_l + half_r + half_m]], axis=1)


def _rope_tables(positions):
    bsz, s = positions.shape
    ts = 1024
    inv_r = ROPE_THETA ** (-jnp.arange(0, RET_DK, 2, dtype=F32) / RET_DK)
    inv_m = ROPE_THETA ** (-jnp.arange(0, MLA_ROPE, 2, dtype=F32) / MLA_ROPE)
    inv_half = jnp.concatenate([inv_r, inv_m, jnp.zeros(LANES // 2 - inv_r.size - inv_m.size, F32)])
    inv = jnp.tile(inv_half, 2).reshape(1, LANES)
    tab = pl.BlockSpec((None, ts, LANES), lambda b, i: (b, i, 0))
    shp = jax.ShapeDtypeStruct((bsz, s, LANES), F32)
    tab_t = pl.BlockSpec((None, inv_m.size, ts), lambda b, i: (b, 0, i))
    shp_t = jax.ShapeDtypeStruct((bsz, inv_m.size, s), F32)
    return pl.pallas_call(
        _rope_kernel,
        grid=(bsz, s // ts),
        in_specs=[pl.BlockSpec((None, ts, 1), lambda b, i: (b, i, 0)), pl.BlockSpec((1, LANES), lambda b, i: (0, 0))],
        out_specs=[tab, tab, tab, tab, tab_t, tab_t],
        out_shape=[shp, shp, shp, shp, shp_t, shp_t],
        compiler_params=_params("parallel", "parallel"),
        name="rope_tables",
    )(positions.reshape(bsz, s, 1), inv)


def _prenorm_matmul_kernel(x_ref, g_ref, sh_ref, sc_ref, w_ref, o_ref, *, tn):
    h = (_rms(x_ref[...], g_ref[...]) * (1.0 + sc_ref[...]) + sh_ref[...]).astype(BF16)
    for j in range(w_ref.shape[1] // tn):
        cols = slice(j * tn, (j + 1) * tn)
        o_ref[:, cols] = jnp.dot(h, w_ref[:, cols], preferred_element_type=F32).astype(o_ref.dtype)


def _prenorm_matmul(l, x, g, mod, k_shift, k_scale, w, tn, name):
    bsz, s, d = x.shape
    n = w.shape[-1]
    tm = 512
    return pl.pallas_call(
        functools.partial(_prenorm_matmul_kernel, tn=tn),
        grid=(bsz, s // tm),
        in_specs=[
            pl.BlockSpec((None, tm, d), lambda b, i: (b, i, 0)),
            _of_layer(l, 1, d),
            _mod_spec(l, k_shift, d), _mod_spec(l, k_scale, d),
            _of_layer(l, d, n, pipeline_mode=pl.Buffered(1)),
        ],
        out_specs=pl.BlockSpec((None, tm, n), lambda b, i: (b, i, 0)),
        out_shape=jax.ShapeDtypeStruct((bsz, s, n), BF16),
        compiler_params=_params("parallel", "parallel"),
        name=name,
    )(x, g, mod, mod, w)


def _lru_kernel(t_ref, cw_ref, cb_ref, wbd_ref, bb_ref, lam_ref, o_ref, xbuf_ref, hc_ref, h_ref, *, ts):
    w = LRU_WIDTH

    @pl.when(pl.program_id(1) == 0)
    def _():
        xbuf_ref[0:SUBLANES, :] = jnp.zeros((SUBLANES, w), F32)
        hc_ref[...] = jnp.zeros((1, w), F32)

    xb = t_ref[:, 0:w].astype(F32)
    gb = t_ref[:, w:2 * w].astype(F32)
    xbuf_ref[SUBLANES:SUBLANES + ts, :] = xb
    xc = xb * cw_ref[LRU_CONV - 1:LRU_CONV, :] + cb_ref[...]
    for k in range(LRU_CONV - 1):
        back = LRU_CONV - 1 - k
        xc = xc + xbuf_ref[pl.ds(SUBLANES - back, ts), :] * cw_ref[k:k + 1, :]
    xbuf_ref[0:SUBLANES, :] = xb[ts - SUBLANES:, :]

    z = jnp.dot(xc.astype(BF16), wbd_ref[...], preferred_element_type=F32) + bb_ref[...]
    r = _sigmoid(z[:, :w])
    ig = _sigmoid(z[:, w:])
    nl = -lam_ref[...]
    softplus = jnp.maximum(nl, 0.0) + jnp.log1p(jnp.exp(-jnp.abs(nl)))
    log_a = (-LRU_C) * r * softplus
    a = jnp.exp(log_a)
    th = jnp.tanh(log_a)
    u = jnp.sqrt(-2.0 * th / (1.0 - th)) * (ig * xc)

    groups = ts // SUBLANES
    a = a.reshape(groups, SUBLANES, w)
    u = u.reshape(groups, SUBLANES, w)
    sub = lax.broadcasted_iota(jnp.int32, (groups, SUBLANES, w), 1)
    k = 1
    while k < SUBLANES:
        keep = sub >= k
        a_prev = jnp.where(keep, pltpu.roll(a, k, 1), 1.0)
        u_prev = jnp.where(keep, pltpu.roll(u, k, 1), 0.0)
        u = a * u_prev + u
        a = a * a_prev
        k *= 2
    h_prev = hc_ref[...]
    for r in range(groups):
        hb = a[r] * h_prev + u[r]
        h_ref[r * SUBLANES:(r + 1) * SUBLANES, :] = hb
        h_prev = hb[SUBLANES - 1:SUBLANES, :]
    hc_ref[...] = h_prev
    o_ref[...] = (h_ref[...] * _gelu_tanh(gb)).astype(o_ref.dtype)


def _lru(l, t, conv_w, conv_b, wbd, bb, lam):
    bsz, s, _ = t.shape
    ts = 256
    w = LRU_WIDTH
    return pl.pallas_call(
        functools.partial(_lru_kernel, ts=ts),
        grid=(bsz, s // ts),
        in_specs=[
            pl.BlockSpec((None, ts, 2 * w), lambda b, i: (b, i, COL_LRU // (2 * w))),
            _of_layer(l, LRU_CONV, w), _of_layer(l, 1, w), _of_layer(l, w, 2 * w), _of_layer(l, 1, 2 * w),
            _of_layer(l, 1, w),
        ],
        out_specs=pl.BlockSpec((None, ts, w), lambda b, i: (b, i, 0)),
        out_shape=jax.ShapeDtypeStruct((bsz, s, w), BF16),
        scratch_shapes=[pltpu.VMEM((ts + SUBLANES, w), F32), pltpu.VMEM((1, w), F32), pltpu.VMEM((ts, w), F32)],
        compiler_params=_params("parallel", "arbitrary"),
        name="lru_mixer",
    )(t, conv_w, conv_b, wbd, bb, lam)


def _ret_kernel(q_ref, k_ref, v_ref, g_ref, cos_ref, sin_ref, dec_ref, qdec_ref, kvdec_ref, cdec_ref,
                bd_ref, o_ref, st_ref, *, n_chunks):
    c_len = RET_CHUNK
    pairs = RET_HEADS // 2

    @pl.when(pl.program_id(1) == 0)
    def _():
        st_ref[...] = jnp.zeros(st_ref.shape, F32)

    lane = lax.broadcasted_iota(jnp.int32, (c_len, LANES), 1)
    head0 = lane < RET_DK
    first_half = (lane % RET_DK) < (RET_DK // 2)
    inv_n = 1.0 / RET_DV

    def rope(x, cos, sin):
        swapped = jnp.where(first_half, pltpu.roll(x, LANES - RET_DK // 2, 1), pltpu.roll(x, RET_DK // 2, 1))
        return x * cos + swapped * sin

    def head_mean(x):
        m0 = jnp.sum(jnp.where(head0, x, 0.0), axis=-1, keepdims=True) * inv_n
        m1 = jnp.sum(jnp.where(head0, 0.0, x), axis=-1, keepdims=True) * inv_n
        return jnp.where(head0, m0, m1)

    nt = (((1,), (1,)), ((), ()))
    tn = (((0,), (0,)), ((), ()))
    for c in range(n_chunks):
        rows = slice(c * c_len, (c + 1) * c_len)
        cos = cos_ref[rows, :]
        sin = sin_ref[rows, :]
        for p in range(pairs):
            cols = slice(p * LANES, (p + 1) * LANES)
            q = rope(q_ref[rows, cols].astype(F32), cos, sin)
            k = rope(k_ref[rows, cols].astype(F32), cos, sin) * (RET_DK ** -0.5)
            v = v_ref[rows, cols]
            qb = q.astype(BF16)
            kb = k.astype(BF16)
            zero = jnp.zeros_like(qb)
            s0 = lax.dot_general(jnp.where(head0, qb, zero), kb, nt, preferred_element_type=F32)
            s1 = lax.dot_general(jnp.where(head0, zero, qb), kb, nt, preferred_element_type=F32)
            probs = jnp.concatenate([s0 * dec_ref[2 * p], s1 * dec_ref[2 * p + 1]], axis=1).astype(BF16)
            v2 = jnp.concatenate([jnp.where(head0, v, zero), jnp.where(head0, zero, v)], axis=0)
            y = jnp.dot(probs, v2, preferred_element_type=F32)
            state = st_ref[p]
            y = y + jnp.dot(qb, state.astype(BF16), preferred_element_type=F32) * qdec_ref[p]
            vd = (v.astype(F32) * kvdec_ref[p]).astype(BF16)
            kv = lax.dot_general(kb, vd, tn, preferred_element_type=F32)
            st_ref[p] = state * cdec_ref[p] + kv * bd_ref[...]

            d = y - head_mean(y)
            yn = d * lax.rsqrt(head_mean(d * d) + EPS)
            g = g_ref[rows, cols].astype(F32)
            o_ref[rows, cols] = (g * _sigmoid(g) * yn).astype(o_ref.dtype)


def _ret_constants():
    f32 = F32
    log_gamma = jnp.log1p(-(2.0 ** (-5.0 - jnp.arange(RET_HEADS, dtype=f32))))
    idx = jnp.arange(RET_CHUNK, dtype=f32)
    diff = idx[:, None] - idx[None, :]
    causal = diff >= 0
    inner = jnp.where(causal[None], jnp.exp(jnp.where(causal, diff, 0.0)[None] * log_gamma[:, None, None]), 0.0)
    kv_decay = jnp.exp((RET_CHUNK - 1.0 - idx)[None, :] * log_gamma[:, None])
    q_decay = jnp.exp((idx + 1.0)[:, None] * log_gamma[None, :])
    chunk_decay = jnp.exp(RET_CHUNK * log_gamma)
    pairs = RET_HEADS // 2

    def by_lane(per_head):
        rows = per_head.shape[0]
        return jnp.repeat(per_head.reshape(rows, pairs, 2), RET_DK, axis=2).reshape(rows, pairs, LANES).transpose(1, 0, 2)

    qdec = by_lane(q_decay)
    kvdec = by_lane(kv_decay.T)
    cdec = by_lane(chunk_decay[None, :])
    lane_head = jnp.arange(LANES) // RET_DK
    bd = (lane_head[:, None] == lane_head[None, :]).astype(f32)
    return inner, qdec, kvdec, cdec, bd


def _ret(t, cos_r, sin_r):
    bsz, s, _ = t.shape
    tc = 512
    w = RET_HEADS * RET_DK
    inner, qdec, kvdec, cdec, bd = _ret_constants()
    base = COL_RET // w
    col = lambda j: pl.BlockSpec((None, tc, w), lambda b, i: (b, i, base + j))
    tab = pl.BlockSpec((None, tc, LANES), lambda b, i: (b, i, 0))
    const = lambda shape: pl.BlockSpec(shape, lambda b, i: (0,) * len(shape))
    return pl.pallas_call(
        functools.partial(_ret_kernel, n_chunks=tc // RET_CHUNK),
        grid=(bsz, s // tc),
        in_specs=[col(0), col(1), col(2), col(3), tab, tab,
                  const(inner.shape), const(qdec.shape), const(kvdec.shape), const(cdec.shape), const(bd.shape)],
        out_specs=pl.BlockSpec((None, tc, w), lambda b, i: (b, i, 0)),
        out_shape=jax.ShapeDtypeStruct((bsz, s, w), BF16),
        scratch_shapes=[pltpu.VMEM((RET_HEADS // 2, LANES, LANES), F32)],
        compiler_params=_params("parallel", "arbitrary"),
        name="ret_mixer",
    )(t, t, t, t, cos_r, sin_r, inner, qdec, kvdec, cdec, bd)


def _mla_proj_kernel(t_ref, gq_ref, gkv_ref, wqt_ref, wk_ref, wvt_ref, cos_ref, sin_ref, cost_ref, sint_ref,
                     qt_ref, k_ref, vt_ref):
    ts = t_ref.shape[0]
    half = MLA_ROPE // 2
    nt = (((1,), (1,)), ((), ()))
    ckv = t_ref[:, 0:MLA_KV_RANK].astype(F32)
    kr = t_ref[:, MLA_KV_RANK:MLA_KV_RANK + LANES].astype(F32)
    cq = t_ref[:, MLA_PACK - MLA_Q_RANK:MLA_PACK].astype(F32)
    cqn = _rms(cq, gq_ref[...]).astype(BF16)
    ckvn = _rms(ckv, gkv_ref[...]).astype(BF16)

    qt = lax.dot_general(wqt_ref[...], cqn, nt, preferred_element_type=F32)
    cos_t = cost_ref[...]
    sin_t = sint_ref[...]
    scale = (MLA_NOPE + MLA_ROPE) ** -0.5 * LOG2_E
    for h in range(MLA_HEADS):
        base = h * LANES
        x1 = qt[base + MLA_NOPE:base + MLA_NOPE + half]
        x2 = qt[base + MLA_NOPE + half:base + MLA_NOPE + MLA_ROPE]
        blk = jnp.concatenate([qt[base:base + MLA_NOPE], x1 * cos_t - x2 * sin_t, x1 * sin_t + x2 * cos_t,
                               qt[base + MLA_NOPE + MLA_ROPE:base + LANES]], axis=0)
        qt_ref[base:base + LANES, :] = (blk * scale).astype(qt_ref.dtype)

    lane = lax.broadcasted_iota(jnp.int32, (ts, LANES), 1)
    swapped = jnp.where(lane < MLA_NOPE + half, pltpu.roll(kr, LANES - half, 1), pltpu.roll(kr, half, 1))
    k_rope = kr * cos_ref[...] + swapped * sin_ref[...]
    kn = jnp.dot(ckvn, wk_ref[...], preferred_element_type=F32)
    for h in range(MLA_HEADS):
        cols = slice(h * LANES, (h + 1) * LANES)
        k_ref[:, cols] = (kn[:, cols] + k_rope).astype(k_ref.dtype)

    vt = lax.dot_general(wvt_ref[...], ckvn, nt, preferred_element_type=F32)
    row = lax.broadcasted_iota(jnp.int32, vt.shape, 0)
    vt_ref[...] = jnp.where(row % LANES == MLA_V, 1.0, vt).astype(vt_ref.dtype)


def _mla_proj(l, t, gq, gkv, wqt, wk, wvt, cos_m, sin_m, cos_mt, sin_mt):
    bsz, s, _ = t.shape
    ts = 512
    hq = MLA_HEADS * LANES
    half = MLA_ROPE // 2
    tab = pl.BlockSpec((None, ts, LANES), lambda b, i: (b, i, 0))
    tab_t = pl.BlockSpec((None, half, ts), lambda b, i: (b, 0, i))
    rowmajor = pl.BlockSpec((None, ts, hq), lambda b, i: (b, i, 0))
    transposed = pl.BlockSpec((None, hq, ts), lambda b, i: (b, 0, i))
    return pl.pallas_call(
        _mla_proj_kernel,
        grid=(bsz, s // ts),
        in_specs=[pl.BlockSpec((None, ts, MLA_PACK), lambda b, i: (b, i, COL_MLA // MLA_PACK)),
                  _of_layer(l, 1, MLA_Q_RANK), _of_layer(l, 1, MLA_KV_RANK), _of_layer(l, *wqt.shape[1:]),
                  _of_layer(l, *wk.shape[1:]), _of_layer(l, *wvt.shape[1:]), tab, tab, tab_t, tab_t],
        out_specs=[transposed, rowmajor, transposed],
        out_shape=[jax.ShapeDtypeStruct((bsz, hq, s), BF16), jax.ShapeDtypeStruct((bsz, s, hq), BF16),
                   jax.ShapeDtypeStruct((bsz, hq, s), BF16)],
        compiler_params=_params("parallel", "parallel"),
        name="mla_proj",
    )(t, gq, gkv, wqt, wk, wvt, cos_m, sin_m, cos_mt, sin_mt)


def _flash_kernel(qt_ref, k_ref, vt_ref, o_ref, acc_ref, sa_ref, sb_ref, *, tq):
    i = pl.program_id(2)
    acc_ref[...] = jnp.zeros(acc_ref.shape, F32)

    def qk(j, dst_ref, masked):
        off = pl.multiple_of(j * tq, tq)
        for h in range(FLASH_HEADS):
            rows = slice(h * LANES, (h + 1) * LANES)
            s = jnp.dot(k_ref[pl.ds(off, tq), rows], qt_ref[rows, :], preferred_element_type=F32)
            if masked:
                key = lax.broadcasted_iota(jnp.int32, (tq, tq), 0)
                qry = lax.broadcasted_iota(jnp.int32, (tq, tq), 1)
                s = jnp.where(key <= qry, s, -1e30)
            dst_ref[h] = s

    def softmax_pv(j, src_ref, ms):
        off = pl.multiple_of(j * tq, tq)
        new_m = []
        for h in range(FLASH_HEADS):
            s = src_ref[h]
            m_new = jnp.maximum(ms[h], jnp.max(s, axis=0, keepdims=True))
            alpha = jnp.exp2(ms[h] - m_new)
            p = jnp.exp2(s - m_new).astype(BF16)
            vt = vt_ref[h * LANES:h * LANES + V_ROWS, pl.ds(off, tq)]
            acc_ref[h] = alpha * acc_ref[h] + jnp.dot(vt, p, preferred_element_type=F32)
            new_m.append(m_new)
        return tuple(new_m)

    n_pairs = i // 2
    qk(i, sa_ref, True)

    def pair(jj, ms):
        qk(2 * jj, sb_ref, False)
        ms = softmax_pv(jnp.where(jj == 0, i, 2 * jj - 1), sa_ref, ms)
        qk(2 * jj + 1, sa_ref, False)
        return softmax_pv(2 * jj, sb_ref, ms)

    m0 = jnp.full((1, tq), -1e30, F32)
    ms = lax.fori_loop(0, n_pairs, pair, (m0,) * FLASH_HEADS)
    in_a = jnp.where(n_pairs == 0, i, 2 * n_pairs - 1)

    @pl.when(i % 2 == 1)
    def _():
        qk(i - 1, sb_ref, False)
        softmax_pv(i - 1, sb_ref, softmax_pv(in_a, sa_ref, ms))

    @pl.when(i % 2 == 0)
    def _():
        softmax_pv(in_a, sa_ref, ms)

    for g in range(FLASH_HEADS // 2):
        outs = []
        for h in (2 * g, 2 * g + 1):
            acc = acc_ref[h]
            outs.append(acc[0:MLA_V, :] / acc[MLA_V:MLA_V + 1, :])
        o_ref[:, g * LANES:(g + 1) * LANES] = jnp.concatenate(outs, axis=0).T.astype(o_ref.dtype)


def _flash(qt, k, vt):
    bsz, s, _ = k.shape
    tq = 512
    hs = FLASH_HEADS
    return pl.pallas_call(
        functools.partial(_flash_kernel, tq=tq),
        grid=(bsz, MLA_HEADS // hs, s // tq),
        in_specs=[
            pl.BlockSpec((None, hs * LANES, tq), lambda b, p, i: (b, p, i)),
            pl.BlockSpec((None, s, hs * LANES), lambda b, p, i: (b, 0, p)),
            pl.BlockSpec((None, hs * LANES, s), lambda b, p, i: (b, p, 0)),
        ],
        out_specs=pl.BlockSpec((None, tq, hs * MLA_V), lambda b, p, i: (b, i, p)),
        out_shape=jax.ShapeDtypeStruct((bsz, s, MLA_HEADS * MLA_V), BF16),
        scratch_shapes=[pltpu.VMEM((hs, V_ROWS, tq), F32), pltpu.VMEM((hs, tq, tq), F32),
                        pltpu.VMEM((hs, tq, tq), F32)],
        compiler_params=_params("parallel", "parallel", "arbitrary"),
        name="mla_flash",
    )(qt, k, vt)


def _merge_kernel(yl_ref, yr_ref, ym_ref, gt_ref, x_ref, wl_ref, wr_ref, wm_ref, wo_ref, gp_ref, gm_ref, o_ref):
    d = D_MODEL
    merged = _sigmoid(gt_ref[:, 0:d].astype(F32)) * jnp.dot(yl_ref[...], wl_ref[...], preferred_element_type=F32)
    merged = merged + _sigmoid(gt_ref[:, d:2 * d].astype(F32)) * jnp.dot(
        yr_ref[...], wr_ref[...], preferred_element_type=F32)
    merged = merged + _sigmoid(gt_ref[:, 2 * d:3 * d].astype(F32)) * jnp.dot(
        ym_ref[...], wm_ref[...], preferred_element_type=F32)
    y = jnp.dot(merged.astype(BF16), wo_ref[...], preferred_element_type=F32)
    o_ref[...] = x_ref[...] + gm_ref[...] * _rms(y, gp_ref[...])


def _merge(l, y_lru, y_ret, y_mla, t, x, wl, wr, wm, wo, g_post, mod, k_gate):
    bsz, s, d = x.shape
    tm = 512
    w = y_lru.shape[-1]
    br = pl.BlockSpec((None, tm, w), lambda b, i: (b, i, 0))
    return pl.pallas_call(
        _merge_kernel,
        grid=(bsz, s // tm),
        in_specs=[br, br, br,
                  pl.BlockSpec((None, tm, 3 * d), lambda b, i: (b, i, COL_GATE // (3 * d))),
                  pl.BlockSpec((None, tm, d), lambda b, i: (b, i, 0)),
                  _of_layer(l, w, d), _of_layer(l, w, d), _of_layer(l, w, d), _of_layer(l, d, d),
                  _of_layer(l, 1, d), _mod_spec(l, k_gate, d)],
        out_specs=pl.BlockSpec((None, tm, d), lambda b, i: (b, i, 0)),
        out_shape=jax.ShapeDtypeStruct((bsz, s, d), F32),
        compiler_params=_params("parallel", "parallel"),
        name="mixer_merge",
    )(y_lru, y_ret, y_mla, t, x, wl, wr, wm, wo, g_post, mod)


def _ffn_kernel(x_ref, gpre_ref, sh_ref, sc_ref, wu_ref, cw_ref, cb_ref, wd_ref, gp_ref, gf_ref, o_ref,
                act_ref, halo_ref, *, tm, cw):
    @pl.when(pl.program_id(1) == 0)
    def _():
        halo_ref[...] = jnp.zeros(halo_ref.shape, F32)

    row = lax.broadcasted_iota(jnp.int32, (SUBLANES, cw), 0)
    n_sub = act_ref.shape[0]
    sub = tm // n_sub

    def conv(h, c0, gain):
        cols = slice(c0, c0 + cw)
        taps = cw_ref[:, cols] * gain
        bias = cb_ref[:, cols] * gain
        xv = jnp.dot(h, wu_ref[:, cols], preferred_element_type=F32)
        h1 = halo_ref[SUBLANES - 1:SUBLANES, cols]
        h2 = halo_ref[SUBLANES - 2:SUBLANES - 1, cols]
        halo_ref[:, cols] = xv[sub - SUBLANES:, :]
        r1 = pltpu.roll(xv, 1, 0)
        r2 = pltpu.roll(xv, 2, 0)
        xm1 = jnp.concatenate([jnp.where(row == 0, h1, r1[:SUBLANES]), r1[SUBLANES:]], axis=0)
        top2 = jnp.where(row == 0, h2, jnp.where(row == 1, h1, r2[:SUBLANES]))
        xm2 = jnp.concatenate([top2, r2[SUBLANES:]], axis=0)
        y = xv * taps[2:3] + bias
        y = y + xm2 * taps[0:1]
        return y + xm1 * taps[1:2]

    for r in range(n_sub):
        rows = slice(r * sub, (r + 1) * sub)
        h = (_rms(x_ref[rows, :], gpre_ref[...]) * (1.0 + sc_ref[...]) + sh_ref[...]).astype(BF16)
        for c in range(D_FF // cw):
            hu = conv(h, c * cw, 0.5)
            g = conv(h, D_FF + c * cw, 1.0)
            th = jnp.tanh(g * (GELU_C1 + GELU_C2 * (g * g)))
            act_ref[r, :, c * cw:(c + 1) * cw] = ((hu * g) * (1.0 + th)).astype(BF16)
    for r in range(n_sub):
        rows = slice(r * sub, (r + 1) * sub)
        y = jnp.dot(act_ref[r], wd_ref[...], preferred_element_type=F32)
        o_ref[rows, :] = x_ref[rows, :] + gf_ref[...] * _rms(y, gp_ref[...])


def _ffn(l, x, g_pre, mod, k_shift, k_scale, k_gate, wu, conv_w, conv_b, wd, g_post):
    bsz, s, d = x.shape
    tm = 512
    n_sub = 1
    n = wu.shape[-1]
    single = dict(pipeline_mode=pl.Buffered(1))
    return pl.pallas_call(
        functools.partial(_ffn_kernel, tm=tm, cw=256),
        grid=(bsz, s // tm),
        in_specs=[pl.BlockSpec((None, tm, d), lambda b, i: (b, i, 0)),
                  _of_layer(l, 1, d), _mod_spec(l, k_shift, d), _mod_spec(l, k_scale, d),
                  _of_layer(l, d, n, **single), _of_layer(l, FFN_CONV, n), _of_layer(l, 1, n),
                  _of_layer(l, D_FF, d, **single), _of_layer(l, 1, d), _mod_spec(l, k_gate, d)],
        out_specs=pl.BlockSpec((None, tm, d), lambda b, i: (b, i, 0)),
        out_shape=jax.ShapeDtypeStruct((bsz, s, d), F32),
        scratch_shapes=[pltpu.VMEM((n_sub, tm // n_sub, D_FF), BF16), pltpu.VMEM((SUBLANES, n), F32)],
        compiler_params=_params("parallel", "arbitrary"),
        name="ffn_fused",
    )(x, g_pre, mod, mod, wu, conv_w, conv_b, wd, g_post, mod)


def _pack_w_in_kernel(w_ref, o_ref, *, n_all):
    o_cq = 2 * LRU_WIDTH + 4 * RET_HEADS * RET_DK
    o_ckv = o_cq + MLA_Q_RANK
    o_kr = o_ckv + MLA_KV_RANK
    o_gate = o_kr + MLA_ROPE
    n_gate = n_all - o_gate
    rows = w_ref.shape[0]
    o_ref[:, 0:o_cq] = w_ref[:, 0:o_cq]
    o_ref[:, COL_MLA:COL_MLA + MLA_KV_RANK] = w_ref[:, o_ckv:o_kr]
    o_ref[:, N_IN_PACKED - MLA_Q_RANK:N_IN_PACKED] = w_ref[:, o_cq:o_ckv]
    tail = w_ref[:, o_kr:w_ref.shape[1]].astype(F32)
    o_ref[:, COL_GATE:COL_GATE + n_gate] = tail[:, MLA_ROPE:MLA_ROPE + n_gate].astype(BF16)
    kr = jnp.concatenate([jnp.zeros((rows, KR_LANE), F32), tail[:, :MLA_ROPE],
                          jnp.zeros((rows, LANES - KR_LANE - MLA_ROPE), F32)], axis=1)
    o_ref[:, COL_MLA + MLA_KV_RANK:COL_MLA + MLA_KV_RANK + LANES] = kr.astype(BF16)


def _pack_w_in(w, n_all):
    depth, d, n = w.shape
    tr = 256
    return pl.pallas_call(
        functools.partial(_pack_w_in_kernel, n_all=n_all),
        grid=(depth, d // tr),
        in_specs=[pl.BlockSpec((None, tr, n), lambda l, i: (l, i, 0))],
        out_specs=pl.BlockSpec((None, tr, N_IN_PACKED), lambda l, i: (l, i, 0)),
        out_shape=jax.ShapeDtypeStruct((depth, d, N_IN_PACKED), BF16),
        compiler_params=_params("parallel", "parallel"),
        name="pack_w_in",
    )(w)


def _pad_heads(w, width):
    depth, r, _ = w.shape
    w4 = w.reshape(depth, r, MLA_HEADS, width)
    return jnp.pad(w4, ((0, 0), (0, 0), (0, 0), (0, LANES - width))).reshape(depth, r, MLA_HEADS * LANES)


def _pack_w_ukv(w):
    depth, r, _ = w.shape
    w4 = w.reshape(depth, r, MLA_HEADS, MLA_NOPE + MLA_V)
    k = _pad_heads(w4[..., :MLA_NOPE].reshape(depth, r, -1), MLA_NOPE)
    v = _pad_heads(w4[..., MLA_NOPE:].reshape(depth, r, -1), MLA_V)
    return k.astype(BF16), jnp.swapaxes(v, 1, 2).astype(BF16)


def _block_diag(w):
    depth, nb, n, _ = w.shape
    eye = jnp.eye(nb, dtype=w.dtype)
    return (eye[None, :, None, :, None] * w[:, :, :, None, :]).reshape(depth, nb * n, nb * n)


def kernel(x, c, positions, ada_w, ada_b, mix_pre_g, mix_post_g, w_in, lru_conv_w, lru_conv_b, lru_wa, lru_ba, lru_wx, lru_bx, lru_lambda, lru_wo, ret_wo, mla_q_norm_g, mla_w_uq, mla_kv_norm_g, mla_w_ukv, mla_wo, w_out, ffn_pre_g, ffn_post_g, ffn_w_up, ffn_conv_w, ffn_conv_b, ffn_w_down):
    bsz, s, d = x.shape
    depth = w_in.shape[0]
    row = lambda a: a.reshape(depth, 1, a.shape[-1])
    mod = _ada(c, ada_w, ada_b).reshape(depth, SUBLANES, 6, 1, d)
    cos_r, sin_r, cos_m, sin_m, cos_mt, sin_mt = _rope_tables(positions)

    w_in_p = _pack_w_in(jnp.pad(w_in, ((0, 0), (0, 0), (0, N_IN_PACKED - w_in.shape[-1]))).astype(BF16), w_in.shape[-1])
    wbd = jnp.concatenate([_block_diag(lru_wa), _block_diag(lru_wx)], axis=-1).astype(BF16)
    bb = jnp.concatenate([lru_ba, lru_bx], axis=-1).reshape(depth, 1, 2 * LRU_WIDTH)
    w_uqt = jnp.swapaxes(_pad_heads(mla_w_uq, MLA_NOPE + MLA_ROPE), 1, 2).astype(BF16)
    w_uk, w_uvt = _pack_w_ukv(mla_w_ukv)
    wl, wr, wm, wo = (a.astype(BF16) for a in (lru_wo, ret_wo, mla_wo, w_out))
    wu, wd = ffn_w_up.astype(BF16), ffn_w_down.astype(BF16)

    for l in range(depth):
        t = _prenorm_matmul(l, x, row(mix_pre_g), mod, 0, 1, w_in_p, 2304, "mixer_in_proj")
        y_lru = _lru(l, t, lru_conv_w, row(lru_conv_b), wbd, bb, row(lru_lambda))
        y_ret = _ret(t, cos_r, sin_r)
        qt, k, vt = _mla_proj(l, t, row(mla_q_norm_g), row(mla_kv_norm_g), w_uqt, w_uk, w_uvt,
                              cos_m, sin_m, cos_mt, sin_mt)
        y_mla = _flash(qt, k, vt)
        x = _merge(l, y_lru, y_ret, y_mla, t, x, wl, wr, wm, wo, row(mix_post_g), mod, 2)
        x = _ffn(l, x, row(ffn_pre_g), mod, 3, 4, 5, wu, ffn_conv_w, row(ffn_conv_b), wd, row(ffn_post_g))
    return x
```

```python
import functools

import jax
import jax.numpy as jnp
import numpy as np
from jax import lax
from jax.experimental import pallas as pl
from jax.experimental.pallas import tpu as pltpu

F32 = jnp.float32
BF16 = jnp.bfloat16

D_MODEL = 1024
DEPTH = 2
EPS = 1e-6
ROPE_THETA = 10000.0
LRU_WIDTH = 512
LRU_BLOCKS = 8
LRU_BLOCK = LRU_WIDTH // LRU_BLOCKS
LRU_CONV = 4
LRU_C = 8.0
RET_HEADS = 8
RET_DK = 64
RET_DV = 64
RET_CHUNK = 128
MLA_HEADS = 8
MLA_Q_RANK = 384
MLA_KV_RANK = 256
MLA_NOPE = 64
MLA_ROPE = 32
MLA_V = 64
D_FF = 2816
FFN_CONV = 3

LANES = 128
SUBLANES = 8
VMEM_LIMIT = 56 * 1024 * 1024

COL_LRU = 0
COL_RET = 1024
COL_GATE = 3072
COL_MLA = 6144
N_IN_PACKED = 6912
MLA_PACK = 768
KR_LANE = 64
LOG2_E = 1.4426950408889634
FLASH_HEADS = 4
V_ROWS = MLA_V + 16


def _params(*sem):
    return pltpu.CompilerParams(dimension_semantics=sem, vmem_limit_bytes=VMEM_LIMIT)


def _of_layer(l, *tail, **kw):
    return pl.BlockSpec((None,) + tail, lambda *_: (l,) + (0,) * len(tail), **kw)


def _mod_spec(l, k, d):
    return pl.BlockSpec((None, None, None, 1, d), lambda b, *_: (l, b, k, 0, 0))


GELU_C1 = 0.7978845608028654
GELU_C2 = GELU_C1 * 0.044715


def _gelu_tanh(x):
    return 0.5 * x * (1.0 + jnp.tanh(x * (GELU_C1 + GELU_C2 * (x * x))))


def _sigmoid(x):
    return 1.0 / (1.0 + jnp.exp(-x))


def _rms(x, g):
    return x * lax.rsqrt(jnp.mean(x * x, axis=-1, keepdims=True) + EPS) * g


def _ada_kernel(c_ref, w_ref, b_ref, o_ref):
    c = c_ref[...]
    ca = c * _sigmoid(c)
    o_ref[...] = jnp.dot(ca, w_ref[...], preferred_element_type=F32,
                         precision=lax.Precision.HIGHEST) + b_ref[...]


def _ada(c, ada_w, ada_b):
    depth, d, n = ada_w.shape
    rows = SUBLANES
    c_pad = jnp.pad(c, ((0, rows - c.shape[0]), (0, 0)))
    tn = 1536
    return pl.pallas_call(
        _ada_kernel,
        grid=(depth, n // tn),
        in_specs=[
            pl.BlockSpec((rows, d), lambda l, j: (0, 0)),
            pl.BlockSpec((None, d, tn), lambda l, j: (l, 0, j)),
            pl.BlockSpec((None, 1, tn), lambda l, j: (l, 0, j)),
        ],
        out_specs=pl.BlockSpec((None, rows, tn), lambda l, j: (l, 0, j)),
        out_shape=jax.ShapeDtypeStruct((depth, rows, n), F32),
        compiler_params=_params("parallel", "parallel"),
        name="ada_mod",
    )(c_pad, ada_w, ada_b.reshape(depth, 1, n))


def _rope_kernel(pos_ref, inv_ref, cr_ref, sr_ref, cm_ref, sm_ref, cmt_ref, smt_ref):
    half_r = RET_DK // 2
    half_m = MLA_ROPE // 2
    half_l = LANES // 2
    rows = pos_ref.shape[0] // 2
    pos = pos_ref[...].astype(F32)
    lane = lax.broadcasted_iota(jnp.int32, (rows, LANES), 1)
    ang = jnp.where(lane < half_l, pos[:rows], pos[rows:]) * inv_ref[...]
    c2 = jnp.cos(ang)
    s2 = jnp.sin(ang)

    def unpack(x, lo, n):
        return jnp.concatenate([x[:, lo:lo + n], x[:, half_l + lo:half_l + lo + n]], axis=0)

    c_r, s_r = unpack(c2, 0, half_r), unpack(s2, 0, half_r)
    c_m, s_m = unpack(c2, half_r, half_m), unpack(s2, half_r, half_m)
    reps = LANES // RET_DK
    cr_ref[...] = jnp.concatenate([c_r, c_r] * reps, axis=1)
    sr_ref[...] = jnp.concatenate([-s_r, s_r] * reps, axis=1)
    ts = 2 * rows
    tail = LANES - MLA_NOPE - MLA_ROPE
    cm_ref[...] = jnp.concatenate([jnp.ones((ts, MLA_NOPE), F32), c_m, c_m, jnp.ones((ts, tail), F32)], axis=1)
    sm_ref[...] = jnp.concatenate([jnp.zeros((ts, MLA_NOPE), F32), -s_m, s_m, jnp.zeros((ts, tail), F32)], axis=1)
    c2t = c2.T
    s2t = s2.T
    cmt_ref[...] = jnp.concatenate([c2t[half_r:half_r + half_m], c2t[half_l + half_r:half_l + half_r + half_m]], axis=1)
    smt_ref[...] = jnp.concatenate([s2t[half_r:half_r + half_m], s2t[half_l + half_r:half_l + half_r + half_m]], axis=1)


def _rope_tables(positions):
    bsz, s = positions.shape
    ts = 1024
    inv_r = ROPE_THETA ** (-jnp.arange(0, RET_DK, 2, dtype=F32) / RET_DK)
    inv_m = ROPE_THETA ** (-jnp.arange(0, MLA_ROPE, 2, dtype=F32) / MLA_ROPE)
    inv_half = jnp.concatenate([inv_r, inv_m, jnp.zeros(LANES // 2 - inv_r.size - inv_m.size, F32)])
    inv = jnp.tile(inv_half, 2).reshape(1, LANES)
    tab = pl.BlockSpec((None, ts, LANES), lambda b, i: (b, i, 0))
    shp = jax.ShapeDtypeStruct((bsz, s, LANES), F32)
    tab_t = pl.BlockSpec((None, inv_m.size, ts), lambda b, i: (b, 0, i))
    shp_t = jax.ShapeDtypeStruct((bsz, inv_m.size, s), F32)
    return pl.pallas_call(
        _rope_kernel,
        grid=(bsz, s // ts),
        in_specs=[pl.BlockSpec((None, ts, 1), lambda b, i: (b, i, 0)), pl.BlockSpec((1, LANES), lambda b, i: (0, 0))],
        out_specs=[tab, tab, tab, tab, tab_t, tab_t],
        out_shape=[shp, shp, shp, shp, shp_t, shp_t],
        compiler_params=_params("parallel", "parallel"),
        name="rope_tables",
    )(positions.reshape(bsz, s, 1), inv)


def _prenorm_matmul_kernel(x_ref, g_ref, sh_ref, sc_ref, w_ref, o_ref, *, tn):
    h = (_rms(x_ref[...], g_ref[...]) * (1.0 + sc_ref[...]) + sh_ref[...]).astype(BF16)
    for j in range(w_ref.shape[1] // tn):
        cols = slice(j * tn, (j + 1) * tn)
        o_ref[:, cols] = jnp.dot(h, w_ref[:, cols], preferred_element_type=F32).astype(o_ref.dtype)


def _prenorm_matmul(l, x, g, mod, k_shift, k_scale, w, tn, name):
    bsz, s, d = x.shape
    n = w.shape[-1]
    tm = 512
    return pl.pallas_call(
        functools.partial(_prenorm_matmul_kernel, tn=tn),
        grid=(bsz, s // tm),
        in_specs=[
            pl.BlockSpec((None, tm, d), lambda b, i: (b, i, 0)),
            _of_layer(l, 1, d),
            _mod_spec(l, k_shift, d), _mod_spec(l, k_scale, d),
            _of_layer(l, d, n, pipeline_mode=pl.Buffered(1)),
        ],
        out_specs=pl.BlockSpec((None, tm, n), lambda b, i: (b, i, 0)),
        out_shape=jax.ShapeDtypeStruct((bsz, s, n), BF16),
        compiler_params=_params("parallel", "parallel"),
        name=name,
    )(x, g, mod, mod, w)


def _lru_kernel(t_ref, cw_ref, cb_ref, wbd_ref, bb_ref, lam_ref, o_ref, xbuf_ref, hc_ref, h_ref, *, ts):
    w = LRU_WIDTH

    @pl.when(pl.program_id(1) == 0)
    def _():
        xbuf_ref[0:SUBLANES, :] = jnp.zeros((SUBLANES, w), F32)
        hc_ref[...] = jnp.zeros((1, w), F32)

    xb = t_ref[:, 0:w].astype(F32)
    gb = t_ref[:, w:2 * w].astype(F32)
    xbuf_ref[SUBLANES:SUBLANES + ts, :] = xb
    xc = xb * cw_ref[LRU_CONV - 1:LRU_CONV, :] + cb_ref[...]
    for k in range(LRU_CONV - 1):
        back = LRU_CONV - 1 - k
        xc = xc + xbuf_ref[pl.ds(SUBLANES - back, ts), :] * cw_ref[k:k + 1, :]
    xbuf_ref[0:SUBLANES, :] = xb[ts - SUBLANES:, :]

    z = jnp.dot(xc.astype(BF16), wbd_ref[...], preferred_element_type=F32) + bb_ref[...]
    r = _sigmoid(z[:, :w])
    ig = _sigmoid(z[:, w:])
    nl = -lam_ref[...]
    softplus = jnp.maximum(nl, 0.0) + jnp.log1p(jnp.exp(-jnp.abs(nl)))
    log_a = (-LRU_C) * r * softplus
    a = jnp.exp(log_a)
    th = jnp.tanh(log_a)
    u = jnp.sqrt(-2.0 * th / (1.0 - th)) * (ig * xc)

    groups = ts // SUBLANES
    a = a.reshape(groups, SUBLANES, w)
    u = u.reshape(groups, SUBLANES, w)
    sub = lax.broadcasted_iota(jnp.int32, (groups, SUBLANES, w), 1)
    k = 1
    while k < SUBLANES:
        keep = sub >= k
        a_prev = jnp.where(keep, pltpu.roll(a, k, 1), 1.0)
        u_prev = jnp.where(keep, pltpu.roll(u, k, 1), 0.0)
        u = a * u_prev + u
        a = a * a_prev
        k *= 2
    h_prev = hc_ref[...]
    for r in range(groups):
        hb = a[r] * h_prev + u[r]
        h_ref[r * SUBLANES:(r + 1) * SUBLANES, :] = hb
        h_prev = hb[SUBLANES - 1:SUBLANES, :]
    hc_ref[...] = h_prev
    o_ref[...] = (h_ref[...] * _gelu_tanh(gb)).astype(o_ref.dtype)


def _lru(l, t, conv_w, conv_b, wbd, bb, lam):
    bsz, s, _ = t.shape
    ts = 256
    w = LRU_WIDTH
    return pl.pallas_call(
        functools.partial(_lru_kernel, ts=ts),
        grid=(bsz, s // ts),
        in_specs=[
            pl.BlockSpec((None, ts, 2 * w), lambda b, i: (b, i, COL_LRU // (2 * w))),
            _of_layer(l, LRU_CONV, w), _of_layer(l, 1, w), _of_layer(l, w, 2 * w), _of_layer(l, 1, 2 * w),
            _of_layer(l, 1, w),
        ],
        out_specs=pl.BlockSpec((None, ts, w), lambda b, i: (b, i, 0)),
        out_shape=jax.ShapeDtypeStruct((bsz, s, w), BF16),
        scratch_shapes=[pltpu.VMEM((ts + SUBLANES, w), F32), pltpu.VMEM((1, w), F32), pltpu.VMEM((ts, w), F32)],
        compiler_params=_params("parallel", "arbitrary"),
        name="lru_mixer",
    )(t, conv_w, conv_b, wbd, bb, lam)


def _ret_kernel(q_ref, k_ref, v_ref, g_ref, cos_ref, sin_ref, dec_ref, qdec_ref, kvdec_ref, cdec_ref,
                bd_ref, o_ref, st_ref, *, n_chunks):
    c_len = RET_CHUNK
    pairs = RET_HEADS // 2

    @pl.when(pl.program_id(1) == 0)
    def _():
        st_ref[...] = jnp.zeros(st_ref.shape, F32)

    lane = lax.broadcasted_iota(jnp.int32, (c_len, LANES), 1)
    head0 = lane < RET_DK
    first_half = (lane % RET_DK) < (RET_DK // 2)
    inv_n = 1.0 / RET_DV

    def rope(x, cos, sin):
        swapped = jnp.where(first_half, pltpu.roll(x, LANES - RET_DK // 2, 1), pltpu.roll(x, RET_DK // 2, 1))
        return x * cos + swapped * sin

    def head_mean(x):
        m0 = jnp.sum(jnp.where(head0, x, 0.0), axis=-1, keepdims=True) * inv_n
        m1 = jnp.sum(jnp.where(head0, 0.0, x), axis=-1, keepdims=True) * inv_n
        return jnp.where(head0, m0, m1)

    nt = (((1,), (1,)), ((), ()))
    tn = (((0,), (0,)), ((), ()))
    for c in range(n_chunks):
        rows = slice(c * c_len, (c + 1) * c_len)
        cos = cos_ref[rows, :]
        sin = sin_ref[rows, :]
        for p in range(pairs):
            cols = slice(p * LANES, (p + 1) * LANES)
            q = rope(q_ref[rows, cols].astype(F32), cos, sin)
            k = rope(k_ref[rows, cols].astype(F32), cos, sin) * (RET_DK ** -0.5)
            v = v_ref[rows, cols]
            qb = q.astype(BF16)
            kb = k.astype(BF16)
            zero = jnp.zeros_like(qb)
            s0 = lax.dot_general(jnp.where(head0, qb, zero), kb, nt, preferred_element_type=F32)
            s1 = lax.dot_general(jnp.where(head0, zero, qb), kb, nt, preferred_element_type=F32)
            probs = jnp.concatenate([s0 * dec_ref[2 * p], s1 * dec_ref[2 * p + 1]], axis=1).astype(BF16)
            v2 = jnp.concatenate([jnp.where(head0, v, zero), jnp.where(head0, zero, v)], axis=0)
            y = jnp.dot(probs, v2, preferred_element_type=F32)
            state = st_ref[p]
            y = y + jnp.dot(qb, state.astype(BF16), preferred_element_type=F32) * qdec_ref[p]
            vd = (v.astype(F32) * kvdec_ref[p]).astype(BF16)
            kv = lax.dot_general(kb, vd, tn, preferred_element_type=F32)
            st_ref[p] = state * cdec_ref[p] + kv * bd_ref[...]

            d = y - head_mean(y)
            yn = d * lax.rsqrt(head_mean(d * d) + EPS)
            g = g_ref[rows, cols].astype(F32)
            o_ref[rows, cols] = (g * _sigmoid(g) * yn).astype(o_ref.dtype)


def _ret_constants():
    f32 = F32
    log_gamma = jnp.log1p(-(2.0 ** (-5.0 - jnp.arange(RET_HEADS, dtype=f32))))
    idx = jnp.arange(RET_CHUNK, dtype=f32)
    diff = idx[:, None] - idx[None, :]
    causal = diff >= 0
    inner = jnp.where(causal[None], jnp.exp(jnp.where(causal, diff, 0.0)[None] * log_gamma[:, None, None]), 0.0)
    kv_decay = jnp.exp((RET_CHUNK - 1.0 - idx)[None, :] * log_gamma[:, None])
    q_decay = jnp.exp((idx + 1.0)[:, None] * log_gamma[None, :])
    chunk_decay = jnp.exp(RET_CHUNK * log_gamma)
    pairs = RET_HEADS // 2

    def by_lane(per_head):
        rows = per_head.shape[0]
        return jnp.repeat(per_head.reshape(rows, pairs, 2), RET_DK, axis=2).reshape(rows, pairs, LANES).transpose(1, 0, 2)

    qdec = by_lane(q_decay)
    kvdec = by_lane(kv_decay.T)
    cdec = by_lane(chunk_decay[None, :])
    lane_head = jnp.arange(LANES) // RET_DK
    bd = (lane_head[:, None] == lane_head[None, :]).astype(f32)
    return inner, qdec, kvdec, cdec, bd


def _ret(t, cos_r, sin_r):
    bsz, s, _ = t.shape
    tc = 512
    w = RET_HEADS * RET_DK
    inner, qdec, kvdec, cdec, bd = _ret_constants()
    base = COL_RET // w
    col = lambda j: pl.BlockSpec((None, tc, w), lambda b, i: (b, i, base + j))
    tab = pl.BlockSpec((None, tc, LANES), lambda b, i: (b, i, 0))
    const = lambda shape: pl.BlockSpec(shape, lambda b, i: (0,) * len(shape))
    return pl.pallas_call(
        functools.partial(_ret_kernel, n_chunks=tc // RET_CHUNK),
        grid=(bsz, s // tc),
        in_specs=[col(0), col(1), col(2), col(3), tab, tab,
                  const(inner.shape), const(qdec.shape), const(kvdec.shape), const(cdec.shape), const(bd.shape)],
        out_specs=pl.BlockSpec((None, tc, w), lambda b, i: (b, i, 0)),
        out_shape=jax.ShapeDtypeStruct((bsz, s, w), BF16),
        scratch_shapes=[pltpu.VMEM((RET_HEADS // 2, LANES, LANES), F32)],
        compiler_params=_params("parallel", "arbitrary"),
        name="ret_mixer",
    )(t, t, t, t, cos_r, sin_r, inner, qdec, kvdec, cdec, bd)


def _mla_proj_kernel(t_ref, gq_ref, gkv_ref, wqt_ref, wk_ref, wvt_ref, cos_ref, sin_ref, cost_ref, sint_ref,
                     qt_ref, k_ref, vt_ref):
    ts = t_ref.shape[0]
    half = MLA_ROPE // 2
    nt = (((1,), (1,)), ((), ()))
    ckv = t_ref[:, 0:MLA_KV_RANK].astype(F32)
    kr = t_ref[:, MLA_KV_RANK:MLA_KV_RANK + LANES].astype(F32)
    cq = t_ref[:, MLA_PACK - MLA_Q_RANK:MLA_PACK].astype(F32)
    cqn = _rms(cq, gq_ref[...]).astype(BF16)
    ckvn = _rms(ckv, gkv_ref[...]).astype(BF16)

    qt = lax.dot_general(wqt_ref[...], cqn, nt, preferred_element_type=F32)
    cos_t = cost_ref[...]
    sin_t = sint_ref[...]
    scale = (MLA_NOPE + MLA_ROPE) ** -0.5 * LOG2_E
    for h in range(MLA_HEADS):
        base = h * LANES
        x1 = qt[base + MLA_NOPE:base + MLA_NOPE + half]
        x2 = qt[base + MLA_NOPE + half:base + MLA_NOPE + MLA_ROPE]
        blk = jnp.concatenate([qt[base:base + MLA_NOPE], x1 * cos_t - x2 * sin_t, x1 * sin_t + x2 * cos_t,
                               qt[base + MLA_NOPE + MLA_ROPE:base + LANES]], axis=0)
        qt_ref[base:base + LANES, :] = (blk * scale).astype(qt_ref.dtype)

    lane = lax.broadcasted_iota(jnp.int32, (ts, LANES), 1)
    swapped = jnp.where(lane < MLA_NOPE + half, pltpu.roll(kr, LANES - half, 1), pltpu.roll(kr, half, 1))
    k_rope = kr * cos_ref[...] + swapped * sin_ref[...]
    kn = jnp.dot(ckvn, wk_ref[...], preferred_element_type=F32)
    for h in range(MLA_HEADS):
        cols = slice(h * LANES, (h + 1) * LANES)
        k_ref[:, cols] = (kn[:, cols] + k_rope).astype(k_ref.dtype)

    vt = lax.dot_general(wvt_ref[...], ckvn, nt, preferred_element_type=F32)
    row = lax.broadcasted_iota(jnp.int32, vt.shape, 0)
    vt_ref[...] = jnp.where(row % LANES == MLA_V, 1.0, vt).astype(vt_ref.dtype)


def _mla_proj(l, t, gq, gkv, wqt, wk, wvt, cos_m, sin_m, cos_mt, sin_mt):
    bsz, s, _ = t.shape
    ts = 512
    hq = MLA_HEADS * LANES
    half = MLA_ROPE // 2
    tab = pl.BlockSpec((None, ts, LANES), lambda b, i: (b, i, 0))
    tab_t = pl.BlockSpec((None, half, ts), lambda b, i: (b, 0, i))
    rowmajor = pl.BlockSpec((None, ts, hq), lambda b, i: (b, i, 0))
    transposed = pl.BlockSpec((None, hq, ts), lambda b, i: (b, 0, i))
    return pl.pallas_call(
        _mla_proj_kernel,
        grid=(bsz, s // ts),
        in_specs=[pl.BlockSpec((None, ts, MLA_PACK), lambda b, i: (b, i, COL_MLA // MLA_PACK)),
                  _of_layer(l, 1, MLA_Q_RANK), _of_layer(l, 1, MLA_KV_RANK), _of_layer(l, *wqt.shape[1:]),
                  _of_layer(l, *wk.shape[1:]), _of_layer(l, *wvt.shape[1:]), tab, tab, tab_t, tab_t],
        out_specs=[transposed, rowmajor, transposed],
        out_shape=[jax.ShapeDtypeStruct((bsz, hq, s), BF16), jax.ShapeDtypeStruct((bsz, s, hq), BF16),
                   jax.ShapeDtypeStruct((bsz, hq, s), BF16)],
        compiler_params=_params("parallel", "parallel"),
        name="mla_proj",
    )(t, gq, gkv, wqt, wk, wvt, cos_m, sin_m, cos_mt, sin_mt)


def _flash_kernel(qt_ref, k_ref, vt_ref, o_ref, acc_ref, sa_ref, sb_ref, *, tq):
    g = pl.program_id(2)
    acc_ref[...] = jnp.zeros(acc_ref.shape, F32)

    def qk(j, dst_ref, which, masked):
        off = pl.multiple_of(j * tq, tq)
        for h in range(FLASH_HEADS):
            rows = slice(h * LANES, (h + 1) * LANES)
            s = jnp.dot(k_ref[pl.ds(off, tq), rows], qt_ref[rows, which * tq:(which + 1) * tq],
                        preferred_element_type=F32)
            if masked:
                key = lax.broadcasted_iota(jnp.int32, (tq, tq), 0)
                qry = lax.broadcasted_iota(jnp.int32, (tq, tq), 1)
                s = jnp.where(key <= qry, s, -1e30)
            dst_ref[h] = s

    def softmax_pv(j, src_ref, which, ms):
        off = pl.multiple_of(j * tq, tq)
        new_m = []
        for h in range(FLASH_HEADS):
            s = src_ref[h]
            m_new = jnp.maximum(ms[h], jnp.max(s, axis=0, keepdims=True))
            alpha = jnp.exp2(ms[h] - m_new)
            p = jnp.exp2(s - m_new).astype(BF16)
            vt = vt_ref[h * LANES:h * LANES + V_ROWS, pl.ds(off, tq)]
            acc_ref[which, h] = alpha * acc_ref[which, h] + jnp.dot(vt, p, preferred_element_type=F32)
            new_m.append(m_new)
        return tuple(new_m)

    def sweep(which, diag, first_ref, other_ref):
        def pair(jj, ms):
            qk(2 * jj, other_ref, which, False)
            ms = softmax_pv(jnp.where(jj == 0, diag, 2 * jj - 1), first_ref, which, ms)
            qk(2 * jj + 1, first_ref, which, False)
            return softmax_pv(2 * jj, other_ref, which, ms)

        m0 = jnp.full((1, tq), -1e30, F32)
        ms = lax.fori_loop(0, g, pair, (m0,) * FLASH_HEADS)
        return ms, jnp.where(g == 0, diag, 2 * g - 1)

    def write_out(which):
        for c in range(FLASH_HEADS // 2):
            outs = []
            for h in (2 * c, 2 * c + 1):
                acc = acc_ref[which, h]
                outs.append(acc[0:MLA_V, :] / acc[MLA_V:MLA_V + 1, :])
            o_ref[which * tq:(which + 1) * tq, c * LANES:(c + 1) * LANES] = (
                jnp.concatenate(outs, axis=0).T.astype(o_ref.dtype))

    qk(2 * g, sa_ref, 0, True)
    ms, left = sweep(0, 2 * g, sa_ref, sb_ref)
    qk(2 * g + 1, sb_ref, 1, True)
    softmax_pv(left, sa_ref, 0, ms)
    write_out(0)
    ms, left = sweep(1, 2 * g + 1, sb_ref, sa_ref)
    qk(2 * g, sa_ref, 1, False)
    ms = softmax_pv(left, sb_ref, 1, ms)
    softmax_pv(2 * g, sa_ref, 1, ms)
    write_out(1)


def _flash(qt, k, vt):
    bsz, s, _ = k.shape
    tq = 512
    hs = FLASH_HEADS
    return pl.pallas_call(
        functools.partial(_flash_kernel, tq=tq),
        grid=(bsz, MLA_HEADS // hs, s // (2 * tq)),
        in_specs=[
            pl.BlockSpec((None, hs * LANES, 2 * tq), lambda b, p, g: (b, p, g)),
            pl.BlockSpec((None, s, hs * LANES), lambda b, p, g: (b, 0, p)),
            pl.BlockSpec((None, hs * LANES, s), lambda b, p, g: (b, p, 0)),
        ],
        out_specs=pl.BlockSpec((None, 2 * tq, hs * MLA_V), lambda b, p, g: (b, g, p)),
        out_shape=jax.ShapeDtypeStruct((bsz, s, MLA_HEADS * MLA_V), BF16),
        scratch_shapes=[pltpu.VMEM((2, hs, V_ROWS, tq), F32), pltpu.VMEM((hs, tq, tq), F32),
                        pltpu.VMEM((hs, tq, tq), F32)],
        compiler_params=_params("parallel", "parallel", "arbitrary"),
        name="mla_flash",
    )(qt, k, vt)


def _merge_kernel(yl_ref, yr_ref, ym_ref, gt_ref, x_ref, wl_ref, wr_ref, wm_ref, wo_ref, gp_ref, gm_ref, o_ref):
    d = D_MODEL
    merged = _sigmoid(gt_ref[:, 0:d].astype(F32)) * jnp.dot(yl_ref[...], wl_ref[...], preferred_element_type=F32)
    merged = merged + _sigmoid(gt_ref[:, d:2 * d].astype(F32)) * jnp.dot(
        yr_ref[...], wr_ref[...], preferred_element_type=F32)
    merged = merged + _sigmoid(gt_ref[:, 2 * d:3 * d].astype(F32)) * jnp.dot(
        ym_ref[...], wm_ref[...], preferred_element_type=F32)
    y = jnp.dot(merged.astype(BF16), wo_ref[...], preferred_element_type=F32)
    o_ref[...] = x_ref[...] + gm_ref[...] * _rms(y, gp_ref[...])


def _merge(l, y_lru, y_ret, y_mla, t, x, wl, wr, wm, wo, g_post, mod, k_gate):
    bsz, s, d = x.shape
    tm = 512
    w = y_lru.shape[-1]
    br = pl.BlockSpec((None, tm, w), lambda b, i: (b, i, 0))
    return pl.pallas_call(
        _merge_kernel,
        grid=(bsz, s // tm),
        in_specs=[br, br, br,
                  pl.BlockSpec((None, tm, 3 * d), lambda b, i: (b, i, COL_GATE // (3 * d))),
                  pl.BlockSpec((None, tm, d), lambda b, i: (b, i, 0)),
                  _of_layer(l, w, d), _of_layer(l, w, d), _of_layer(l, w, d), _of_layer(l, d, d),
                  _of_layer(l, 1, d), _mod_spec(l, k_gate, d)],
        out_specs=pl.BlockSpec((None, tm, d), lambda b, i: (b, i, 0)),
        out_shape=jax.ShapeDtypeStruct((bsz, s, d), F32),
        compiler_params=_params("parallel", "parallel"),
        name="mixer_merge",
    )(y_lru, y_ret, y_mla, t, x, wl, wr, wm, wo, g_post, mod)


def _ffn_kernel(x_ref, gpre_ref, sh_ref, sc_ref, wu_ref, cw_ref, cb_ref, wd_ref, gp_ref, gf_ref, o_ref,
                act_ref, halo_ref, *, tm, cw):
    @pl.when(pl.program_id(1) == 0)
    def _():
        halo_ref[...] = jnp.zeros(halo_ref.shape, F32)

    row = lax.broadcasted_iota(jnp.int32, (SUBLANES, cw), 0)
    n_sub = act_ref.shape[0]
    sub = tm // n_sub

    def conv(h, c0, gain):
        cols = slice(c0, c0 + cw)
        taps = cw_ref[:, cols] * gain
        bias = cb_ref[:, cols] * gain
        xv = jnp.dot(h, wu_ref[:, cols], preferred_element_type=F32)
        h1 = halo_ref[SUBLANES - 1:SUBLANES, cols]
        h2 = halo_ref[SUBLANES - 2:SUBLANES - 1, cols]
        halo_ref[:, cols] = xv[sub - SUBLANES:, :]
        r1 = pltpu.roll(xv, 1, 0)
        r2 = pltpu.roll(xv, 2, 0)
        xm1 = jnp.concatenate([jnp.where(row == 0, h1, r1[:SUBLANES]), r1[SUBLANES:]], axis=0)
        top2 = jnp.where(row == 0, h2, jnp.where(row == 1, h1, r2[:SUBLANES]))
        xm2 = jnp.concatenate([top2, r2[SUBLANES:]], axis=0)
        y = xv * taps[2:3] + bias
        y = y + xm2 * taps[0:1]
        return y + xm1 * taps[1:2]

    for r in range(n_sub):
        rows = slice(r * sub, (r + 1) * sub)
        h = (_rms(x_ref[rows, :], gpre_ref[...]) * (1.0 + sc_ref[...]) + sh_ref[...]).astype(BF16)
        for c in range(D_FF // cw):
            hu = conv(h, c * cw, 0.5)
            g = conv(h, D_FF + c * cw, 1.0)
            th = jnp.tanh(g * (GELU_C1 + GELU_C2 * (g * g)))
            act_ref[r, :, c * cw:(c + 1) * cw] = ((hu * g) * (1.0 + th)).astype(BF16)
    for r in range(n_sub):
        rows = slice(r * sub, (r + 1) * sub)
        y = jnp.dot(act_ref[r], wd_ref[...], preferred_element_type=F32)
        o_ref[rows, :] = x_ref[rows, :] + gf_ref[...] * _rms(y, gp_ref[...])


def _ffn(l, x, g_pre, mod, k_shift, k_scale, k_gate, wu, conv_w, conv_b, wd, g_post):
    bsz, s, d = x.shape
    tm = 512
    n_sub = 1
    n = wu.shape[-1]
    single = dict(pipeline_mode=pl.Buffered(1))
    return pl.pallas_call(
        functools.partial(_ffn_kernel, tm=tm, cw=256),
        grid=(bsz, s // tm),
        in_specs=[pl.BlockSpec((None, tm, d), lambda b, i: (b, i, 0)),
                  _of_layer(l, 1, d), _mod_spec(l, k_shift, d), _mod_spec(l, k_scale, d),
                  _of_layer(l, d, n, **single), _of_layer(l, FFN_CONV, n), _of_layer(l, 1, n),
                  _of_layer(l, D_FF, d, **single), _of_layer(l, 1, d), _mod_spec(l, k_gate, d)],
        out_specs=pl.BlockSpec((None, tm, d), lambda b, i: (b, i, 0)),
        out_shape=jax.ShapeDtypeStruct((bsz, s, d), F32),
        scratch_shapes=[pltpu.VMEM((n_sub, tm // n_sub, D_FF), BF16), pltpu.VMEM((SUBLANES, n), F32)],
        compiler_params=_params("parallel", "arbitrary"),
        name="ffn_fused",
    )(x, g_pre, mod, mod, wu, conv_w, conv_b, wd, g_post, mod)


def _pack_w_in_kernel(w_ref, o_ref, *, n_all):
    o_cq = 2 * LRU_WIDTH + 4 * RET_HEADS * RET_DK
    o_ckv = o_cq + MLA_Q_RANK
    o_kr = o_ckv + MLA_KV_RANK
    o_gate = o_kr + MLA_ROPE
    n_gate = n_all - o_gate
    rows = w_ref.shape[0]
    o_ref[:, 0:o_cq] = w_ref[:, 0:o_cq]
    o_ref[:, COL_MLA:COL_MLA + MLA_KV_RANK] = w_ref[:, o_ckv:o_kr]
    o_ref[:, N_IN_PACKED - MLA_Q_RANK:N_IN_PACKED] = w_ref[:, o_cq:o_ckv]
    tail = w_ref[:, o_kr:w_ref.shape[1]].astype(F32)
    o_ref[:, COL_GATE:COL_GATE + n_gate] = tail[:, MLA_ROPE:MLA_ROPE + n_gate].astype(BF16)
    kr = jnp.concatenate([jnp.zeros((rows, KR_LANE), F32), tail[:, :MLA_ROPE],
                          jnp.zeros((rows, LANES - KR_LANE - MLA_ROPE), F32)], axis=1)
    o_ref[:, COL_MLA + MLA_KV_RANK:COL_MLA + MLA_KV_RANK + LANES] = kr.astype(BF16)


def _pack_w_in(w, n_all):
    depth, d, n = w.shape
    tr = 256
    return pl.pallas_call(
        functools.partial(_pack_w_in_kernel, n_all=n_all),
        grid=(depth, d // tr),
        in_specs=[pl.BlockSpec((None, tr, n), lambda l, i: (l, i, 0))],
        out_specs=pl.BlockSpec((None, tr, N_IN_PACKED), lambda l, i: (l, i, 0)),
        out_shape=jax.ShapeDtypeStruct((depth, d, N_IN_PACKED), BF16),
        compiler_params=_params("parallel", "parallel"),
        name="pack_w_in",
    )(w)


def _pad_heads(w, width):
    depth, r, _ = w.shape
    w4 = w.reshape(depth, r, MLA_HEADS, width)
    return jnp.pad(w4, ((0, 0), (0, 0), (0, 0), (0, LANES - width))).reshape(depth, r, MLA_HEADS * LANES)


def _pack_w_ukv(w):
    depth, r, _ = w.shape
    w4 = w.reshape(depth, r, MLA_HEADS, MLA_NOPE + MLA_V)
    k = _pad_heads(w4[..., :MLA_NOPE].reshape(depth, r, -1), MLA_NOPE)
    v = _pad_heads(w4[..., MLA_NOPE:].reshape(depth, r, -1), MLA_V)
    return k.astype(BF16), jnp.swapaxes(v, 1, 2).astype(BF16)


def _block_diag(w):
    depth, nb, n, _ = w.shape
    eye = jnp.eye(nb, dtype=w.dtype)
    return (eye[None, :, None, :, None] * w[:, :, :, None, :]).reshape(depth, nb * n, nb * n)


def kernel(x, c, positions, ada_w, ada_b, mix_pre_g, mix_post_g, w_in, lru_conv_w, lru_conv_b, lru_wa, lru_ba, lru_wx, lru_bx, lru_lambda, lru_wo, ret_wo, mla_q_norm_g, mla_w_uq, mla_kv_norm_g, mla_w_ukv, mla_wo, w_out, ffn_pre_g, ffn_post_g, ffn_w_up, ffn_conv_w, ffn_conv_b, ffn_w_down):
    bsz, s, d = x.shape
    depth = w_in.shape[0]
    row = lambda a: a.reshape(depth, 1, a.shape[-1])
    mod = _ada(c, ada_w, ada_b).reshape(depth, SUBLANES, 6, 1, d)
    cos_r, sin_r, cos_m, sin_m, cos_mt, sin_mt = _rope_tables(positions)

    w_in_p = _pack_w_in(jnp.pad(w_in, ((0, 0), (0, 0), (0, N_IN_PACKED - w_in.shape[-1]))).astype(BF16), w_in.shape[-1])
    wbd = jnp.concatenate([_block_diag(lru_wa), _block_diag(lru_wx)], axis=-1).astype(BF16)
    bb = jnp.concatenate([lru_ba, lru_bx], axis=-1).reshape(depth, 1, 2 * LRU_WIDTH)
    w_uqt = jnp.swapaxes(_pad_heads(mla_w_uq, MLA_NOPE + MLA_ROPE), 1, 2).astype(BF16)
    w_uk, w_uvt = _pack_w_ukv(mla_w_ukv)
    wl, wr, wm, wo = (a.astype(BF16) for a in (lru_wo, ret_wo, mla_wo, w_out))
    wu, wd = ffn_w_up.astype(BF16), ffn_w_down.astype(BF16)

    for l in range(depth):
        t = _prenorm_matmul(l, x, row(mix_pre_g), mod, 0, 1, w_in_p, 2304, "mixer_in_proj")
        y_lru = _lru(l, t, lru_conv_w, row(lru_conv_b), wbd, bb, row(lru_lambda))
        y_ret = _ret(t, cos_r, sin_r)
        qt, k, vt = _mla_proj(l, t, row(mla_q_norm_g), row(mla_kv_norm_g), w_uqt, w_uk, w_uvt,
                              cos_m, sin_m, cos_mt, sin_mt)
        y_mla = _flash(qt, k, vt)
        x = _merge(l, y_lru, y_ret, y_mla, t, x, wl, wr, wm, wo, row(mix_post_g), mod, 2)
        x = _ffn(l, x, row(ffn_pre_g), mod, 3, 4, 5, wu, ffn_conv_w, row(ffn_conv_b), wd, row(ffn_post_g))
    return x
```

```python
import functools

import jax
import jax.numpy as jnp
import numpy as np
from jax import lax
from jax.experimental import pallas as pl
from jax.experimental.pallas import tpu as pltpu

F32 = jnp.float32
BF16 = jnp.bfloat16

D_MODEL = 1024
DEPTH = 2
EPS = 1e-6
ROPE_THETA = 10000.0
LRU_WIDTH = 512
LRU_BLOCKS = 8
LRU_BLOCK = LRU_WIDTH // LRU_BLOCKS
LRU_CONV = 4
LRU_C = 8.0
RET_HEADS = 8
RET_DK = 64
RET_DV = 64
RET_CHUNK = 128
MLA_HEADS = 8
MLA_Q_RANK = 384
MLA_KV_RANK = 256
MLA_NOPE = 64
MLA_ROPE = 32
MLA_V = 64
D_FF = 2816
FFN_CONV = 3

LANES = 128
SUBLANES = 8
MXU_DIM = 256
VMEM_LIMIT = 56 * 1024 * 1024

TILES = dict(
    ada_cols=1536,
    rope_rows=1024,
    pack_rows=256,
    in_proj_rows=512, in_proj_cols=2304,
    lru_rows=512,
    ret_rows=1024,
    mla_proj_rows=1024,
    flash_block=512,
    merge_rows=512,
    ffn_rows=512, ffn_cols=MXU_DIM,
)
MASKED_SCORE = -1e30

COL_LRU = 0
COL_RET = 1024
COL_GATE = 3072
COL_MLA = 6144
N_IN_PACKED = 6912
MLA_PACK = 768
KR_LANE = 64
LOG2_E = 1.4426950408889634
FLASH_HEADS = 4
V_ROWS = MLA_V + 16


def _params(*sem):
    return pltpu.CompilerParams(dimension_semantics=sem, vmem_limit_bytes=VMEM_LIMIT)


def _of_layer(l, *tail, **kw):
    return pl.BlockSpec((None,) + tail, lambda *_: (l,) + (0,) * len(tail), **kw)


def _mod_spec(l, k, d):
    return pl.BlockSpec((None, None, None, 1, d), lambda b, *_: (l, b, k, 0, 0))


GELU_C1 = 0.7978845608028654
GELU_C2 = GELU_C1 * 0.044715


def _gelu_tanh(x):
    return 0.5 * x * (1.0 + jnp.tanh(x * (GELU_C1 + GELU_C2 * (x * x))))


def _sigmoid(x):
    return 1.0 / (1.0 + jnp.exp(-x))


def _rms(x, g):
    return x * lax.rsqrt(jnp.mean(x * x, axis=-1, keepdims=True) + EPS) * g


def _ada_kernel(c_ref, w_ref, b_ref, o_ref):
    c = c_ref[...]
    ca = c * _sigmoid(c)
    o_ref[...] = jnp.dot(ca, w_ref[...], preferred_element_type=F32,
                         precision=lax.Precision.HIGHEST) + b_ref[...]


def _ada(c, ada_w, ada_b):
    depth, d, n = ada_w.shape
    rows = SUBLANES
    c_pad = jnp.pad(c, ((0, rows - c.shape[0]), (0, 0)))
    tn = TILES["ada_cols"]
    return pl.pallas_call(
        _ada_kernel,
        grid=(depth, n // tn),
        in_specs=[
            pl.BlockSpec((rows, d), lambda l, j: (0, 0)),
            pl.BlockSpec((None, d, tn), lambda l, j: (l, 0, j)),
            pl.BlockSpec((None, 1, tn), lambda l, j: (l, 0, j)),
        ],
        out_specs=pl.BlockSpec((None, rows, tn), lambda l, j: (l, 0, j)),
        out_shape=jax.ShapeDtypeStruct((depth, rows, n), F32),
        compiler_params=_params("parallel", "parallel"),
        name="ada_mod",
    )(c_pad, ada_w, ada_b.reshape(depth, 1, n))


def _rope_kernel(pos_ref, inv_ref, cr_ref, sr_ref, cm_ref, sm_ref, cmt_ref, smt_ref):
    half_r = RET_DK // 2
    half_m = MLA_ROPE // 2
    half_l = LANES // 2
    rows = pos_ref.shape[0] // 2
    pos = pos_ref[...].astype(F32)
    lane = lax.broadcasted_iota(jnp.int32, (rows, LANES), 1)
    ang = jnp.where(lane < half_l, pos[:rows], pos[rows:]) * inv_ref[...]
    c2 = jnp.cos(ang)
    s2 = jnp.sin(ang)

    def unpack(x, lo, n):
        return jnp.concatenate([x[:, lo:lo + n], x[:, half_l + lo:half_l + lo + n]], axis=0)

    c_r, s_r = unpack(c2, 0, half_r), unpack(s2, 0, half_r)
    c_m, s_m = unpack(c2, half_r, half_m), unpack(s2, half_r, half_m)
    reps = LANES // RET_DK
    cr_ref[...] = jnp.concatenate([c_r, c_r] * reps, axis=1)
    sr_ref[...] = jnp.concatenate([-s_r, s_r] * reps, axis=1)
    ts = 2 * rows
    tail = LANES - MLA_NOPE - MLA_ROPE
    cm_ref[...] = jnp.concatenate([jnp.ones((ts, MLA_NOPE), F32), c_m, c_m, jnp.ones((ts, tail), F32)], axis=1)
    sm_ref[...] = jnp.concatenate([jnp.zeros((ts, MLA_NOPE), F32), -s_m, s_m, jnp.zeros((ts, tail), F32)], axis=1)
    c2t = c2.T
    s2t = s2.T
    cmt_ref[...] = jnp.concatenate([c2t[half_r:half_r + half_m], c2t[half_l + half_r:half_l + half_r + half_m]], axis=1)
    smt_ref[...] = jnp.concatenate([s2t[half_r:half_r + half_m], s2t[half_l + half_r:half_l + half_r + half_m]], axis=1)


def _rope_tables(positions):
    bsz, s = positions.shape
    ts = TILES["rope_rows"]
    inv_r =ROPE_THETA ** (-jnp.arange(0, RET_DK, 2, dtype=F32) / RET_DK)
    inv_m = ROPE_THETA ** (-jnp.arange(0, MLA_ROPE, 2, dtype=F32) / MLA_ROPE)
    inv_half = jnp.concatenate([inv_r, inv_m, jnp.zeros(LANES // 2 - inv_r.size - inv_m.size, F32)])
    inv = jnp.tile(inv_half, 2).reshape(1, LANES)
    tab = pl.BlockSpec((None, ts, LANES), lambda b, i: (b, i, 0))
    shp = jax.ShapeDtypeStruct((bsz, s, LANES), F32)
    tab_t = pl.BlockSpec((None, inv_m.size, ts), lambda b, i: (b, 0, i))
    shp_t = jax.ShapeDtypeStruct((bsz, inv_m.size, s), F32)
    return pl.pallas_call(
        _rope_kernel,
        grid=(bsz, s // ts),
        in_specs=[pl.BlockSpec((None, ts, 1), lambda b, i: (b, i, 0)), pl.BlockSpec((1, LANES), lambda b, i: (0, 0))],
        out_specs=[tab, tab, tab, tab, tab_t, tab_t],
        out_shape=[shp, shp, shp, shp, shp_t, shp_t],
        compiler_params=_params("parallel", "parallel"),
        name="rope_tables",
    )(positions.reshape(bsz, s, 1), inv)


def _prenorm_matmul_kernel(x_ref, g_ref, sh_ref, sc_ref, w_ref, o_ref, *, tn):
    h = (_rms(x_ref[...], g_ref[...]) * (1.0 + sc_ref[...]) + sh_ref[...]).astype(BF16)
    for j in range(w_ref.shape[1] // tn):
        cols = slice(j * tn, (j + 1) * tn)
        o_ref[:, cols] = jnp.dot(h, w_ref[:, cols], preferred_element_type=F32).astype(o_ref.dtype)


def _prenorm_matmul(l, x, g, mod, k_shift, k_scale, w, name):
    bsz, s, d = x.shape
    n = w.shape[-1]
    tm, tn = TILES["in_proj_rows"], TILES["in_proj_cols"]
    return pl.pallas_call(
        functools.partial(_prenorm_matmul_kernel, tn=tn),
        grid=(bsz, s // tm),
        in_specs=[
            pl.BlockSpec((None, tm, d), lambda b, i: (b, i, 0)),
            _of_layer(l, 1, d),
            _mod_spec(l, k_shift, d), _mod_spec(l, k_scale, d),
            _of_layer(l, d, n, pipeline_mode=pl.Buffered(1)),
        ],
        out_specs=pl.BlockSpec((None, tm, n), lambda b, i: (b, i, 0)),
        out_shape=jax.ShapeDtypeStruct((bsz, s, n), BF16),
        compiler_params=_params("parallel", "parallel"),
        name=name,
    )(x, g, mod, mod, w)


def _lru_kernel(t_ref, cw_ref, cb_ref, wbd_ref, bb_ref, lam_ref, o_ref, xbuf_ref, hc_ref, h_ref, *, ts):
    w = LRU_WIDTH

    @pl.when(pl.program_id(1) == 0)
    def _():
        xbuf_ref[0:SUBLANES, :] = jnp.zeros((SUBLANES, w), F32)
        hc_ref[...] = jnp.zeros((1, w), F32)

    xb = t_ref[:, 0:w].astype(F32)
    gb = t_ref[:, w:2 * w].astype(F32)
    xbuf_ref[SUBLANES:SUBLANES + ts, :] = xb
    xc = xb * cw_ref[LRU_CONV - 1:LRU_CONV, :] + cb_ref[...]
    for k in range(LRU_CONV - 1):
        back = LRU_CONV - 1 - k
        xc = xc + xbuf_ref[pl.ds(SUBLANES - back, ts), :] * cw_ref[k:k + 1, :]
    xbuf_ref[0:SUBLANES, :] = xb[ts - SUBLANES:, :]

    z = jnp.dot(xc.astype(BF16), wbd_ref[...], preferred_element_type=F32) + bb_ref[...]
    r = _sigmoid(z[:, :w])
    ig = _sigmoid(z[:, w:])
    nl = -lam_ref[...]
    softplus = jnp.maximum(nl, 0.0) + jnp.log1p(jnp.exp(-jnp.abs(nl)))
    log_a = (-LRU_C) * r * softplus
    a = jnp.exp(log_a)
    th = jnp.tanh(log_a)
    u = jnp.sqrt(-2.0 * th / (1.0 - th)) * (ig * xc)

    groups = ts // SUBLANES
    a = a.reshape(groups, SUBLANES, w)
    u = u.reshape(groups, SUBLANES, w)
    sub = lax.broadcasted_iota(jnp.int32, (groups, SUBLANES, w), 1)
    k = 1
    while k < SUBLANES:
        keep = sub >= k
        a_prev = jnp.where(keep, pltpu.roll(a, k, 1), 1.0)
        u_prev = jnp.where(keep, pltpu.roll(u, k, 1), 0.0)
        u = a * u_prev + u
        a = a * a_prev
        k *= 2
    h_prev = hc_ref[...]
    for r in range(groups):
        hb = a[r] * h_prev + u[r]
        h_ref[r * SUBLANES:(r + 1) * SUBLANES, :] = hb
        h_prev = hb[SUBLANES - 1:SUBLANES, :]
    hc_ref[...] = h_prev
    o_ref[...] = (h_ref[...] * _gelu_tanh(gb)).astype(o_ref.dtype)


def _lru(l, t, conv_w, conv_b, wbd, bb, lam):
    bsz, s, _ = t.shape
    ts = TILES["lru_rows"]
    w = LRU_WIDTH
    return pl.pallas_call(
        functools.partial(_lru_kernel, ts=ts),
        grid=(bsz, s // ts),
        in_specs=[
            pl.BlockSpec((None, ts, 2 * w), lambda b, i: (b, i, COL_LRU // (2 * w))),
            _of_layer(l, LRU_CONV, w), _of_layer(l, 1, w), _of_layer(l, w, 2 * w), _of_layer(l, 1, 2 * w),
            _of_layer(l, 1, w),
        ],
        out_specs=pl.BlockSpec((None, ts, w), lambda b, i: (b, i, 0)),
        out_shape=jax.ShapeDtypeStruct((bsz, s, w), BF16),
        scratch_shapes=[pltpu.VMEM((ts + SUBLANES, w), F32), pltpu.VMEM((1, w), F32), pltpu.VMEM((ts, w), F32)],
        compiler_params=_params("parallel", "arbitrary"),
        name="lru_mixer",
    )(t, conv_w, conv_b, wbd, bb, lam)


def _ret_kernel(q_ref, k_ref, v_ref, g_ref, cos_ref, sin_ref, dec_ref, qdec_ref, kvdec_ref, cdec_ref,
                bd_ref, o_ref, st_ref, *, n_chunks):
    c_len = RET_CHUNK
    pairs = RET_HEADS // 2

    @pl.when(pl.program_id(1) == 0)
    def _():
        st_ref[...] = jnp.zeros(st_ref.shape, F32)

    lane = lax.broadcasted_iota(jnp.int32, (c_len, LANES), 1)
    head0 = lane < RET_DK
    first_half = (lane % RET_DK) < (RET_DK // 2)
    inv_n = 1.0 / RET_DV

    def rope(x, cos, sin):
        swapped = jnp.where(first_half, pltpu.roll(x, LANES - RET_DK // 2, 1), pltpu.roll(x, RET_DK // 2, 1))
        return x * cos + swapped * sin

    def head_mean(x):
        m0 = jnp.sum(jnp.where(head0, x, 0.0), axis=-1, keepdims=True) * inv_n
        m1 = jnp.sum(jnp.where(head0, 0.0, x), axis=-1, keepdims=True) * inv_n
        return jnp.where(head0, m0, m1)

    nt = (((1,), (1,)), ((), ()))
    tn = (((0,), (0,)), ((), ()))
    for c in range(n_chunks):
        rows = slice(c * c_len, (c + 1) * c_len)
        cos = cos_ref[rows, :]
        sin = sin_ref[rows, :]
        for p in range(pairs):
            cols = slice(p * LANES, (p + 1) * LANES)
            q = rope(q_ref[rows, cols].astype(F32), cos, sin)
            k = rope(k_ref[rows, cols].astype(F32), cos, sin) * (RET_DK ** -0.5)
            v = v_ref[rows, cols]
            qb = q.astype(BF16)
            kb = k.astype(BF16)
            zero = jnp.zeros_like(qb)
            s0 = lax.dot_general(jnp.where(head0, qb, zero), kb, nt, preferred_element_type=F32)
            s1 = lax.dot_general(jnp.where(head0, zero, qb), kb, nt, preferred_element_type=F32)
            probs = jnp.concatenate([s0 * dec_ref[2 * p], s1 * dec_ref[2 * p + 1]], axis=1).astype(BF16)
            v2 = jnp.concatenate([jnp.where(head0, v, zero), jnp.where(head0, zero, v)], axis=0)
            y = jnp.dot(probs, v2, preferred_element_type=F32)
            state = st_ref[p]
            y = y + jnp.dot(qb, state.astype(BF16), preferred_element_type=F32) * qdec_ref[p]
            vd = (v.astype(F32) * kvdec_ref[p]).astype(BF16)
            kv = lax.dot_general(kb, vd, tn, preferred_element_type=F32)
            st_ref[p] = state * cdec_ref[p] + kv * bd_ref[...]

            d = y - head_mean(y)
            yn = d * lax.rsqrt(head_mean(d * d) + EPS)
            g = g_ref[rows, cols].astype(F32)
            o_ref[rows, cols] = (g * _sigmoid(g) * yn).astype(o_ref.dtype)


def _ret_constants():
    f32 = F32
    log_gamma = jnp.log1p(-(2.0 ** (-5.0 - jnp.arange(RET_HEADS, dtype=f32))))
    idx = jnp.arange(RET_CHUNK, dtype=f32)
    diff = idx[:, None] - idx[None, :]
    causal = diff >= 0
    inner = jnp.where(causal[None], jnp.exp(jnp.where(causal, diff, 0.0)[None] * log_gamma[:, None, None]), 0.0)
    kv_decay = jnp.exp((RET_CHUNK - 1.0 - idx)[None, :] * log_gamma[:, None])
    q_decay = jnp.exp((idx + 1.0)[:, None] * log_gamma[None, :])
    chunk_decay = jnp.exp(RET_CHUNK * log_gamma)
    pairs = RET_HEADS // 2

    def by_lane(per_head):
        rows = per_head.shape[0]
        return jnp.repeat(per_head.reshape(rows, pairs, 2), RET_DK, axis=2).reshape(rows, pairs, LANES).transpose(1, 0, 2)

    qdec = by_lane(q_decay)
    kvdec = by_lane(kv_decay.T)
    cdec = by_lane(chunk_decay[None, :])
    lane_head = jnp.arange(LANES) // RET_DK
    bd = (lane_head[:, None] == lane_head[None, :]).astype(f32)
    return inner, qdec, kvdec, cdec, bd


def _ret(t, cos_r, sin_r):
    bsz, s, _ = t.shape
    tc = TILES["ret_rows"]
    w = RET_HEADS * RET_DK
    inner, qdec, kvdec, cdec, bd = _ret_constants()
    base = COL_RET // w
    col = lambda j: pl.BlockSpec((None, tc, w), lambda b, i: (b, i, base + j))
    tab = pl.BlockSpec((None, tc, LANES), lambda b, i: (b, i, 0))
    const = lambda shape: pl.BlockSpec(shape, lambda b, i: (0,) * len(shape))
    return pl.pallas_call(
        functools.partial(_ret_kernel, n_chunks=tc // RET_CHUNK),
        grid=(bsz, s // tc),
        in_specs=[col(0), col(1), col(2), col(3), tab, tab,
                  const(inner.shape), const(qdec.shape), const(kvdec.shape), const(cdec.shape), const(bd.shape)],
        out_specs=pl.BlockSpec((None, tc, w), lambda b, i: (b, i, 0)),
        out_shape=jax.ShapeDtypeStruct((bsz, s, w), BF16),
        scratch_shapes=[pltpu.VMEM((RET_HEADS // 2, LANES, LANES), F32)],
        compiler_params=_params("parallel", "arbitrary"),
        name="ret_mixer",
    )(t, t, t, t, cos_r, sin_r, inner, qdec, kvdec, cdec, bd)


def _mla_proj_kernel(t_ref, gq_ref, gkv_ref, wqt_ref, wk_ref, wvt_ref, cos_ref, sin_ref, cost_ref, sint_ref,
                     qt_ref, k_ref, vt_ref):
    ts = t_ref.shape[0]
    half = MLA_ROPE // 2
    nt = (((1,), (1,)), ((), ()))
    ckv = t_ref[:, 0:MLA_KV_RANK].astype(F32)
    kr = t_ref[:, MLA_KV_RANK:MLA_KV_RANK + LANES].astype(F32)
    cq = t_ref[:, MLA_PACK - MLA_Q_RANK:MLA_PACK].astype(F32)
    cqn = _rms(cq, gq_ref[...]).astype(BF16)
    ckvn = _rms(ckv, gkv_ref[...]).astype(BF16)

    qt = lax.dot_general(wqt_ref[...], cqn, nt, preferred_element_type=F32)
    cos_t = cost_ref[...]
    sin_t = sint_ref[...]
    scale = (MLA_NOPE + MLA_ROPE) ** -0.5 * LOG2_E
    for h in range(MLA_HEADS):
        base = h * LANES
        x1 = qt[base + MLA_NOPE:base + MLA_NOPE + half]
        x2 = qt[base + MLA_NOPE + half:base + MLA_NOPE + MLA_ROPE]
        blk = jnp.concatenate([qt[base:base + MLA_NOPE], x1 * cos_t - x2 * sin_t, x1 * sin_t + x2 * cos_t,
                               qt[base + MLA_NOPE + MLA_ROPE:base + LANES]], axis=0)
        qt_ref[base:base + LANES, :] = (blk * scale).astype(qt_ref.dtype)

    lane = lax.broadcasted_iota(jnp.int32, (ts, LANES), 1)
    swapped = jnp.where(lane < MLA_NOPE + half, pltpu.roll(kr, LANES - half, 1), pltpu.roll(kr, half, 1))
    k_rope = kr * cos_ref[...] + swapped * sin_ref[...]
    kn = jnp.dot(ckvn, wk_ref[...], preferred_element_type=F32)
    for h in range(MLA_HEADS):
        cols = slice(h * LANES, (h + 1) * LANES)
        k_ref[:, cols] = (kn[:, cols] + k_rope).astype(k_ref.dtype)

    vt = lax.dot_general(wvt_ref[...], ckvn, nt, preferred_element_type=F32)
    row = lax.broadcasted_iota(jnp.int32, vt.shape, 0)
    vt_ref[...] = jnp.where(row % LANES == MLA_V, 1.0, vt).astype(vt_ref.dtype)


def _mla_proj(l, t, gq, gkv, wqt, wk, wvt, cos_m, sin_m, cos_mt, sin_mt):
    bsz, s, _ = t.shape
    ts = TILES["mla_proj_rows"]
    hq = MLA_HEADS * LANES
    half = MLA_ROPE // 2
    tab = pl.BlockSpec((None, ts, LANES), lambda b, i: (b, i, 0))
    tab_t = pl.BlockSpec((None, half, ts), lambda b, i: (b, 0, i))
    rowmajor = pl.BlockSpec((None, ts, hq), lambda b, i: (b, i, 0))
    transposed = pl.BlockSpec((None, hq, ts), lambda b, i: (b, 0, i))
    return pl.pallas_call(
        _mla_proj_kernel,
        grid=(bsz, s // ts),
        in_specs=[pl.BlockSpec((None, ts, MLA_PACK), lambda b, i: (b, i, COL_MLA // MLA_PACK)),
                  _of_layer(l, 1, MLA_Q_RANK), _of_layer(l, 1, MLA_KV_RANK), _of_layer(l, *wqt.shape[1:]),
                  _of_layer(l, *wk.shape[1:]), _of_layer(l, *wvt.shape[1:]), tab, tab, tab_t, tab_t],
        out_specs=[transposed, rowmajor, transposed],
        out_shape=[jax.ShapeDtypeStruct((bsz, hq, s), BF16), jax.ShapeDtypeStruct((bsz, s, hq), BF16),
                   jax.ShapeDtypeStruct((bsz, hq, s), BF16)],
        compiler_params=_params("parallel", "parallel"),
        name="mla_proj",
    )(t, gq, gkv, wqt, wk, wvt, cos_m, sin_m, cos_mt, sin_mt)


def _flash_kernel(qt_ref, k_ref, vt_ref, o_ref, acc_ref, sa_ref, sb_ref, *, tq):
    g = pl.program_id(2)
    acc_ref[...] = jnp.zeros(acc_ref.shape, F32)

    def qk(j, dst_ref, which, masked):
        off = pl.multiple_of(j * tq, tq)
        for h in range(FLASH_HEADS):
            rows = slice(h * LANES, (h + 1) * LANES)
            s = jnp.dot(k_ref[pl.ds(off, tq), rows], qt_ref[rows, which * tq:(which + 1) * tq],
                        preferred_element_type=F32)
            if masked:
                key = lax.broadcasted_iota(jnp.int32, (tq, tq), 0)
                qry = lax.broadcasted_iota(jnp.int32, (tq, tq), 1)
                s = jnp.where(key <= qry, s, MASKED_SCORE)
            dst_ref[h] = s

    def softmax_pv(j, src_ref, which, ms):
        off = pl.multiple_of(j * tq, tq)
        new_m = []
        for h in range(FLASH_HEADS):
            s = src_ref[h]
            m_new = jnp.maximum(ms[h], jnp.max(s, axis=0, keepdims=True))
            alpha = jnp.exp2(ms[h] - m_new)
            p = jnp.exp2(s - m_new).astype(BF16)
            vt = vt_ref[h * LANES:h * LANES + V_ROWS, pl.ds(off, tq)]
            acc_ref[which, h] = alpha * acc_ref[which, h] + jnp.dot(vt, p, preferred_element_type=F32)
            new_m.append(m_new)
        return tuple(new_m)

    def sweep(which, diag, first_ref, other_ref):
        def pair(jj, ms):
            qk(2 * jj, other_ref, which, False)
            ms = softmax_pv(jnp.where(jj == 0, diag, 2 * jj - 1), first_ref, which, ms)
            qk(2 * jj + 1, first_ref, which, False)
            return softmax_pv(2 * jj, other_ref, which, ms)

        m0 = jnp.full((1, tq), MASKED_SCORE, F32)
        ms = lax.fori_loop(0, g, pair, (m0,) * FLASH_HEADS)
        return ms, jnp.where(g == 0, diag, 2 * g - 1)

    def write_out(which):
        for c in range(FLASH_HEADS // 2):
            outs = []
            for h in (2 * c, 2 * c + 1):
                acc = acc_ref[which, h]
                outs.append(acc[0:MLA_V, :] / acc[MLA_V:MLA_V + 1, :])
            o_ref[which * tq:(which + 1) * tq, c * LANES:(c + 1) * LANES] = (
                jnp.concatenate(outs, axis=0).T.astype(o_ref.dtype))

    qk(2 * g, sa_ref, 0, True)
    ms, left = sweep(0, 2 * g, sa_ref, sb_ref)
    qk(2 * g + 1, sb_ref, 1, True)
    softmax_pv(left, sa_ref, 0, ms)
    write_out(0)
    ms, left = sweep(1, 2 * g + 1, sb_ref, sa_ref)
    qk(2 * g, sa_ref, 1, False)
    ms = softmax_pv(left, sb_ref, 1, ms)
    softmax_pv(2 * g, sa_ref, 1, ms)
    write_out(1)


def _flash(qt, k, vt):
    bsz, s, _ = k.shape
    tq = TILES["flash_block"]
    hs = FLASH_HEADS
    return pl.pallas_call(
        functools.partial(_flash_kernel, tq=tq),
        grid=(bsz, MLA_HEADS // hs, s // (2 * tq)),
        in_specs=[
            pl.BlockSpec((None, hs * LANES, 2 * tq), lambda b, p, g: (b, p, g)),
            pl.BlockSpec((None, s, hs * LANES), lambda b, p, g: (b, 0, p)),
            pl.BlockSpec((None, hs * LANES, s), lambda b, p, g: (b, p, 0)),
        ],
        out_specs=pl.BlockSpec((None, 2 * tq, hs * MLA_V), lambda b, p, g: (b, g, p)),
        out_shape=jax.ShapeDtypeStruct((bsz, s, MLA_HEADS * MLA_V), BF16),
        scratch_shapes=[pltpu.VMEM((2, hs, V_ROWS, tq), F32), pltpu.VMEM((hs, tq, tq), F32),
                        pltpu.VMEM((hs, tq, tq), F32)],
        compiler_params=_params("parallel", "parallel", "arbitrary"),
        name="mla_flash",
    )(qt, k, vt)


def _merge_kernel(yl_ref, yr_ref, ym_ref, gt_ref, x_ref, wl_ref, wr_ref, wm_ref, wo_ref, gp_ref, gm_ref, o_ref):
    d = D_MODEL
    merged = _sigmoid(gt_ref[:, 0:d].astype(F32)) * jnp.dot(yl_ref[...], wl_ref[...], preferred_element_type=F32)
    merged = merged + _sigmoid(gt_ref[:, d:2 * d].astype(F32)) * jnp.dot(
        yr_ref[...], wr_ref[...], preferred_element_type=F32)
    merged = merged + _sigmoid(gt_ref[:, 2 * d:3 * d].astype(F32)) * jnp.dot(
        ym_ref[...], wm_ref[...], preferred_element_type=F32)
    y = jnp.dot(merged.astype(BF16), wo_ref[...], preferred_element_type=F32)
    o_ref[...] = x_ref[...] + gm_ref[...] * _rms(y, gp_ref[...])


def _merge(l, y_lru, y_ret, y_mla, t, x, wl, wr, wm, wo, g_post, mod, k_gate):
    bsz, s, d = x.shape
    tm = TILES["merge_rows"]
    w = y_lru.shape[-1]
    br = pl.BlockSpec((None, tm, w), lambda b, i: (b, i, 0))
    return pl.pallas_call(
        _merge_kernel,
        grid=(bsz, s // tm),
        in_specs=[br, br, br,
                  pl.BlockSpec((None, tm, 3 * d), lambda b, i: (b, i, COL_GATE // (3 * d))),
                  pl.BlockSpec((None, tm, d), lambda b, i: (b, i, 0)),
                  _of_layer(l, w, d), _of_layer(l, w, d), _of_layer(l, w, d), _of_layer(l, d, d),
                  _of_layer(l, 1, d), _mod_spec(l, k_gate, d)],
        out_specs=pl.BlockSpec((None, tm, d), lambda b, i: (b, i, 0)),
        out_shape=jax.ShapeDtypeStruct((bsz, s, d), F32),
        compiler_params=_params("parallel", "parallel"),
        name="mixer_merge",
    )(y_lru, y_ret, y_mla, t, x, wl, wr, wm, wo, g_post, mod)


def _ffn_kernel(x_ref, gpre_ref, sh_ref, sc_ref, wu_ref, cw_ref, cb_ref, wd_ref, gp_ref, gf_ref, o_ref,
                act_ref, halo_ref, *, tm, cw):
    @pl.when(pl.program_id(1) == 0)
    def _():
        halo_ref[...] = jnp.zeros(halo_ref.shape, F32)

    row = lax.broadcasted_iota(jnp.int32, (SUBLANES, cw), 0)
    x = x_ref[...]
    h = (_rms(x, gpre_ref[...]) * (1.0 + sc_ref[...]) + sh_ref[...]).astype(BF16)

    def conv(c0, gain):
        cols = slice(c0, c0 + cw)
        taps = cw_ref[:, cols] * gain
        bias = cb_ref[:, cols] * gain
        xv = jnp.dot(h, wu_ref[:, cols], preferred_element_type=F32)
        h1 = halo_ref[SUBLANES - 1:SUBLANES, cols]
        h2 = halo_ref[SUBLANES - 2:SUBLANES - 1, cols]
        halo_ref[:, cols] = xv[tm - SUBLANES:, :]
        r1 = pltpu.roll(xv, 1, 0)
        r2 = pltpu.roll(xv, 2, 0)
        xm1 = jnp.concatenate([jnp.where(row == 0, h1, r1[:SUBLANES]), r1[SUBLANES:]], axis=0)
        top2 = jnp.where(row == 0, h2, jnp.where(row == 1, h1, r2[:SUBLANES]))
        xm2 = jnp.concatenate([top2, r2[SUBLANES:]], axis=0)
        y = xv * taps[2:3] + bias
        y = y + xm2 * taps[0:1]
        return y + xm1 * taps[1:2]

    for c in range(D_FF // cw):
        hu = conv(c * cw, 0.5)
        g = conv(D_FF + c * cw, 1.0)
        th = jnp.tanh(g * (GELU_C1 + GELU_C2 * (g * g)))
        act_ref[:, c * cw:(c + 1) * cw] = ((hu * g) * (1.0 + th)).astype(BF16)
    y = jnp.dot(act_ref[...], wd_ref[...], preferred_element_type=F32)
    o_ref[...] = x + gf_ref[...] * _rms(y, gp_ref[...])


def _ffn(l, x, g_pre, mod, k_shift, k_scale, k_gate, wu, conv_w, conv_b, wd, g_post):
    bsz, s, d = x.shape
    tm = TILES["ffn_rows"]
    n = wu.shape[-1]
    single = dict(pipeline_mode=pl.Buffered(1))
    return pl.pallas_call(
        functools.partial(_ffn_kernel, tm=tm, cw=TILES["ffn_cols"]),
        grid=(bsz, s // tm),
        in_specs=[pl.BlockSpec((None, tm, d), lambda b, i: (b, i, 0)),
                  _of_layer(l, 1, d), _mod_spec(l, k_shift, d), _mod_spec(l, k_scale, d),
                  _of_layer(l, d, n, **single), _of_layer(l, FFN_CONV, n), _of_layer(l, 1, n),
                  _of_layer(l, D_FF, d, **single), _of_layer(l, 1, d), _mod_spec(l, k_gate, d)],
        out_specs=pl.BlockSpec((None, tm, d), lambda b, i: (b, i, 0)),
        out_shape=jax.ShapeDtypeStruct((bsz, s, d), F32),
        scratch_shapes=[pltpu.VMEM((tm, D_FF), BF16), pltpu.VMEM((SUBLANES, n), F32)],
        compiler_params=_params("parallel", "arbitrary"),
        name="ffn_fused",
    )(x, g_pre, mod, mod, wu, conv_w, conv_b, wd, g_post, mod)


def _pack_w_in_kernel(w_ref, o_ref, *, n_all):
    o_cq = 2 * LRU_WIDTH + 4 * RET_HEADS * RET_DK
    o_ckv = o_cq + MLA_Q_RANK
    o_kr = o_ckv + MLA_KV_RANK
    o_gate = o_kr + MLA_ROPE
    n_gate = n_all - o_gate
    rows = w_ref.shape[0]
    o_ref[:, 0:o_cq] = w_ref[:, 0:o_cq]
    o_ref[:, COL_MLA:COL_MLA + MLA_KV_RANK] = w_ref[:, o_ckv:o_kr]
    o_ref[:, N_IN_PACKED - MLA_Q_RANK:N_IN_PACKED] = w_ref[:, o_cq:o_ckv]
    tail = w_ref[:, o_kr:w_ref.shape[1]].astype(F32)
    o_ref[:, COL_GATE:COL_GATE + n_gate] = tail[:, MLA_ROPE:MLA_ROPE + n_gate].astype(BF16)
    kr = jnp.concatenate([jnp.zeros((rows, KR_LANE), F32), tail[:, :MLA_ROPE],
                          jnp.zeros((rows, LANES - KR_LANE - MLA_ROPE), F32)], axis=1)
    o_ref[:, COL_MLA + MLA_KV_RANK:COL_MLA + MLA_KV_RANK + LANES] = kr.astype(BF16)


def _pack_w_in(w, n_all):
    depth, d, n = w.shape
    tr = TILES["pack_rows"]
    return pl.pallas_call(
        functools.partial(_pack_w_in_kernel, n_all=n_all),
        grid=(depth, d // tr),
        in_specs=[pl.BlockSpec((None, tr, n), lambda l, i: (l, i, 0))],
        out_specs=pl.BlockSpec((None, tr, N_IN_PACKED), lambda l, i: (l, i, 0)),
        out_shape=jax.ShapeDtypeStruct((depth, d, N_IN_PACKED), BF16),
        compiler_params=_params("parallel", "parallel"),
        name="pack_w_in",
    )(w)


def _pad_heads(w, width):
    depth, r, _ = w.shape
    w4 = w.reshape(depth, r, MLA_HEADS, width)
    return jnp.pad(w4, ((0, 0), (0, 0), (0, 0), (0, LANES - width))).reshape(depth, r, MLA_HEADS * LANES)


def _pack_w_ukv(w):
    depth, r, _ = w.shape
    w4 = w.reshape(depth, r, MLA_HEADS, MLA_NOPE + MLA_V)
    k = _pad_heads(w4[..., :MLA_NOPE].reshape(depth, r, -1), MLA_NOPE)
    v = _pad_heads(w4[..., MLA_NOPE:].reshape(depth, r, -1), MLA_V)
    return k.astype(BF16), jnp.swapaxes(v, 1, 2).astype(BF16)


def _block_diag(w):
    depth, nb, n, _ = w.shape
    eye = jnp.eye(nb, dtype=w.dtype)
    return (eye[None, :, None, :, None] * w[:, :, :, None, :]).reshape(depth, nb * n, nb * n)


def kernel(x, c, positions, ada_w, ada_b, mix_pre_g, mix_post_g, w_in, lru_conv_w, lru_conv_b, lru_wa, lru_ba, lru_wx, lru_bx, lru_lambda, lru_wo, ret_wo, mla_q_norm_g, mla_w_uq, mla_kv_norm_g, mla_w_ukv, mla_wo, w_out, ffn_pre_g, ffn_post_g, ffn_w_up, ffn_conv_w, ffn_conv_b, ffn_w_down):
    bsz, s, d = x.shape
    depth = w_in.shape[0]
    row = lambda a: a.reshape(depth, 1, a.shape[-1])
    mod = _ada(c, ada_w, ada_b).reshape(depth, SUBLANES, 6, 1, d)
    cos_r, sin_r, cos_m, sin_m, cos_mt, sin_mt = _rope_tables(positions)

    n_in = w_in.shape[-1]
    w_in_p = _pack_w_in(jnp.concatenate(
        [w_in.astype(BF16), jnp.zeros((depth, d, N_IN_PACKED - n_in), BF16)], axis=-1), n_in)
    wbd = jnp.concatenate([_block_diag(lru_wa), _block_diag(lru_wx)], axis=-1).astype(BF16)
    bb = jnp.concatenate([lru_ba, lru_bx], axis=-1).reshape(depth, 1, 2 * LRU_WIDTH)
    w_uqt = jnp.swapaxes(_pad_heads(mla_w_uq, MLA_NOPE + MLA_ROPE), 1, 2).astype(BF16)
    w_uk, w_uvt = _pack_w_ukv(mla_w_ukv)
    wl, wr, wm, wo = (a.astype(BF16) for a in (lru_wo, ret_wo, mla_wo, w_out))
    wu, wd = ffn_w_up.astype(BF16), ffn_w_down.astype(BF16)

    for l in range(depth):
        t = _prenorm_matmul(l, x, row(mix_pre_g), mod, 0, 1, w_in_p, "mixer_in_proj")
        y_lru = _lru(l, t, lru_conv_w, row(lru_conv_b), wbd, bb, row(lru_lambda))
        y_ret = _ret(t, cos_r, sin_r)
        qt, k, vt = _mla_proj(l, t, row(mla_q_norm_g), row(mla_kv_norm_g), w_uqt, w_uk, w_uvt,
                              cos_m, sin_m, cos_mt, sin_mt)
        y_mla = _flash(qt, k, vt)
        x = _merge(l, y_lru, y_ret, y_mla, t, x, wl, wr, wm, wo, row(mix_post_g), mod, 2)
        x = _ffn(l, x, row(ffn_pre_g), mod, 3, 4, 5, wu, ffn_conv_w, row(ffn_conv_b), wd, row(ffn_post_g))
    return x
```

```python
import functools

import jax
import jax.numpy as jnp
import numpy as np
from jax import lax
from jax.experimental import pallas as pl
from jax.experimental.pallas import tpu as pltpu

F32 = jnp.float32
BF16 = jnp.bfloat16

D_MODEL = 1024
DEPTH = 2
EPS = 1e-6
ROPE_THETA = 10000.0
LRU_WIDTH = 512
LRU_BLOCKS = 8
LRU_BLOCK = LRU_WIDTH // LRU_BLOCKS
LRU_CONV = 4
LRU_C = 8.0
RET_HEADS = 8
RET_DK = 64
RET_DV = 64
RET_CHUNK = 128
MLA_HEADS = 8
MLA_Q_RANK = 384
MLA_KV_RANK = 256
MLA_NOPE = 64
MLA_ROPE = 32
MLA_V = 64
D_FF = 2816
FFN_CONV = 3

LANES = 128
SUBLANES = 8
MXU_DIM = 256
VMEM_LIMIT = 56 * 1024 * 1024

TILES = dict(
    ada_cols=1536,
    rope_rows=2048,
    pack_rows=256,
    in_proj_rows=512, in_proj_cols=2304,
    lru_rows=1024,
    ret_rows=2048,
    mla_proj_rows=2048,
    flash_block=512,
    merge_rows=1024,
    ffn_rows=512, ffn_cols=MXU_DIM,
)
MASKED_SCORE = -1e30

COL_LRU = 0
COL_RET = 1024
COL_GATE = 3072
COL_MLA = 6144
N_IN_PACKED = 6912
MLA_PACK = 768
KR_LANE = 64
LOG2_E = 1.4426950408889634
FLASH_HEADS = 4
V_ROWS = MLA_V + 16


def _params(*sem):
    return pltpu.CompilerParams(dimension_semantics=sem, vmem_limit_bytes=VMEM_LIMIT)


def _of_layer(l, *tail, **kw):
    return pl.BlockSpec((None,) + tail, lambda *_: (l,) + (0,) * len(tail), **kw)


def _mod_spec(l, k, d):
    return pl.BlockSpec((None, None, None, 1, d), lambda b, *_: (l, b, k, 0, 0))


GELU_C1 = 0.7978845608028654
GELU_C2 = GELU_C1 * 0.044715


def _gelu_tanh(x):
    return 0.5 * x * (1.0 + jnp.tanh(x * (GELU_C1 + GELU_C2 * (x * x))))


def _sigmoid(x):
    return 1.0 / (1.0 + jnp.exp(-x))


def _rms(x, g):
    return x * lax.rsqrt(jnp.mean(x * x, axis=-1, keepdims=True) + EPS) * g


def _ada_kernel(c_ref, w_ref, b_ref, o_ref):
    c = c_ref[...]
    ca = c * _sigmoid(c)
    o_ref[...] = jnp.dot(ca, w_ref[...], preferred_element_type=F32,
                         precision=lax.Precision.HIGHEST) + b_ref[...]


def _ada(c, ada_w, ada_b):
    depth, d, n = ada_w.shape
    rows = SUBLANES
    c_pad = jnp.pad(c, ((0, rows - c.shape[0]), (0, 0)))
    tn = TILES["ada_cols"]
    return pl.pallas_call(
        _ada_kernel,
        grid=(depth, n // tn),
        in_specs=[
            pl.BlockSpec((rows, d), lambda l, j: (0, 0)),
            pl.BlockSpec((None, d, tn), lambda l, j: (l, 0, j)),
            pl.BlockSpec((None, 1, tn), lambda l, j: (l, 0, j)),
        ],
        out_specs=pl.BlockSpec((None, rows, tn), lambda l, j: (l, 0, j)),
        out_shape=jax.ShapeDtypeStruct((depth, rows, n), F32),
        compiler_params=_params("parallel", "parallel"),
        name="ada_mod",
    )(c_pad, ada_w, ada_b.reshape(depth, 1, n))


def _rope_kernel(pos_ref, inv_ref, cr_ref, sr_ref, cm_ref, sm_ref, cmt_ref, smt_ref):
    half_r = RET_DK // 2
    half_m = MLA_ROPE // 2
    half_l = LANES // 2
    rows = pos_ref.shape[0] // 2
    pos = pos_ref[...].astype(F32)
    lane = lax.broadcasted_iota(jnp.int32, (rows, LANES), 1)
    ang = jnp.where(lane < half_l, pos[:rows], pos[rows:]) * inv_ref[...]
    c2 = jnp.cos(ang)
    s2 = jnp.sin(ang)

    def unpack(x, lo, n):
        return jnp.concatenate([x[:, lo:lo + n], x[:, half_l + lo:half_l + lo + n]], axis=0)

    c_r, s_r = unpack(c2, 0, half_r), unpack(s2, 0, half_r)
    c_m, s_m = unpack(c2, half_r, half_m), unpack(s2, half_r, half_m)
    reps = LANES // RET_DK
    cr_ref[...] = jnp.concatenate([c_r, c_r] * reps, axis=1)
    sr_ref[...] = jnp.concatenate([-s_r, s_r] * reps, axis=1)
    ts = 2 * rows
    tail = LANES - MLA_NOPE - MLA_ROPE
    cm_ref[...] = jnp.concatenate([jnp.ones((ts, MLA_NOPE), F32), c_m, c_m, jnp.ones((ts, tail), F32)], axis=1)
    sm_ref[...] = jnp.concatenate([jnp.zeros((ts, MLA_NOPE), F32), -s_m, s_m, jnp.zeros((ts, tail), F32)], axis=1)
    c2t = c2.T
    s2t = s2.T
    cmt_ref[...] = jnp.concatenate([c2t[half_r:half_r + half_m], c2t[half_l + half_r:half_l + half_r + half_m]], axis=1)
    smt_ref[...] = jnp.concatenate([s2t[half_r:half_r + half_m], s2t[half_l + half_r:half_l + half_r + half_m]], axis=1)


def _rope_tables(positions):
    bsz, s = positions.shape
    ts = TILES["rope_rows"]
    inv_r =ROPE_THETA ** (-jnp.arange(0, RET_DK, 2, dtype=F32) / RET_DK)
    inv_m = ROPE_THETA ** (-jnp.arange(0, MLA_ROPE, 2, dtype=F32) / MLA_ROPE)
    inv_half = jnp.concatenate([inv_r, inv_m, jnp.zeros(LANES // 2 - inv_r.size - inv_m.size, F32)])
    inv = jnp.tile(inv_half, 2).reshape(1, LANES)
    tab = pl.BlockSpec((None, ts, LANES), lambda b, i: (b, i, 0))
    shp = jax.ShapeDtypeStruct((bsz, s, LANES), F32)
    tab_t = pl.BlockSpec((None, inv_m.size, ts), lambda b, i: (b, 0, i))
    shp_t = jax.ShapeDtypeStruct((bsz, inv_m.size, s), F32)
    return pl.pallas_call(
        _rope_kernel,
        grid=(bsz, s // ts),
        in_specs=[pl.BlockSpec((None, ts, 1), lambda b, i: (b, i, 0)), pl.BlockSpec((1, LANES), lambda b, i: (0, 0))],
        out_specs=[tab, tab, tab, tab, tab_t, tab_t],
        out_shape=[shp, shp, shp, shp, shp_t, shp_t],
        compiler_params=_params("parallel", "parallel"),
        name="rope_tables",
    )(positions.reshape(bsz, s, 1), inv)


def _prenorm_matmul_kernel(x_ref, g_ref, sh_ref, sc_ref, w_ref, o_ref, *, tn):
    h = (_rms(x_ref[...], g_ref[...]) * (1.0 + sc_ref[...]) + sh_ref[...]).astype(BF16)
    for j in range(w_ref.shape[1] // tn):
        cols = slice(j * tn, (j + 1) * tn)
        o_ref[:, cols] = jnp.dot(h, w_ref[:, cols], preferred_element_type=F32).astype(o_ref.dtype)


def _prenorm_matmul(l, x, g, mod, k_shift, k_scale, w, name):
    bsz, s, d = x.shape
    n = w.shape[-1]
    tm, tn = TILES["in_proj_rows"], TILES["in_proj_cols"]
    return pl.pallas_call(
        functools.partial(_prenorm_matmul_kernel, tn=tn),
        grid=(bsz, s // tm),
        in_specs=[
            pl.BlockSpec((None, tm, d), lambda b, i: (b, i, 0)),
            _of_layer(l, 1, d),
            _mod_spec(l, k_shift, d), _mod_spec(l, k_scale, d),
            _of_layer(l, d, n, pipeline_mode=pl.Buffered(1)),
        ],
        out_specs=pl.BlockSpec((None, tm, n), lambda b, i: (b, i, 0)),
        out_shape=jax.ShapeDtypeStruct((bsz, s, n), BF16),
        compiler_params=_params("parallel", "parallel"),
        name=name,
    )(x, g, mod, mod, w)


def _lru_kernel(t_ref, cw_ref, cb_ref, wbd_ref, bb_ref, lam_ref, o_ref, xbuf_ref, hc_ref, h_ref, *, ts):
    w = LRU_WIDTH

    @pl.when(pl.program_id(1) == 0)
    def _():
        xbuf_ref[0:SUBLANES, :] = jnp.zeros((SUBLANES, w), F32)
        hc_ref[...] = jnp.zeros((1, w), F32)

    xb = t_ref[:, 0:w].astype(F32)
    gb = t_ref[:, w:2 * w].astype(F32)
    xbuf_ref[SUBLANES:SUBLANES + ts, :] = xb
    xc = xb * cw_ref[LRU_CONV - 1:LRU_CONV, :] + cb_ref[...]
    for k in range(LRU_CONV - 1):
        back = LRU_CONV - 1 - k
        xc = xc + xbuf_ref[pl.ds(SUBLANES - back, ts), :] * cw_ref[k:k + 1, :]
    xbuf_ref[0:SUBLANES, :] = xb[ts - SUBLANES:, :]

    z = jnp.dot(xc.astype(BF16), wbd_ref[...], preferred_element_type=F32) + bb_ref[...]
    r = _sigmoid(z[:, :w])
    ig = _sigmoid(z[:, w:])
    nl = -lam_ref[...]
    softplus = jnp.maximum(nl, 0.0) + jnp.log1p(jnp.exp(-jnp.abs(nl)))
    log_a = (-LRU_C) * r * softplus
    a = jnp.exp(log_a)
    th = jnp.tanh(log_a)
    u = jnp.sqrt(-2.0 * th / (1.0 - th)) * (ig * xc)

    groups = ts // SUBLANES
    a = a.reshape(groups, SUBLANES, w)
    u = u.reshape(groups, SUBLANES, w)
    sub = lax.broadcasted_iota(jnp.int32, (groups, SUBLANES, w), 1)
    k = 1
    while k < SUBLANES:
        keep = sub >= k
        a_prev = jnp.where(keep, pltpu.roll(a, k, 1), 1.0)
        u_prev = jnp.where(keep, pltpu.roll(u, k, 1), 0.0)
        u = a * u_prev + u
        a = a * a_prev
        k *= 2
    h_prev = hc_ref[...]
    for r in range(groups):
        hb = a[r] * h_prev + u[r]
        h_ref[r * SUBLANES:(r + 1) * SUBLANES, :] = hb
        h_prev = hb[SUBLANES - 1:SUBLANES, :]
    hc_ref[...] = h_prev
    o_ref[...] = (h_ref[...] * _gelu_tanh(gb)).astype(o_ref.dtype)


def _lru(l, t, conv_w, conv_b, wbd, bb, lam):
    bsz, s, _ = t.shape
    ts = TILES["lru_rows"]
    w = LRU_WIDTH
    return pl.pallas_call(
        functools.partial(_lru_kernel, ts=ts),
        grid=(bsz, s // ts),
        in_specs=[
            pl.BlockSpec((None, ts, 2 * w), lambda b, i: (b, i, COL_LRU // (2 * w))),
            _of_layer(l, LRU_CONV, w), _of_layer(l, 1, w), _of_layer(l, w, 2 * w), _of_layer(l, 1, 2 * w),
            _of_layer(l, 1, w),
        ],
        out_specs=pl.BlockSpec((None, ts, w), lambda b, i: (b, i, 0)),
        out_shape=jax.ShapeDtypeStruct((bsz, s, w), BF16),
        scratch_shapes=[pltpu.VMEM((ts + SUBLANES, w), F32), pltpu.VMEM((1, w), F32), pltpu.VMEM((ts, w), F32)],
        compiler_params=_params("parallel", "arbitrary"),
        name="lru_mixer",
    )(t, conv_w, conv_b, wbd, bb, lam)


def _ret_kernel(q_ref, k_ref, v_ref, g_ref, cos_ref, sin_ref, dec_ref, qdec_ref, kvdec_ref, cdec_ref,
                bd_ref, o_ref, st_ref, *, n_chunks):
    c_len = RET_CHUNK
    pairs = RET_HEADS // 2

    @pl.when(pl.program_id(1) == 0)
    def _():
        st_ref[...] = jnp.zeros(st_ref.shape, F32)

    lane = lax.broadcasted_iota(jnp.int32, (c_len, LANES), 1)
    head0 = lane < RET_DK
    first_half = (lane % RET_DK) < (RET_DK // 2)
    inv_n = 1.0 / RET_DV

    def rope(x, cos, sin):
        swapped = jnp.where(first_half, pltpu.roll(x, LANES - RET_DK // 2, 1), pltpu.roll(x, RET_DK // 2, 1))
        return x * cos + swapped * sin

    def head_mean(x):
        m0 = jnp.sum(jnp.where(head0, x, 0.0), axis=-1, keepdims=True) * inv_n
        m1 = jnp.sum(jnp.where(head0, 0.0, x), axis=-1, keepdims=True) * inv_n
        return jnp.where(head0, m0, m1)

    nt = (((1,), (1,)), ((), ()))
    tn = (((0,), (0,)), ((), ()))
    for c in range(n_chunks):
        rows = slice(c * c_len, (c + 1) * c_len)
        cos = cos_ref[rows, :]
        sin = sin_ref[rows, :]
        for p in range(pairs):
            cols = slice(p * LANES, (p + 1) * LANES)
            q = rope(q_ref[rows, cols].astype(F32), cos, sin)
            k = rope(k_ref[rows, cols].astype(F32), cos, sin) * (RET_DK ** -0.5)
            v = v_ref[rows, cols]
            qb = q.astype(BF16)
            kb = k.astype(BF16)
            zero = jnp.zeros_like(qb)
            s0 = lax.dot_general(jnp.where(head0, qb, zero), kb, nt, preferred_element_type=F32)
            s1 = lax.dot_general(jnp.where(head0, zero, qb), kb, nt, preferred_element_type=F32)
            probs = jnp.concatenate([s0 * dec_ref[2 * p], s1 * dec_ref[2 * p + 1]], axis=1).astype(BF16)
            v2 = jnp.concatenate([jnp.where(head0, v, zero), jnp.where(head0, zero, v)], axis=0)
            y = jnp.dot(probs, v2, preferred_element_type=F32)
            state = st_ref[p]
            y = y + jnp.dot(qb, state.astype(BF16), preferred_element_type=F32) * qdec_ref[p]
            vd = (v.astype(F32) * kvdec_ref[p]).astype(BF16)
            kv = lax.dot_general(kb, vd, tn, preferred_element_type=F32)
            st_ref[p] = state * cdec_ref[p] + kv * bd_ref[...]

            d = y - head_mean(y)
            yn = d * lax.rsqrt(head_mean(d * d) + EPS)
            g = g_ref[rows, cols].astype(F32)
            o_ref[rows, cols] = (g * _sigmoid(g) * yn).astype(o_ref.dtype)


def _ret_constants():
    f32 = F32
    log_gamma = jnp.log1p(-(2.0 ** (-5.0 - jnp.arange(RET_HEADS, dtype=f32))))
    idx = jnp.arange(RET_CHUNK, dtype=f32)
    diff = idx[:, None] - idx[None, :]
    causal = diff >= 0
    inner = jnp.where(causal[None], jnp.exp(jnp.where(causal, diff, 0.0)[None] * log_gamma[:, None, None]), 0.0)
    kv_decay = jnp.exp((RET_CHUNK - 1.0 - idx)[None, :] * log_gamma[:, None])
    q_decay = jnp.exp((idx + 1.0)[:, None] * log_gamma[None, :])
    chunk_decay = jnp.exp(RET_CHUNK * log_gamma)
    pairs = RET_HEADS // 2

    def by_lane(per_head):
        rows = per_head.shape[0]
        return jnp.repeat(per_head.reshape(rows, pairs, 2), RET_DK, axis=2).reshape(rows, pairs, LANES).transpose(1, 0, 2)

    qdec = by_lane(q_decay)
    kvdec = by_lane(kv_decay.T)
    cdec = by_lane(chunk_decay[None, :])
    lane_head = jnp.arange(LANES) // RET_DK
    bd = (lane_head[:, None] == lane_head[None, :]).astype(f32)
    return inner, qdec, kvdec, cdec, bd


def _ret(t, cos_r, sin_r):
    bsz, s, _ = t.shape
    tc = TILES["ret_rows"]
    w = RET_HEADS * RET_DK
    inner, qdec, kvdec, cdec, bd = _ret_constants()
    base = COL_RET // w
    col = lambda j: pl.BlockSpec((None, tc, w), lambda b, i: (b, i, base + j))
    tab = pl.BlockSpec((None, tc, LANES), lambda b, i: (b, i, 0))
    const = lambda shape: pl.BlockSpec(shape, lambda b, i: (0,) * len(shape))
    return pl.pallas_call(
        functools.partial(_ret_kernel, n_chunks=tc // RET_CHUNK),
        grid=(bsz, s // tc),
        in_specs=[col(0), col(1), col(2), col(3), tab, tab,
                  const(inner.shape), const(qdec.shape), const(kvdec.shape), const(cdec.shape), const(bd.shape)],
        out_specs=pl.BlockSpec((None, tc, w), lambda b, i: (b, i, 0)),
        out_shape=jax.ShapeDtypeStruct((bsz, s, w), BF16),
        scratch_shapes=[pltpu.VMEM((RET_HEADS // 2, LANES, LANES), F32)],
        compiler_params=_params("parallel", "arbitrary"),
        name="ret_mixer",
    )(t, t, t, t, cos_r, sin_r, inner, qdec, kvdec, cdec, bd)


def _mla_proj_kernel(t_ref, gq_ref, gkv_ref, wqt_ref, wk_ref, wvt_ref, cos_ref, sin_ref, cost_ref, sint_ref,
                     qt_ref, k_ref, vt_ref):
    ts = t_ref.shape[0]
    half = MLA_ROPE // 2
    nt = (((1,), (1,)), ((), ()))
    ckv = t_ref[:, 0:MLA_KV_RANK].astype(F32)
    kr = t_ref[:, MLA_KV_RANK:MLA_KV_RANK + LANES].astype(F32)
    cq = t_ref[:, MLA_PACK - MLA_Q_RANK:MLA_PACK].astype(F32)
    cqn = _rms(cq, gq_ref[...]).astype(BF16)
    ckvn = _rms(ckv, gkv_ref[...]).astype(BF16)

    qt = lax.dot_general(wqt_ref[...], cqn, nt, preferred_element_type=F32)
    cos_t = cost_ref[...]
    sin_t = sint_ref[...]
    scale = (MLA_NOPE + MLA_ROPE) ** -0.5 * LOG2_E
    for h in range(MLA_HEADS):
        base = h * LANES
        x1 = qt[base + MLA_NOPE:base + MLA_NOPE + half]
        x2 = qt[base + MLA_NOPE + half:base + MLA_NOPE + MLA_ROPE]
        blk = jnp.concatenate([qt[base:base + MLA_NOPE], x1 * cos_t - x2 * sin_t, x1 * sin_t + x2 * cos_t,
                               qt[base + MLA_NOPE + MLA_ROPE:base + LANES]], axis=0)
        qt_ref[base:base + LANES, :] = (blk * scale).astype(qt_ref.dtype)

    lane = lax.broadcasted_iota(jnp.int32, (ts, LANES), 1)
    swapped = jnp.where(lane < MLA_NOPE + half, pltpu.roll(kr, LANES - half, 1), pltpu.roll(kr, half, 1))
    k_rope = kr * cos_ref[...] + swapped * sin_ref[...]
    kn = jnp.dot(ckvn, wk_ref[...], preferred_element_type=F32)
    for h in range(MLA_HEADS):
        cols = slice(h * LANES, (h + 1) * LANES)
        k_ref[:, cols] = (kn[:, cols] + k_rope).astype(k_ref.dtype)

    vt = lax.dot_general(wvt_ref[...], ckvn, nt, preferred_element_type=F32)
    row = lax.broadcasted_iota(jnp.int32, vt.shape, 0)
    vt_ref[...] = jnp.where(row % LANES == MLA_V, 1.0, vt).astype(vt_ref.dtype)


def _mla_proj(l, t, gq, gkv, wqt, wk, wvt, cos_m, sin_m, cos_mt, sin_mt):
    bsz, s, _ = t.shape
    ts = TILES["mla_proj_rows"]
    hq = MLA_HEADS * LANES
    half = MLA_ROPE // 2
    tab = pl.BlockSpec((None, ts, LANES), lambda b, i: (b, i, 0))
    tab_t = pl.BlockSpec((None, half, ts), lambda b, i: (b, 0, i))
    rowmajor = pl.BlockSpec((None, ts, hq), lambda b, i: (b, i, 0))
    transposed = pl.BlockSpec((None, hq, ts), lambda b, i: (b, 0, i))
    return pl.pallas_call(
        _mla_proj_kernel,
        grid=(bsz, s // ts),
        in_specs=[pl.BlockSpec((None, ts, MLA_PACK), lambda b, i: (b, i, COL_MLA // MLA_PACK)),
                  _of_layer(l, 1, MLA_Q_RANK), _of_layer(l, 1, MLA_KV_RANK), _of_layer(l, *wqt.shape[1:]),
                  _of_layer(l, *wk.shape[1:]), _of_layer(l, *wvt.shape[1:]), tab, tab, tab_t, tab_t],
        out_specs=[transposed, rowmajor, transposed],
        out_shape=[jax.ShapeDtypeStruct((bsz, hq, s), BF16), jax.ShapeDtypeStruct((bsz, s, hq), BF16),
                   jax.ShapeDtypeStruct((bsz, hq, s), BF16)],
        compiler_params=_params("parallel", "parallel"),
        name="mla_proj",
    )(t, gq, gkv, wqt, wk, wvt, cos_m, sin_m, cos_mt, sin_mt)


def _flash_kernel(qt_ref, k_ref, vt_ref, o_ref, acc_ref, sa_ref, sb_ref, *, tq):
    g = pl.program_id(2)
    acc_ref[...] = jnp.zeros(acc_ref.shape, F32)

    def qk(j, dst_ref, which, masked):
        off = pl.multiple_of(j * tq, tq)
        for h in range(FLASH_HEADS):
            rows = slice(h * LANES, (h + 1) * LANES)
            s = jnp.dot(k_ref[pl.ds(off, tq), rows], qt_ref[rows, which * tq:(which + 1) * tq],
                        preferred_element_type=F32)
            if masked:
                key = lax.broadcasted_iota(jnp.int32, (tq, tq), 0)
                qry = lax.broadcasted_iota(jnp.int32, (tq, tq), 1)
                s = jnp.where(key <= qry, s, MASKED_SCORE)
            dst_ref[h] = s

    def softmax_pv(j, src_ref, which, ms):
        off = pl.multiple_of(j * tq, tq)
        new_m = []
        for h in range(FLASH_HEADS):
            s = src_ref[h]
            m_new = jnp.maximum(ms[h], jnp.max(s, axis=0, keepdims=True))
            alpha = jnp.exp2(ms[h] - m_new)
            p = jnp.exp2(s - m_new).astype(BF16)
            vt = vt_ref[h * LANES:h * LANES + V_ROWS, pl.ds(off, tq)]
            acc_ref[which, h] = alpha * acc_ref[which, h] + jnp.dot(vt, p, preferred_element_type=F32)
            new_m.append(m_new)
        return tuple(new_m)

    def sweep(which, diag, first_ref, other_ref):
        def pair(jj, ms):
            qk(2 * jj, other_ref, which, False)
            ms = softmax_pv(jnp.where(jj == 0, diag, 2 * jj - 1), first_ref, which, ms)
            qk(2 * jj + 1, first_ref, which, False)
            return softmax_pv(2 * jj, other_ref, which, ms)

        m0 = jnp.full((1, tq), MASKED_SCORE, F32)
        ms = lax.fori_loop(0, g, pair, (m0,) * FLASH_HEADS)
        return ms, jnp.where(g == 0, diag, 2 * g - 1)

    def write_out(which):
        for c in range(FLASH_HEADS // 2):
            outs = []
            for h in (2 * c, 2 * c + 1):
                acc = acc_ref[which, h]
                outs.append(acc[0:MLA_V, :] / acc[MLA_V:MLA_V + 1, :])
            o_ref[which * tq:(which + 1) * tq, c * LANES:(c + 1) * LANES] = (
                jnp.concatenate(outs, axis=0).T.astype(o_ref.dtype))

    qk(2 * g, sa_ref, 0, True)
    ms, left = sweep(0, 2 * g, sa_ref, sb_ref)
    qk(2 * g + 1, sb_ref, 1, True)
    softmax_pv(left, sa_ref, 0, ms)
    write_out(0)
    ms, left = sweep(1, 2 * g + 1, sb_ref, sa_ref)
    qk(2 * g, sa_ref, 1, False)
    ms = softmax_pv(left, sb_ref, 1, ms)
    softmax_pv(2 * g, sa_ref, 1, ms)
    write_out(1)


def _flash(qt, k, vt):
    bsz, s, _ = k.shape
    tq = TILES["flash_block"]
    hs = FLASH_HEADS
    return pl.pallas_call(
        functools.partial(_flash_kernel, tq=tq),
        grid=(bsz, MLA_HEADS // hs, s // (2 * tq)),
        in_specs=[
            pl.BlockSpec((None, hs * LANES, 2 * tq), lambda b, p, g: (b, p, g)),
            pl.BlockSpec((None, s, hs * LANES), lambda b, p, g: (b, 0, p)),
            pl.BlockSpec((None, hs * LANES, s), lambda b, p, g: (b, p, 0)),
        ],
        out_specs=pl.BlockSpec((None, 2 * tq, hs * MLA_V), lambda b, p, g: (b, g, p)),
        out_shape=jax.ShapeDtypeStruct((bsz, s, MLA_HEADS * MLA_V), BF16),
        scratch_shapes=[pltpu.VMEM((2, hs, V_ROWS, tq), F32), pltpu.VMEM((hs, tq, tq), F32),
                        pltpu.VMEM((hs, tq, tq), F32)],
        compiler_params=_params("parallel", "parallel", "arbitrary"),
        name="mla_flash",
    )(qt, k, vt)


def _merge_kernel(yl_ref, yr_ref, ym_ref, gt_ref, x_ref, wl_ref, wr_ref, wm_ref, wo_ref, gp_ref, gm_ref, o_ref):
    d = D_MODEL
    merged = _sigmoid(gt_ref[:, 0:d].astype(F32)) * jnp.dot(yl_ref[...], wl_ref[...], preferred_element_type=F32)
    merged = merged + _sigmoid(gt_ref[:, d:2 * d].astype(F32)) * jnp.dot(
        yr_ref[...], wr_ref[...], preferred_element_type=F32)
    merged = merged + _sigmoid(gt_ref[:, 2 * d:3 * d].astype(F32)) * jnp.dot(
        ym_ref[...], wm_ref[...], preferred_element_type=F32)
    y = jnp.dot(merged.astype(BF16), wo_ref[...], preferred_element_type=F32)
    o_ref[...] = x_ref[...] + gm_ref[...] * _rms(y, gp_ref[...])


def _merge(l, y_lru, y_ret, y_mla, t, x, wl, wr, wm, wo, g_post, mod, k_gate):
    bsz, s, d = x.shape
    tm = TILES["merge_rows"]
    w = y_lru.shape[-1]
    br = pl.BlockSpec((None, tm, w), lambda b, i: (b, i, 0))
    return pl.pallas_call(
        _merge_kernel,
        grid=(bsz, s // tm),
        in_specs=[br, br, br,
                  pl.BlockSpec((None, tm, 3 * d), lambda b, i: (b, i, COL_GATE // (3 * d))),
                  pl.BlockSpec((None, tm, d), lambda b, i: (b, i, 0)),
                  _of_layer(l, w, d), _of_layer(l, w, d), _of_layer(l, w, d), _of_layer(l, d, d),
                  _of_layer(l, 1, d), _mod_spec(l, k_gate, d)],
        out_specs=pl.BlockSpec((None, tm, d), lambda b, i: (b, i, 0)),
        out_shape=jax.ShapeDtypeStruct((bsz, s, d), F32),
        compiler_params=_params("parallel", "parallel"),
        name="mixer_merge",
    )(y_lru, y_ret, y_mla, t, x, wl, wr, wm, wo, g_post, mod)


def _ffn_kernel(x_ref, gpre_ref, sh_ref, sc_ref, wu_ref, cw_ref, cb_ref, wd_ref, gp_ref, gf_ref, o_ref,
                act_ref, halo_ref, *, tm, cw):
    @pl.when(pl.program_id(1) == 0)
    def _():
        halo_ref[...] = jnp.zeros(halo_ref.shape, F32)

    row = lax.broadcasted_iota(jnp.int32, (SUBLANES, cw), 0)
    x = x_ref[...]
    h = (_rms(x, gpre_ref[...]) * (1.0 + sc_ref[...]) + sh_ref[...]).astype(BF16)

    def conv(c0, gain):
        cols = slice(c0, c0 + cw)
        taps = cw_ref[:, cols] * gain
        bias = cb_ref[:, cols] * gain
        xv = jnp.dot(h, wu_ref[:, cols], preferred_element_type=F32)
        h1 = halo_ref[SUBLANES - 1:SUBLANES, cols]
        h2 = halo_ref[SUBLANES - 2:SUBLANES - 1, cols]
        halo_ref[:, cols] = xv[tm - SUBLANES:, :]
        r1 = pltpu.roll(xv, 1, 0)
        r2 = pltpu.roll(xv, 2, 0)
        xm1 = jnp.concatenate([jnp.where(row == 0, h1, r1[:SUBLANES]), r1[SUBLANES:]], axis=0)
        top2 = jnp.where(row == 0, h2, jnp.where(row == 1, h1, r2[:SUBLANES]))
        xm2 = jnp.concatenate([top2, r2[SUBLANES:]], axis=0)
        y = xv * taps[2:3] + bias
        y = y + xm2 * taps[0:1]
        return y + xm1 * taps[1:2]

    for c in range(D_FF // cw):
        hu = conv(c * cw, 0.5)
        g = conv(D_FF + c * cw, 1.0)
        th = jnp.tanh(g * (GELU_C1 + GELU_C2 * (g * g)))
        act_ref[:, c * cw:(c + 1) * cw] = ((hu * g) * (1.0 + th)).astype(BF16)
    y = jnp.dot(act_ref[...], wd_ref[...], preferred_element_type=F32)
    o_ref[...] = x + gf_ref[...] * _rms(y, gp_ref[...])


def _ffn(l, x, g_pre, mod, k_shift, k_scale, k_gate, wu, conv_w, conv_b, wd, g_post):
    bsz, s, d = x.shape
    tm = TILES["ffn_rows"]
    n = wu.shape[-1]
    single = dict(pipeline_mode=pl.Buffered(1))
    return pl.pallas_call(
        functools.partial(_ffn_kernel, tm=tm, cw=TILES["ffn_cols"]),
        grid=(bsz, s // tm),
        in_specs=[pl.BlockSpec((None, tm, d), lambda b, i: (b, i, 0)),
                  _of_layer(l, 1, d), _mod_spec(l, k_shift, d), _mod_spec(l, k_scale, d),
                  _of_layer(l, d, n, **single), _of_layer(l, FFN_CONV, n), _of_layer(l, 1, n),
                  _of_layer(l, D_FF, d, **single), _of_layer(l, 1, d), _mod_spec(l, k_gate, d)],
        out_specs=pl.BlockSpec((None, tm, d), lambda b, i: (b, i, 0)),
        out_shape=jax.ShapeDtypeStruct((bsz, s, d), F32),
        scratch_shapes=[pltpu.VMEM((tm, D_FF), BF16), pltpu.VMEM((SUBLANES, n), F32)],
        compiler_params=_params("parallel", "arbitrary"),
        name="ffn_fused",
    )(x, g_pre, mod, mod, wu, conv_w, conv_b, wd, g_post, mod)


def _pack_w_in_kernel(w_ref, o_ref, *, n_all):
    o_cq = 2 * LRU_WIDTH + 4 * RET_HEADS * RET_DK
    o_ckv = o_cq + MLA_Q_RANK
    o_kr = o_ckv + MLA_KV_RANK
    o_gate = o_kr + MLA_ROPE
    n_gate = n_all - o_gate
    rows = w_ref.shape[0]
    o_ref[:, 0:o_cq] = w_ref[:, 0:o_cq]
    o_ref[:, COL_MLA:COL_MLA + MLA_KV_RANK] = w_ref[:, o_ckv:o_kr]
    o_ref[:, N_IN_PACKED - MLA_Q_RANK:N_IN_PACKED] = w_ref[:, o_cq:o_ckv]
    tail = w_ref[:, o_kr:w_ref.shape[1]].astype(F32)
    o_ref[:, COL_GATE:COL_GATE + n_gate] = tail[:, MLA_ROPE:MLA_ROPE + n_gate].astype(BF16)
    kr = jnp.concatenate([jnp.zeros((rows, KR_LANE), F32), tail[:, :MLA_ROPE],
                          jnp.zeros((rows, LANES - KR_LANE - MLA_ROPE), F32)], axis=1)
    o_ref[:, COL_MLA + MLA_KV_RANK:COL_MLA + MLA_KV_RANK + LANES] = kr.astype(BF16)


def _pack_w_in(w, n_all):
    depth, d, n = w.shape
    tr = TILES["pack_rows"]
    return pl.pallas_call(
        functools.partial(_pack_w_in_kernel, n_all=n_all),
        grid=(depth, d // tr),
        in_specs=[pl.BlockSpec((None, tr, n), lambda l, i: (l, i, 0))],
        out_specs=pl.BlockSpec((None, tr, N_IN_PACKED), lambda l, i: (l, i, 0)),
        out_shape=jax.ShapeDtypeStruct((depth, d, N_IN_PACKED), BF16),
        compiler_params=_params("parallel", "parallel"),
        name="pack_w_in",
    )(w)


def _pad_heads(w, width):
    depth, r, _ = w.shape
    w4 = w.reshape(depth, r, MLA_HEADS, width)
    return jnp.pad(w4, ((0, 0), (0, 0), (0, 0), (0, LANES - width))).reshape(depth, r, MLA_HEADS * LANES)


def _pack_w_ukv(w):
    depth, r, _ = w.shape
    w4 = w.reshape(depth, r, MLA_HEADS, MLA_NOPE + MLA_V)
    k = _pad_heads(w4[..., :MLA_NOPE].reshape(depth, r, -1), MLA_NOPE)
    v = _pad_heads(w4[..., MLA_NOPE:].reshape(depth, r, -1), MLA_V)
    return k.astype(BF16), jnp.swapaxes(v, 1, 2).astype(BF16)


def _block_diag(w):
    depth, nb, n, _ = w.shape
    eye = jnp.eye(nb, dtype=w.dtype)
    return (eye[None, :, None, :, None] * w[:, :, :, None, :]).reshape(depth, nb * n, nb * n)


def kernel(x, c, positions, ada_w, ada_b, mix_pre_g, mix_post_g, w_in, lru_conv_w, lru_conv_b, lru_wa, lru_ba, lru_wx, lru_bx, lru_lambda, lru_wo, ret_wo, mla_q_norm_g, mla_w_uq, mla_kv_norm_g, mla_w_ukv, mla_wo, w_out, ffn_pre_g, ffn_post_g, ffn_w_up, ffn_conv_w, ffn_conv_b, ffn_w_down):
    bsz, s, d = x.shape
    depth = w_in.shape[0]
    row = lambda a: a.reshape(depth, 1, a.shape[-1])
    mod = _ada(c, ada_w, ada_b).reshape(depth, SUBLANES, 6, 1, d)
    cos_r, sin_r, cos_m, sin_m, cos_mt, sin_mt = _rope_tables(positions)

    n_in = w_in.shape[-1]
    w_in_p = _pack_w_in(jnp.concatenate(
        [w_in.astype(BF16), jnp.zeros((depth, d, N_IN_PACKED - n_in), BF16)], axis=-1), n_in)
    wbd = jnp.concatenate([_block_diag(lru_wa), _block_diag(lru_wx)], axis=-1).astype(BF16)
    bb = jnp.concatenate([lru_ba, lru_bx], axis=-1).reshape(depth, 1, 2 * LRU_WIDTH)
    w_uqt = jnp.swapaxes(_pad_heads(mla_w_uq, MLA_NOPE + MLA_ROPE), 1, 2).astype(BF16)
    w_uk, w_uvt = _pack_w_ukv(mla_w_ukv)
    wl, wr, wm, wo = (a.astype(BF16) for a in (lru_wo, ret_wo, mla_wo, w_out))
    wu, wd = ffn_w_up.astype(BF16), ffn_w_down.astype(BF16)

    for l in range(depth):
        t = _prenorm_matmul(l, x, row(mix_pre_g), mod, 0, 1, w_in_p, "mixer_in_proj")
        y_lru = _lru(l, t, lru_conv_w, row(lru_conv_b), wbd, bb, row(lru_lambda))
        y_ret = _ret(t, cos_r, sin_r)
        qt, k, vt = _mla_proj(l, t, row(mla_q_norm_g), row(mla_kv_norm_g), w_uqt, w_uk, w_uvt,
                              cos_m, sin_m, cos_mt, sin_mt)
        y_mla = _flash(qt, k, vt)
        x = _merge(l, y_lru, y_ret, y_mla, t, x, wl, wr, wm, wo, row(mix_post_g), mod, 2)
        x = _ffn(l, x, row(ffn_pre_g), mod, 3, 4, 5, wu, ffn_conv_w, row(ffn_conv_b), wd, row(ffn_post_g))
    return x
```

```python
import functools

import jax
import jax.numpy as jnp
from jax import lax
from jax.experimental import pallas as pl
from jax.experimental.pallas import tpu as pltpu

F32 = jnp.float32
BF16 = jnp.bfloat16

D_MODEL = 1024
EPS = 1e-6
ROPE_THETA = 10000.0
LRU_WIDTH = 512
LRU_CONV = 4
LRU_C = 8.0
RET_HEADS = 8
RET_DK = 64
RET_DV = 64
RET_CHUNK = 128
MLA_HEADS = 8
MLA_Q_RANK = 384
MLA_KV_RANK = 256
MLA_NOPE = 64
MLA_ROPE = 32
MLA_V = 64
D_FF = 2816
FFN_CONV = 3

LANES = 128
SUBLANES = 8
MXU_DIM = 256
VMEM_LIMIT = 56 * 1024 * 1024

TILES = dict(
    ada_cols=1536,
    rope_rows=2048,
    pack_rows=256,
    in_proj_rows=512, in_proj_cols=2304,
    lru_rows=1024,
    ret_rows=2048,
    mla_proj_rows=2048,
    flash_block=512,
    merge_rows=1024,
    ffn_rows=1024, ffn_cols=MXU_DIM,
)
MASKED_SCORE = -1e30

COL_LRU = 0
COL_RET = 1024
COL_GATE = 3072
COL_MLA = 6144
N_IN_PACKED = 6912
MLA_PACK = 768
KR_LANE = 64
LOG2_E = 1.4426950408889634
FLASH_HEADS = 4
V_ROWS = MLA_V + 16


def _params(*sem):
    return pltpu.CompilerParams(dimension_semantics=sem, vmem_limit_bytes=VMEM_LIMIT)


def _of_layer(l, *tail, **kw):
    return pl.BlockSpec((None,) + tail, lambda *_: (l,) + (0,) * len(tail), **kw)


def _mod_spec(l, k, d):
    return pl.BlockSpec((None, None, None, 1, d), lambda b, *_: (l, b, k, 0, 0))


GELU_C1 = 0.7978845608028654
GELU_C2 = GELU_C1 * 0.044715


def _gelu_tanh(x):
    return 0.5 * x * (1.0 + jnp.tanh(x * (GELU_C1 + GELU_C2 * (x * x))))


def _sigmoid(x):
    return 1.0 / (1.0 + jnp.exp(-x))


def _rms(x, g):
    return x * lax.rsqrt(jnp.mean(x * x, axis=-1, keepdims=True) + EPS) * g


def _ada_kernel(c_ref, w_ref, b_ref, o_ref):
    c = c_ref[...]
    ca = c * _sigmoid(c)
    o_ref[...] = jnp.dot(ca, w_ref[...], preferred_element_type=F32,
                         precision=lax.Precision.HIGHEST) + b_ref[...]


def _ada(c, ada_w, ada_b):
    depth, d, n = ada_w.shape
    rows = SUBLANES
    c_pad = jnp.pad(c, ((0, rows - c.shape[0]), (0, 0)))
    tn = TILES["ada_cols"]
    return pl.pallas_call(
        _ada_kernel,
        grid=(depth, n // tn),
        in_specs=[
            pl.BlockSpec((rows, d), lambda l, j: (0, 0)),
            pl.BlockSpec((None, d, tn), lambda l, j: (l, 0, j)),
            pl.BlockSpec((None, 1, tn), lambda l, j: (l, 0, j)),
        ],
        out_specs=pl.BlockSpec((None, rows, tn), lambda l, j: (l, 0, j)),
        out_shape=jax.ShapeDtypeStruct((depth, rows, n), F32),
        compiler_params=_params("parallel", "parallel"),
        name="ada_mod",
    )(c_pad, ada_w, ada_b.reshape(depth, 1, n))


def _rope_kernel(pos_ref, inv_ref, cr_ref, sr_ref, cm_ref, sm_ref, cmt_ref, smt_ref):
    half_r = RET_DK // 2
    half_m = MLA_ROPE // 2
    half_l = LANES // 2
    rows = pos_ref.shape[0] // 2
    pos = pos_ref[...].astype(F32)
    lane = lax.broadcasted_iota(jnp.int32, (rows, LANES), 1)
    ang = jnp.where(lane < half_l, pos[:rows], pos[rows:]) * inv_ref[...]
    c2 = jnp.cos(ang)
    s2 = jnp.sin(ang)

    def unpack(x, lo, n):
        return jnp.concatenate([x[:, lo:lo + n], x[:, half_l + lo:half_l + lo + n]], axis=0)

    c_r, s_r = unpack(c2, 0, half_r), unpack(s2, 0, half_r)
    c_m, s_m = unpack(c2, half_r, half_m), unpack(s2, half_r, half_m)
    reps = LANES // RET_DK
    cr_ref[...] = jnp.concatenate([c_r, c_r] * reps, axis=1)
    sr_ref[...] = jnp.concatenate([-s_r, s_r] * reps, axis=1)
    ts = 2 * rows
    tail = LANES - MLA_NOPE - MLA_ROPE
    cm_ref[...] = jnp.concatenate([jnp.ones((ts, MLA_NOPE), F32), c_m, c_m, jnp.ones((ts, tail), F32)], axis=1)
    sm_ref[...] = jnp.concatenate([jnp.zeros((ts, MLA_NOPE), F32), -s_m, s_m, jnp.zeros((ts, tail), F32)], axis=1)
    c2t = c2.T
    s2t = s2.T
    cmt_ref[...] = jnp.concatenate([c2t[half_r:half_r + half_m], c2t[half_l + half_r:half_l + half_r + half_m]], axis=1)
    smt_ref[...] = jnp.concatenate([s2t[half_r:half_r + half_m], s2t[half_l + half_r:half_l + half_r + half_m]], axis=1)


def _rope_tables(positions):
    bsz, s = positions.shape
    ts = TILES["rope_rows"]
    inv_r =ROPE_THETA ** (-jnp.arange(0, RET_DK, 2, dtype=F32) / RET_DK)
    inv_m = ROPE_THETA ** (-jnp.arange(0, MLA_ROPE, 2, dtype=F32) / MLA_ROPE)
    inv_half = jnp.concatenate([inv_r, inv_m, jnp.zeros(LANES // 2 - inv_r.size - inv_m.size, F32)])
    inv = jnp.tile(inv_half, 2).reshape(1, LANES)
    tab = pl.BlockSpec((None, ts, LANES), lambda b, i: (b, i, 0))
    shp = jax.ShapeDtypeStruct((bsz, s, LANES), F32)
    tab_t = pl.BlockSpec((None, inv_m.size, ts), lambda b, i: (b, 0, i))
    shp_t = jax.ShapeDtypeStruct((bsz, inv_m.size, s), F32)
    return pl.pallas_call(
        _rope_kernel,
        grid=(bsz, s // ts),
        in_specs=[pl.BlockSpec((None, ts, 1), lambda b, i: (b, i, 0)), pl.BlockSpec((1, LANES), lambda b, i: (0, 0))],
        out_specs=[tab, tab, tab, tab, tab_t, tab_t],
        out_shape=[shp, shp, shp, shp, shp_t, shp_t],
        compiler_params=_params("parallel", "parallel"),
        name="rope_tables",
    )(positions.reshape(bsz, s, 1), inv)


def _prenorm_matmul_kernel(x_ref, g_ref, sh_ref, sc_ref, w_ref, o_ref, *, tn):
    h = (_rms(x_ref[...], g_ref[...]) * (1.0 + sc_ref[...]) + sh_ref[...]).astype(BF16)
    for j in range(w_ref.shape[1] // tn):
        cols = slice(j * tn, (j + 1) * tn)
        o_ref[:, cols] = jnp.dot(h, w_ref[:, cols], preferred_element_type=F32).astype(o_ref.dtype)


def _prenorm_matmul(l, x, g, mod, k_shift, k_scale, w, name):
    bsz, s, d = x.shape
    n = w.shape[-1]
    tm, tn = TILES["in_proj_rows"], TILES["in_proj_cols"]
    return pl.pallas_call(
        functools.partial(_prenorm_matmul_kernel, tn=tn),
        grid=(bsz, s // tm),
        in_specs=[
            pl.BlockSpec((None, tm, d), lambda b, i: (b, i, 0)),
            _of_layer(l, 1, d),
            _mod_spec(l, k_shift, d), _mod_spec(l, k_scale, d),
            _of_layer(l, d, n, pipeline_mode=pl.Buffered(1)),
        ],
        out_specs=pl.BlockSpec((None, tm, n), lambda b, i: (b, i, 0)),
        out_shape=jax.ShapeDtypeStruct((bsz, s, n), BF16),
        compiler_params=_params("parallel", "parallel"),
        name=name,
    )(x, g, mod, mod, w)


def _lru_kernel(t_ref, cw_ref, cb_ref, wbd_ref, bb_ref, lam_ref, o_ref, xbuf_ref, hc_ref, h_ref, *, ts):
    w = LRU_WIDTH

    @pl.when(pl.program_id(1) == 0)
    def _():
        xbuf_ref[0:SUBLANES, :] = jnp.zeros((SUBLANES, w), F32)
        hc_ref[...] = jnp.zeros((1, w), F32)

    xb = t_ref[:, 0:w].astype(F32)
    gb = t_ref[:, w:2 * w].astype(F32)
    xbuf_ref[SUBLANES:SUBLANES + ts, :] = xb
    xc = xb * cw_ref[LRU_CONV - 1:LRU_CONV, :] + cb_ref[...]
    for k in range(LRU_CONV - 1):
        back = LRU_CONV - 1 - k
        xc = xc + xbuf_ref[pl.ds(SUBLANES - back, ts), :] * cw_ref[k:k + 1, :]
    xbuf_ref[0:SUBLANES, :] = xb[ts - SUBLANES:, :]

    z = jnp.dot(xc.astype(BF16), wbd_ref[...], preferred_element_type=F32) + bb_ref[...]
    r = _sigmoid(z[:, :w])
    ig = _sigmoid(z[:, w:])
    nl = -lam_ref[...]
    softplus = jnp.maximum(nl, 0.0) + jnp.log1p(jnp.exp(-jnp.abs(nl)))
    log_a = (-LRU_C) * r * softplus
    a = jnp.exp(log_a)
    th = jnp.tanh(log_a)
    u = jnp.sqrt(-2.0 * th / (1.0 - th)) * (ig * xc)

    groups = ts // SUBLANES
    a = a.reshape(groups, SUBLANES, w)
    u = u.reshape(groups, SUBLANES, w)
    sub = lax.broadcasted_iota(jnp.int32, (groups, SUBLANES, w), 1)
    k = 1
    while k < SUBLANES:
        keep = sub >= k
        a_prev = jnp.where(keep, pltpu.roll(a, k, 1), 1.0)
        u_prev = jnp.where(keep, pltpu.roll(u, k, 1), 0.0)
        u = a * u_prev + u
        a = a * a_prev
        k *= 2
    h_prev = hc_ref[...]
    for r in range(groups):
        hb = a[r] * h_prev + u[r]
        h_ref[r * SUBLANES:(r + 1) * SUBLANES, :] = hb
        h_prev = hb[SUBLANES - 1:SUBLANES, :]
    hc_ref[...] = h_prev
    o_ref[...] = (h_ref[...] * _gelu_tanh(gb)).astype(o_ref.dtype)


def _lru(l, t, conv_w, conv_b, wbd, bb, lam):
    bsz, s, _ = t.shape
    ts = TILES["lru_rows"]
    w = LRU_WIDTH
    return pl.pallas_call(
        functools.partial(_lru_kernel, ts=ts),
        grid=(bsz, s // ts),
        in_specs=[
            pl.BlockSpec((None, ts, 2 * w), lambda b, i: (b, i, COL_LRU // (2 * w))),
            _of_layer(l, LRU_CONV, w), _of_layer(l, 1, w), _of_layer(l, w, 2 * w), _of_layer(l, 1, 2 * w),
            _of_layer(l, 1, w),
        ],
        out_specs=pl.BlockSpec((None, ts, w), lambda b, i: (b, i, 0)),
        out_shape=jax.ShapeDtypeStruct((bsz, s, w), BF16),
        scratch_shapes=[pltpu.VMEM((ts + SUBLANES, w), F32), pltpu.VMEM((1, w), F32), pltpu.VMEM((ts, w), F32)],
        compiler_params=_params("parallel", "arbitrary"),
        name="lru_mixer",
    )(t, conv_w, conv_b, wbd, bb, lam)


def _ret_kernel(q_ref, k_ref, v_ref, g_ref, cos_ref, sin_ref, dec_ref, qdec_ref, kvdec_ref, cdec_ref,
                bd_ref, o_ref, st_ref, *, n_chunks):
    c_len = RET_CHUNK
    pairs = RET_HEADS // 2

    @pl.when(pl.program_id(1) == 0)
    def _():
        st_ref[...] = jnp.zeros(st_ref.shape, F32)

    lane = lax.broadcasted_iota(jnp.int32, (c_len, LANES), 1)
    head0 = lane < RET_DK
    first_half = (lane % RET_DK) < (RET_DK // 2)
    inv_n = 1.0 / RET_DV

    def rope(x, cos, sin):
        swapped = jnp.where(first_half, pltpu.roll(x, LANES - RET_DK // 2, 1), pltpu.roll(x, RET_DK // 2, 1))
        return x * cos + swapped * sin

    def head_mean(x):
        m0 = jnp.sum(jnp.where(head0, x, 0.0), axis=-1, keepdims=True) * inv_n
        m1 = jnp.sum(jnp.where(head0, 0.0, x), axis=-1, keepdims=True) * inv_n
        return jnp.where(head0, m0, m1)

    nt = (((1,), (1,)), ((), ()))
    tn = (((0,), (0,)), ((), ()))
    for c in range(n_chunks):
        rows = slice(c * c_len, (c + 1) * c_len)
        cos = cos_ref[rows, :]
        sin = sin_ref[rows, :]
        for p in range(pairs):
            cols = slice(p * LANES, (p + 1) * LANES)
            q = rope(q_ref[rows, cols].astype(F32), cos, sin)
            k = rope(k_ref[rows, cols].astype(F32), cos, sin) * (RET_DK ** -0.5)
            v = v_ref[rows, cols]
            qb = q.astype(BF16)
            kb = k.astype(BF16)
            zero = jnp.zeros_like(qb)
            s0 = lax.dot_general(jnp.where(head0, qb, zero), kb, nt, preferred_element_type=F32)
            s1 = lax.dot_general(jnp.where(head0, zero, qb), kb, nt, preferred_element_type=F32)
            probs = jnp.concatenate([s0 * dec_ref[2 * p], s1 * dec_ref[2 * p + 1]], axis=1).astype(BF16)
            v2 = jnp.concatenate([jnp.where(head0, v, zero), jnp.where(head0, zero, v)], axis=0)
            y = jnp.dot(probs, v2, preferred_element_type=F32)
            state = st_ref[p]
            y = y + jnp.dot(qb, state.astype(BF16), preferred_element_type=F32) * qdec_ref[p]
            vd = (v.astype(F32) * kvdec_ref[p]).astype(BF16)
            kv = lax.dot_general(kb, vd, tn, preferred_element_type=F32)
            st_ref[p] = state * cdec_ref[p] + kv * bd_ref[...]

            d = y - head_mean(y)
            yn = d * lax.rsqrt(head_mean(d * d) + EPS)
            g = g_ref[rows, cols].astype(F32)
            o_ref[rows, cols] = (g * _sigmoid(g) * yn).astype(o_ref.dtype)


def _ret_constants():
    f32 = F32
    log_gamma = jnp.log1p(-(2.0 ** (-5.0 - jnp.arange(RET_HEADS, dtype=f32))))
    idx = jnp.arange(RET_CHUNK, dtype=f32)
    diff = idx[:, None] - idx[None, :]
    causal = diff >= 0
    inner = jnp.where(causal[None], jnp.exp(jnp.where(causal, diff, 0.0)[None] * log_gamma[:, None, None]), 0.0)
    kv_decay = jnp.exp((RET_CHUNK - 1.0 - idx)[None, :] * log_gamma[:, None])
    q_decay = jnp.exp((idx + 1.0)[:, None] * log_gamma[None, :])
    chunk_decay = jnp.exp(RET_CHUNK * log_gamma)
    pairs = RET_HEADS // 2

    def by_lane(per_head):
        rows = per_head.shape[0]
        return jnp.repeat(per_head.reshape(rows, pairs, 2), RET_DK, axis=2).reshape(rows, pairs, LANES).transpose(1, 0, 2)

    qdec = by_lane(q_decay)
    kvdec = by_lane(kv_decay.T)
    cdec = by_lane(chunk_decay[None, :])
    lane_head = jnp.arange(LANES) // RET_DK
    bd = (lane_head[:, None] == lane_head[None, :]).astype(f32)
    return inner, qdec, kvdec, cdec, bd


def _ret(t, cos_r, sin_r):
    bsz, s, _ = t.shape
    tc = TILES["ret_rows"]
    w = RET_HEADS * RET_DK
    inner, qdec, kvdec, cdec, bd = _ret_constants()
    base = COL_RET // w
    col = lambda j: pl.BlockSpec((None, tc, w), lambda b, i: (b, i, base + j))
    tab = pl.BlockSpec((None, tc, LANES), lambda b, i: (b, i, 0))
    const = lambda shape: pl.BlockSpec(shape, lambda b, i: (0,) * len(shape))
    return pl.pallas_call(
        functools.partial(_ret_kernel, n_chunks=tc // RET_CHUNK),
        grid=(bsz, s // tc),
        in_specs=[col(0), col(1), col(2), col(3), tab, tab,
                  const(inner.shape), const(qdec.shape), const(kvdec.shape), const(cdec.shape), const(bd.shape)],
        out_specs=pl.BlockSpec((None, tc, w), lambda b, i: (b, i, 0)),
        out_shape=jax.ShapeDtypeStruct((bsz, s, w), BF16),
        scratch_shapes=[pltpu.VMEM((RET_HEADS // 2, LANES, LANES), F32)],
        compiler_params=_params("parallel", "arbitrary"),
        name="ret_mixer",
    )(t, t, t, t, cos_r, sin_r, inner, qdec, kvdec, cdec, bd)


def _mla_proj_kernel(t_ref, gq_ref, gkv_ref, wqt_ref, wk_ref, wvt_ref, cos_ref, sin_ref, cost_ref, sint_ref,
                     qt_ref, k_ref, vt_ref):
    ts = t_ref.shape[0]
    half = MLA_ROPE // 2
    nt = (((1,), (1,)), ((), ()))
    ckv = t_ref[:, 0:MLA_KV_RANK].astype(F32)
    kr = t_ref[:, MLA_KV_RANK:MLA_KV_RANK + LANES].astype(F32)
    cq = t_ref[:, MLA_PACK - MLA_Q_RANK:MLA_PACK].astype(F32)
    cqn = _rms(cq, gq_ref[...]).astype(BF16)
    ckvn = _rms(ckv, gkv_ref[...]).astype(BF16)

    qt = lax.dot_general(wqt_ref[...], cqn, nt, preferred_element_type=F32)
    cos_t = cost_ref[...]
    sin_t = sint_ref[...]
    scale = (MLA_NOPE + MLA_ROPE) ** -0.5 * LOG2_E
    for h in range(MLA_HEADS):
        base = h * LANES
        x1 = qt[base + MLA_NOPE:base + MLA_NOPE + half]
        x2 = qt[base + MLA_NOPE + half:base + MLA_NOPE + MLA_ROPE]
        blk = jnp.concatenate([qt[base:base + MLA_NOPE], x1 * cos_t - x2 * sin_t, x1 * sin_t + x2 * cos_t,
                               qt[base + MLA_NOPE + MLA_ROPE:base + LANES]], axis=0)
        qt_ref[base:base + LANES, :] = (blk * scale).astype(qt_ref.dtype)

    lane = lax.broadcasted_iota(jnp.int32, (ts, LANES), 1)
    swapped = jnp.where(lane < MLA_NOPE + half, pltpu.roll(kr, LANES - half, 1), pltpu.roll(kr, half, 1))
    k_rope = kr * cos_ref[...] + swapped * sin_ref[...]
    kn = jnp.dot(ckvn, wk_ref[...], preferred_element_type=F32)
    for h in range(MLA_HEADS):
        cols = slice(h * LANES, (h + 1) * LANES)
        k_ref[:, cols] = (kn[:, cols] + k_rope).astype(k_ref.dtype)

    vt = lax.dot_general(wvt_ref[...], ckvn, nt, preferred_element_type=F32)
    row = lax.broadcasted_iota(jnp.int32, vt.shape, 0)
    vt_ref[...] = jnp.where(row % LANES == MLA_V, 1.0, vt).astype(vt_ref.dtype)


def _mla_proj(l, t, gq, gkv, wqt, wk, wvt, cos_m, sin_m, cos_mt, sin_mt):
    bsz, s, _ = t.shape
    ts = TILES["mla_proj_rows"]
    hq = MLA_HEADS * LANES
    half = MLA_ROPE // 2
    tab = pl.BlockSpec((None, ts, LANES), lambda b, i: (b, i, 0))
    tab_t = pl.BlockSpec((None, half, ts), lambda b, i: (b, 0, i))
    rowmajor = pl.BlockSpec((None, ts, hq), lambda b, i: (b, i, 0))
    transposed = pl.BlockSpec((None, hq, ts), lambda b, i: (b, 0, i))
    return pl.pallas_call(
        _mla_proj_kernel,
        grid=(bsz, s // ts),
        in_specs=[pl.BlockSpec((None, ts, MLA_PACK), lambda b, i: (b, i, COL_MLA // MLA_PACK)),
                  _of_layer(l, 1, MLA_Q_RANK), _of_layer(l, 1, MLA_KV_RANK), _of_layer(l, *wqt.shape[1:]),
                  _of_layer(l, *wk.shape[1:]), _of_layer(l, *wvt.shape[1:]), tab, tab, tab_t, tab_t],
        out_specs=[transposed, rowmajor, transposed],
        out_shape=[jax.ShapeDtypeStruct((bsz, hq, s), BF16), jax.ShapeDtypeStruct((bsz, s, hq), BF16),
                   jax.ShapeDtypeStruct((bsz, hq, s), BF16)],
        compiler_params=_params("parallel", "parallel"),
        name="mla_proj",
    )(t, gq, gkv, wqt, wk, wvt, cos_m, sin_m, cos_mt, sin_mt)


def _flash_kernel(qt_ref, k_ref, vt_ref, o_ref, acc_ref, sa_ref, sb_ref, *, tq):
    g = pl.program_id(2)
    acc_ref[...] = jnp.zeros(acc_ref.shape, F32)

    def qk(j, dst_ref, which, masked):
        off = pl.multiple_of(j * tq, tq)
        for h in range(FLASH_HEADS):
            rows = slice(h * LANES, (h + 1) * LANES)
            s = jnp.dot(k_ref[pl.ds(off, tq), rows], qt_ref[rows, which * tq:(which + 1) * tq],
                        preferred_element_type=F32)
            if masked:
                key = lax.broadcasted_iota(jnp.int32, (tq, tq), 0)
                qry = lax.broadcasted_iota(jnp.int32, (tq, tq), 1)
                s = jnp.where(key <= qry, s, MASKED_SCORE)
            dst_ref[h] = s

    def softmax_pv(j, src_ref, which, ms):
        off = pl.multiple_of(j * tq, tq)
        new_m = []
        for h in range(FLASH_HEADS):
            s = src_ref[h]
            m_new = jnp.maximum(ms[h], jnp.max(s, axis=0, keepdims=True))
            alpha = jnp.exp2(ms[h] - m_new)
            p = jnp.exp2(s - m_new).astype(BF16)
            vt = vt_ref[h * LANES:h * LANES + V_ROWS, pl.ds(off, tq)]
            acc_ref[which, h] = alpha * acc_ref[which, h] + jnp.dot(vt, p, preferred_element_type=F32)
            new_m.append(m_new)
        return tuple(new_m)

    def sweep(which, diag, first_ref, other_ref):
        def pair(jj, ms):
            qk(2 * jj, other_ref, which, False)
            ms = softmax_pv(jnp.where(jj == 0, diag, 2 * jj - 1), first_ref, which, ms)
            qk(2 * jj + 1, first_ref, which, False)
            return softmax_pv(2 * jj, other_ref, which, ms)

        m0 = jnp.full((1, tq), MASKED_SCORE, F32)
        ms = lax.fori_loop(0, g, pair, (m0,) * FLASH_HEADS)
        return ms, jnp.where(g == 0, diag, 2 * g - 1)

    def write_out(which):
        for c in range(FLASH_HEADS // 2):
            outs = []
            for h in (2 * c, 2 * c + 1):
                acc = acc_ref[which, h]
                outs.append(acc[0:MLA_V, :] / acc[MLA_V:MLA_V + 1, :])
            o_ref[which * tq:(which + 1) * tq, c * LANES:(c + 1) * LANES] = (
                jnp.concatenate(outs, axis=0).T.astype(o_ref.dtype))

    qk(2 * g, sa_ref, 0, True)
    ms, left = sweep(0, 2 * g, sa_ref, sb_ref)
    qk(2 * g + 1, sb_ref, 1, True)
    softmax_pv(left, sa_ref, 0, ms)
    write_out(0)
    ms, left = sweep(1, 2 * g + 1, sb_ref, sa_ref)
    qk(2 * g, sa_ref, 1, False)
    ms = softmax_pv(left, sb_ref, 1, ms)
    softmax_pv(2 * g, sa_ref, 1, ms)
    write_out(1)


def _flash(qt, k, vt):
    bsz, s, _ = k.shape
    tq = TILES["flash_block"]
    hs = FLASH_HEADS
    return pl.pallas_call(
        functools.partial(_flash_kernel, tq=tq),
        grid=(bsz, MLA_HEADS // hs, s // (2 * tq)),
        in_specs=[
            pl.BlockSpec((None, hs * LANES, 2 * tq), lambda b, p, g: (b, p, g)),
            pl.BlockSpec((None, s, hs * LANES), lambda b, p, g: (b, 0, p)),
            pl.BlockSpec((None, hs * LANES, s), lambda b, p, g: (b, p, 0)),
        ],
        out_specs=pl.BlockSpec((None, 2 * tq, hs * MLA_V), lambda b, p, g: (b, g, p)),
        out_shape=jax.ShapeDtypeStruct((bsz, s, MLA_HEADS * MLA_V), BF16),
        scratch_shapes=[pltpu.VMEM((2, hs, V_ROWS, tq), F32), pltpu.VMEM((hs, tq, tq), F32),
                        pltpu.VMEM((hs, tq, tq), F32)],
        compiler_params=_params("parallel", "parallel", "arbitrary"),
        name="mla_flash",
    )(qt, k, vt)


def _merge_kernel(yl_ref, yr_ref, ym_ref, gt_ref, x_ref, wl_ref, wr_ref, wm_ref, wo_ref, gp_ref, gm_ref, o_ref):
    d = D_MODEL
    merged = _sigmoid(gt_ref[:, 0:d].astype(F32)) * jnp.dot(yl_ref[...], wl_ref[...], preferred_element_type=F32)
    merged = merged + _sigmoid(gt_ref[:, d:2 * d].astype(F32)) * jnp.dot(
        yr_ref[...], wr_ref[...], preferred_element_type=F32)
    merged = merged + _sigmoid(gt_ref[:, 2 * d:3 * d].astype(F32)) * jnp.dot(
        ym_ref[...], wm_ref[...], preferred_element_type=F32)
    y = jnp.dot(merged.astype(BF16), wo_ref[...], preferred_element_type=F32)
    o_ref[...] = x_ref[...] + gm_ref[...] * _rms(y, gp_ref[...])


def _merge(l, y_lru, y_ret, y_mla, t, x, wl, wr, wm, wo, g_post, mod, k_gate):
    bsz, s, d = x.shape
    tm = TILES["merge_rows"]
    w = y_lru.shape[-1]
    br = pl.BlockSpec((None, tm, w), lambda b, i: (b, i, 0))
    return pl.pallas_call(
        _merge_kernel,
        grid=(bsz, s // tm),
        in_specs=[br, br, br,
                  pl.BlockSpec((None, tm, 3 * d), lambda b, i: (b, i, COL_GATE // (3 * d))),
                  pl.BlockSpec((None, tm, d), lambda b, i: (b, i, 0)),
                  _of_layer(l, w, d), _of_layer(l, w, d), _of_layer(l, w, d), _of_layer(l, d, d),
                  _of_layer(l, 1, d), _mod_spec(l, k_gate, d)],
        out_specs=pl.BlockSpec((None, tm, d), lambda b, i: (b, i, 0)),
        out_shape=jax.ShapeDtypeStruct((bsz, s, d), F32),
        compiler_params=_params("parallel", "parallel"),
        name="mixer_merge",
    )(y_lru, y_ret, y_mla, t, x, wl, wr, wm, wo, g_post, mod)


def _ffn_kernel(x_ref, gpre_ref, sh_ref, sc_ref, wu_ref, cw_ref, cb_ref, wd_ref, gp_ref, gf_ref, o_ref,
                act_ref, halo_ref, *, tm, cw):
    @pl.when(pl.program_id(1) == 0)
    def _():
        halo_ref[...] = jnp.zeros(halo_ref.shape, F32)

    row = lax.broadcasted_iota(jnp.int32, (SUBLANES, cw), 0)
    x = x_ref[...]
    h = (_rms(x, gpre_ref[...]) * (1.0 + sc_ref[...]) + sh_ref[...]).astype(BF16)

    def conv(c0, gain):
        cols = slice(c0, c0 + cw)
        taps = cw_ref[:, cols] * gain
        bias = cb_ref[:, cols] * gain
        xv = jnp.dot(h, wu_ref[:, cols], preferred_element_type=F32)
        h1 = halo_ref[SUBLANES - 1:SUBLANES, cols]
        h2 = halo_ref[SUBLANES - 2:SUBLANES - 1, cols]
        halo_ref[:, cols] = xv[tm - SUBLANES:, :]
        r1 = pltpu.roll(xv, 1, 0)
        r2 = pltpu.roll(xv, 2, 0)
        xm1 = jnp.concatenate([jnp.where(row == 0, h1, r1[:SUBLANES]), r1[SUBLANES:]], axis=0)
        top2 = jnp.where(row == 0, h2, jnp.where(row == 1, h1, r2[:SUBLANES]))
        xm2 = jnp.concatenate([top2, r2[SUBLANES:]], axis=0)
        y = xv * taps[2:3] + bias
        y = y + xm2 * taps[0:1]
        return y + xm1 * taps[1:2]

    for c in range(D_FF // cw):
        hu = conv(c * cw, 0.5)
        g = conv(D_FF + c * cw, 1.0)
        th = jnp.tanh(g * (GELU_C1 + GELU_C2 * (g * g)))
        act_ref[:, c * cw:(c + 1) * cw] = ((hu * g) * (1.0 + th)).astype(BF16)
    y = jnp.dot(act_ref[...], wd_ref[...], preferred_element_type=F32)
    o_ref[...] = x + gf_ref[...] * _rms(y, gp_ref[...])


def _ffn(l, x, g_pre, mod, k_shift, k_scale, k_gate, wu, conv_w, conv_b, wd, g_post):
    bsz, s, d = x.shape
    tm = TILES["ffn_rows"]
    n = wu.shape[-1]
    single = dict(pipeline_mode=pl.Buffered(1))
    return pl.pallas_call(
        functools.partial(_ffn_kernel, tm=tm, cw=TILES["ffn_cols"]),
        grid=(bsz, s // tm),
        in_specs=[pl.BlockSpec((None, tm, d), lambda b, i: (b, i, 0)),
                  _of_layer(l, 1, d), _mod_spec(l, k_shift, d), _mod_spec(l, k_scale, d),
                  _of_layer(l, d, n, **single), _of_layer(l, FFN_CONV, n), _of_layer(l, 1, n),
                  _of_layer(l, D_FF, d, **single), _of_layer(l, 1, d), _mod_spec(l, k_gate, d)],
        out_specs=pl.BlockSpec((None, tm, d), lambda b, i: (b, i, 0)),
        out_shape=jax.ShapeDtypeStruct((bsz, s, d), F32),
        scratch_shapes=[pltpu.VMEM((tm, D_FF), BF16), pltpu.VMEM((SUBLANES, n), F32)],
        compiler_params=_params("parallel", "arbitrary"),
        name="ffn_fused",
    )(x, g_pre, mod, mod, wu, conv_w, conv_b, wd, g_post, mod)


def _pack_w_in_kernel(w_ref, o_ref, *, n_all):
    o_cq = 2 * LRU_WIDTH + 4 * RET_HEADS * RET_DK
    o_ckv = o_cq + MLA_Q_RANK
    o_kr = o_ckv + MLA_KV_RANK
    o_gate = o_kr + MLA_ROPE
    n_gate = n_all - o_gate
    rows = w_ref.shape[0]
    o_ref[:, 0:o_cq] = w_ref[:, 0:o_cq]
    o_ref[:, COL_MLA:COL_MLA + MLA_KV_RANK] = w_ref[:, o_ckv:o_kr]
    o_ref[:, N_IN_PACKED - MLA_Q_RANK:N_IN_PACKED] = w_ref[:, o_cq:o_ckv]
    tail = w_ref[:, o_kr:w_ref.shape[1]].astype(F32)
    o_ref[:, COL_GATE:COL_GATE + n_gate] = tail[:, MLA_ROPE:MLA_ROPE + n_gate].astype(BF16)
    kr = jnp.concatenate([jnp.zeros((rows, KR_LANE), F32), tail[:, :MLA_ROPE],
                          jnp.zeros((rows, LANES - KR_LANE - MLA_ROPE), F32)], axis=1)
    o_ref[:, COL_MLA + MLA_KV_RANK:COL_MLA + MLA_KV_RANK + LANES] = kr.astype(BF16)


def _pack_w_in(w, n_all):
    depth, d, n = w.shape
    tr = TILES["pack_rows"]
    return pl.pallas_call(
        functools.partial(_pack_w_in_kernel, n_all=n_all),
        grid=(depth, d // tr),
        in_specs=[pl.BlockSpec((None, tr, n), lambda l, i: (l, i, 0))],
        out_specs=pl.BlockSpec((None, tr, N_IN_PACKED), lambda l, i: (l, i, 0)),
        out_shape=jax.ShapeDtypeStruct((depth, d, N_IN_PACKED), BF16),
        compiler_params=_params("parallel", "parallel"),
        name="pack_w_in",
    )(w)


def _pad_heads(w, width):
    depth, r, _ = w.shape
    w4 = w.reshape(depth, r, MLA_HEADS, width)
    return jnp.pad(w4, ((0, 0), (0, 0), (0, 0), (0, LANES - width))).reshape(depth, r, MLA_HEADS * LANES)


def _pack_w_ukv(w):
    depth, r, _ = w.shape
    w4 = w.reshape(depth, r, MLA_HEADS, MLA_NOPE + MLA_V)
    k = _pad_heads(w4[..., :MLA_NOPE].reshape(depth, r, -1), MLA_NOPE)
    v = _pad_heads(w4[..., MLA_NOPE:].reshape(depth, r, -1), MLA_V)
    return k.astype(BF16), jnp.swapaxes(v, 1, 2).astype(BF16)


def _block_diag(w):
    depth, nb, n, _ = w.shape
    eye = jnp.eye(nb, dtype=w.dtype)
    return (eye[None, :, None, :, None] * w[:, :, :, None, :]).reshape(depth, nb * n, nb * n)


def kernel(x, c, positions, ada_w, ada_b, mix_pre_g, mix_post_g, w_in, lru_conv_w, lru_conv_b, lru_wa, lru_ba, lru_wx, lru_bx, lru_lambda, lru_wo, ret_wo, mla_q_norm_g, mla_w_uq, mla_kv_norm_g, mla_w_ukv, mla_wo, w_out, ffn_pre_g, ffn_post_g, ffn_w_up, ffn_conv_w, ffn_conv_b, ffn_w_down):
    bsz, s, d = x.shape
    depth = w_in.shape[0]
    row = lambda a: a.reshape(depth, 1, a.shape[-1])
    mod = _ada(c, ada_w, ada_b).reshape(depth, SUBLANES, 6, 1, d)
    cos_r, sin_r, cos_m, sin_m, cos_mt, sin_mt = _rope_tables(positions)

    n_in = w_in.shape[-1]
    w_in_p = _pack_w_in(jnp.concatenate(
        [w_in.astype(BF16), jnp.zeros((depth, d, N_IN_PACKED - n_in), BF16)], axis=-1), n_in)
    wbd = jnp.concatenate([_block_diag(lru_wa), _block_diag(lru_wx)], axis=-1).astype(BF16)
    bb = jnp.concatenate([lru_ba, lru_bx], axis=-1).reshape(depth, 1, 2 * LRU_WIDTH)
    w_uqt = jnp.swapaxes(_pad_heads(mla_w_uq, MLA_NOPE + MLA_ROPE), 1, 2).astype(BF16)
    w_uk, w_uvt = _pack_w_ukv(mla_w_ukv)
    wl, wr, wm, wo = (a.astype(BF16) for a in (lru_wo, ret_wo, mla_wo, w_out))
    wu, wd = ffn_w_up.astype(BF16), ffn_w_down.astype(BF16)

    for l in range(depth):
        t = _prenorm_matmul(l, x, row(mix_pre_g), mod, 0, 1, w_in_p, "mixer_in_proj")
        y_lru = _lru(l, t, lru_conv_w, row(lru_conv_b), wbd, bb, row(lru_lambda))
        y_ret = _ret(t, cos_r, sin_r)
        qt, k, vt = _mla_proj(l, t, row(mla_q_norm_g), row(mla_kv_norm_g), w_uqt, w_uk, w_uvt,
                              cos_m, sin_m, cos_mt, sin_mt)
        y_mla = _flash(qt, k, vt)
        x = _merge(l, y_lru, y_ret, y_mla, t, x, wl, wr, wm, wo, row(mix_post_g), mod, 2)
        x = _ffn(l, x, row(ffn_pre_g), mod, 3, 4, 5, wu, ffn_conv_w, row(ffn_conv_b), wd, row(ffn_post_g))
    return x
```

```python
import functools

import jax
import jax.numpy as jnp
from jax import lax
from jax.experimental import pallas as pl
from jax.experimental.pallas import tpu as pltpu

F32 = jnp.float32
BF16 = jnp.bfloat16

D_MODEL = 1024
EPS = 1e-6
ROPE_THETA = 10000.0
LRU_WIDTH = 512
LRU_CONV = 4
LRU_C = 8.0
RET_HEADS = 8
RET_DK = 64
RET_DV = 64
RET_CHUNK = 128
MLA_HEADS = 8
MLA_Q_RANK = 384
MLA_KV_RANK = 256
MLA_NOPE = 64
MLA_ROPE = 32
MLA_V = 64
D_FF = 2816
FFN_CONV = 3

LANES = 128
SUBLANES = 8
MXU_DIM = 256
VMEM_LIMIT = 56 * 1024 * 1024

TILES = dict(
    ada_cols=1536,
    rope_rows=2048,
    pack_rows=256,
    in_proj_rows=512, in_proj_cols=2304,
    lru_rows=1024,
    ret_rows=2048,
    mla_proj_rows=2048,
    flash_block=512,
    merge_rows=1024,
    ffn_rows=1024, ffn_cols=MXU_DIM,
)
MASKED_SCORE = -1e30

COL_LRU = 0
COL_RET = 1024
COL_GATE = 3072
COL_MLA = 6144
N_IN_PACKED = 6912
MLA_PACK = 768
KR_LANE = 64
LOG2_E = 1.4426950408889634
FLASH_HEADS = 4
V_ROWS = MLA_V + 16


def _params(*sem):
    return pltpu.CompilerParams(dimension_semantics=sem, vmem_limit_bytes=VMEM_LIMIT)


def _of_layer(l, *tail, **kw):
    return pl.BlockSpec((None,) + tail, lambda *_: (l,) + (0,) * len(tail), **kw)


def _mod_spec(l, k, d):
    return pl.BlockSpec((None, None, None, 1, d), lambda b, *_: (l, b, k, 0, 0))


GELU_C1 = 0.7978845608028654
GELU_C2 = GELU_C1 * 0.044715


def _gelu_tanh(x):
    return 0.5 * x * (1.0 + jnp.tanh(x * (GELU_C1 + GELU_C2 * (x * x))))


def _sigmoid(x):
    return 0.5 * jnp.tanh(0.5 * x) + 0.5


def _rms(x, g):
    return x * lax.rsqrt(jnp.mean(x * x, axis=-1, keepdims=True) + EPS) * g


def _ada_kernel(c_ref, w_ref, b_ref, o_ref):
    c = c_ref[...]
    ca = c * _sigmoid(c)
    o_ref[...] = jnp.dot(ca, w_ref[...], preferred_element_type=F32,
                         precision=lax.Precision.HIGHEST) + b_ref[...]


def _ada(c, ada_w, ada_b):
    depth, d, n = ada_w.shape
    rows = SUBLANES
    c_pad = jnp.pad(c, ((0, rows - c.shape[0]), (0, 0)))
    tn = TILES["ada_cols"]
    return pl.pallas_call(
        _ada_kernel,
        grid=(depth, n // tn),
        in_specs=[
            pl.BlockSpec((rows, d), lambda l, j: (0, 0)),
            pl.BlockSpec((None, d, tn), lambda l, j: (l, 0, j)),
            pl.BlockSpec((None, 1, tn), lambda l, j: (l, 0, j)),
        ],
        out_specs=pl.BlockSpec((None, rows, tn), lambda l, j: (l, 0, j)),
        out_shape=jax.ShapeDtypeStruct((depth, rows, n), F32),
        compiler_params=_params("parallel", "parallel"),
        name="ada_mod",
    )(c_pad, ada_w, ada_b.reshape(depth, 1, n))


def _rope_kernel(pos_ref, inv_ref, cr_ref, sr_ref, cm_ref, sm_ref, cmt_ref, smt_ref):
    half_r = RET_DK // 2
    half_m = MLA_ROPE // 2
    half_l = LANES // 2
    rows = pos_ref.shape[0] // 2
    pos = pos_ref[...].astype(F32)
    lane = lax.broadcasted_iota(jnp.int32, (rows, LANES), 1)
    ang = jnp.where(lane < half_l, pos[:rows], pos[rows:]) * inv_ref[...]
    c2 = jnp.cos(ang)
    s2 = jnp.sin(ang)

    def unpack(x, lo, n):
        return jnp.concatenate([x[:, lo:lo + n], x[:, half_l + lo:half_l + lo + n]], axis=0)

    c_r, s_r = unpack(c2, 0, half_r), unpack(s2, 0, half_r)
    c_m, s_m = unpack(c2, half_r, half_m), unpack(s2, half_r, half_m)
    reps = LANES // RET_DK
    cr_ref[...] = jnp.concatenate([c_r, c_r] * reps, axis=1)
    sr_ref[...] = jnp.concatenate([-s_r, s_r] * reps, axis=1)
    ts = 2 * rows
    tail = LANES - MLA_NOPE - MLA_ROPE
    cm_ref[...] = jnp.concatenate([jnp.ones((ts, MLA_NOPE), F32), c_m, c_m, jnp.ones((ts, tail), F32)], axis=1)
    sm_ref[...] = jnp.concatenate([jnp.zeros((ts, MLA_NOPE), F32), -s_m, s_m, jnp.zeros((ts, tail), F32)], axis=1)
    c2t = c2.T
    s2t = s2.T
    cmt_ref[...] = jnp.concatenate([c2t[half_r:half_r + half_m], c2t[half_l + half_r:half_l + half_r + half_m]], axis=1)
    smt_ref[...] = jnp.concatenate([s2t[half_r:half_r + half_m], s2t[half_l + half_r:half_l + half_r + half_m]], axis=1)


def _rope_tables(positions):
    bsz, s = positions.shape
    ts = TILES["rope_rows"]
    inv_r =ROPE_THETA ** (-jnp.arange(0, RET_DK, 2, dtype=F32) / RET_DK)
    inv_m = ROPE_THETA ** (-jnp.arange(0, MLA_ROPE, 2, dtype=F32) / MLA_ROPE)
    inv_half = jnp.concatenate([inv_r, inv_m, jnp.zeros(LANES // 2 - inv_r.size - inv_m.size, F32)])
    inv = jnp.tile(inv_half, 2).reshape(1, LANES)
    tab = pl.BlockSpec((None, ts, LANES), lambda b, i: (b, i, 0))
    shp = jax.ShapeDtypeStruct((bsz, s, LANES), F32)
    tab_t = pl.BlockSpec((None, inv_m.size, ts), lambda b, i: (b, 0, i))
    shp_t = jax.ShapeDtypeStruct((bsz, inv_m.size, s), F32)
    return pl.pallas_call(
        _rope_kernel,
        grid=(bsz, s // ts),
        in_specs=[pl.BlockSpec((None, ts, 1), lambda b, i: (b, i, 0)), pl.BlockSpec((1, LANES), lambda b, i: (0, 0))],
        out_specs=[tab, tab, tab, tab, tab_t, tab_t],
        out_shape=[shp, shp, shp, shp, shp_t, shp_t],
        compiler_params=_params("parallel", "parallel"),
        name="rope_tables",
    )(positions.reshape(bsz, s, 1), inv)


def _prenorm_matmul_kernel(x_ref, g_ref, sh_ref, sc_ref, w_ref, o_ref, *, tn):
    h = (_rms(x_ref[...], g_ref[...]) * (1.0 + sc_ref[...]) + sh_ref[...]).astype(BF16)
    for j in range(w_ref.shape[1] // tn):
        cols = slice(j * tn, (j + 1) * tn)
        o_ref[:, cols] = jnp.dot(h, w_ref[:, cols], preferred_element_type=F32).astype(o_ref.dtype)


def _prenorm_matmul(l, x, g, mod, k_shift, k_scale, w, name):
    bsz, s, d = x.shape
    n = w.shape[-1]
    tm, tn = TILES["in_proj_rows"], TILES["in_proj_cols"]
    return pl.pallas_call(
        functools.partial(_prenorm_matmul_kernel, tn=tn),
        grid=(bsz, s // tm),
        in_specs=[
            pl.BlockSpec((None, tm, d), lambda b, i: (b, i, 0)),
            _of_layer(l, 1, d),
            _mod_spec(l, k_shift, d), _mod_spec(l, k_scale, d),
            _of_layer(l, d, n, pipeline_mode=pl.Buffered(1)),
        ],
        out_specs=pl.BlockSpec((None, tm, n), lambda b, i: (b, i, 0)),
        out_shape=jax.ShapeDtypeStruct((bsz, s, n), BF16),
        compiler_params=_params("parallel", "parallel"),
        name=name,
    )(x, g, mod, mod, w)


def _lru_kernel(t_ref, cw_ref, cb_ref, wbd_ref, bb_ref, lam_ref, o_ref, xbuf_ref, hc_ref, h_ref, *, ts):
    w = LRU_WIDTH

    @pl.when(pl.program_id(1) == 0)
    def _():
        xbuf_ref[0:SUBLANES, :] = jnp.zeros((SUBLANES, w), F32)
        hc_ref[...] = jnp.zeros((1, w), F32)

    xb = t_ref[:, 0:w].astype(F32)
    gb = t_ref[:, w:2 * w].astype(F32)
    xbuf_ref[SUBLANES:SUBLANES + ts, :] = xb
    xc = xb * cw_ref[LRU_CONV - 1:LRU_CONV, :] + cb_ref[...]
    for k in range(LRU_CONV - 1):
        back = LRU_CONV - 1 - k
        xc = xc + xbuf_ref[pl.ds(SUBLANES - back, ts), :] * cw_ref[k:k + 1, :]
    xbuf_ref[0:SUBLANES, :] = xb[ts - SUBLANES:, :]

    z = jnp.dot(xc.astype(BF16), wbd_ref[...], preferred_element_type=F32) + bb_ref[...]
    r = _sigmoid(z[:, :w])
    ig = _sigmoid(z[:, w:])
    nl = -lam_ref[...]
    softplus = jnp.maximum(nl, 0.0) + jnp.log1p(jnp.exp(-jnp.abs(nl)))
    log_a = (-LRU_C) * r * softplus
    a = jnp.exp(log_a)
    th = jnp.tanh(log_a)
    u = jnp.sqrt(-2.0 * th / (1.0 - th)) * (ig * xc)

    groups = ts // SUBLANES
    a = a.reshape(groups, SUBLANES, w)
    u = u.reshape(groups, SUBLANES, w)
    sub = lax.broadcasted_iota(jnp.int32, (groups, SUBLANES, w), 1)
    k = 1
    while k < SUBLANES:
        keep = sub >= k
        a_prev = jnp.where(keep, pltpu.roll(a, k, 1), 1.0)
        u_prev = jnp.where(keep, pltpu.roll(u, k, 1), 0.0)
        u = a * u_prev + u
        a = a * a_prev
        k *= 2
    h_prev = hc_ref[...]
    for r in range(groups):
        hb = a[r] * h_prev + u[r]
        h_ref[r * SUBLANES:(r + 1) * SUBLANES, :] = hb
        h_prev = hb[SUBLANES - 1:SUBLANES, :]
    hc_ref[...] = h_prev
    o_ref[...] = (h_ref[...] * _gelu_tanh(gb)).astype(o_ref.dtype)


def _lru(l, t, conv_w, conv_b, wbd, bb, lam):
    bsz, s, _ = t.shape
    ts = TILES["lru_rows"]
    w = LRU_WIDTH
    return pl.pallas_call(
        functools.partial(_lru_kernel, ts=ts),
        grid=(bsz, s // ts),
        in_specs=[
            pl.BlockSpec((None, ts, 2 * w), lambda b, i: (b, i, COL_LRU // (2 * w))),
            _of_layer(l, LRU_CONV, w), _of_layer(l, 1, w), _of_layer(l, w, 2 * w), _of_layer(l, 1, 2 * w),
            _of_layer(l, 1, w),
        ],
        out_specs=pl.BlockSpec((None, ts, w), lambda b, i: (b, i, 0)),
        out_shape=jax.ShapeDtypeStruct((bsz, s, w), BF16),
        scratch_shapes=[pltpu.VMEM((ts + SUBLANES, w), F32), pltpu.VMEM((1, w), F32), pltpu.VMEM((ts, w), F32)],
        compiler_params=_params("parallel", "arbitrary"),
        name="lru_mixer",
    )(t, conv_w, conv_b, wbd, bb, lam)


def _ret_kernel(q_ref, k_ref, v_ref, g_ref, cos_ref, sin_ref, dec_ref, qdec_ref, kvdec_ref, cdec_ref,
                bd_ref, o_ref, st_ref, *, n_chunks):
    c_len = RET_CHUNK
    pairs = RET_HEADS // 2

    @pl.when(pl.program_id(1) == 0)
    def _():
        st_ref[...] = jnp.zeros(st_ref.shape, F32)

    lane = lax.broadcasted_iota(jnp.int32, (c_len, LANES), 1)
    head0 = lane < RET_DK
    first_half = (lane % RET_DK) < (RET_DK // 2)
    inv_n = 1.0 / RET_DV

    def rope(x, cos, sin):
        swapped = jnp.where(first_half, pltpu.roll(x, LANES - RET_DK // 2, 1), pltpu.roll(x, RET_DK // 2, 1))
        return x * cos + swapped * sin

    def head_mean(x):
        m0 = jnp.sum(jnp.where(head0, x, 0.0), axis=-1, keepdims=True) * inv_n
        m1 = jnp.sum(jnp.where(head0, 0.0, x), axis=-1, keepdims=True) * inv_n
        return jnp.where(head0, m0, m1)

    nt = (((1,), (1,)), ((), ()))
    tn = (((0,), (0,)), ((), ()))
    for c in range(n_chunks):
        rows = slice(c * c_len, (c + 1) * c_len)
        cos = cos_ref[rows, :]
        sin = sin_ref[rows, :]
        for p in range(pairs):
            cols = slice(p * LANES, (p + 1) * LANES)
            q = rope(q_ref[rows, cols].astype(F32), cos, sin)
            k = rope(k_ref[rows, cols].astype(F32), cos, sin) * (RET_DK ** -0.5)
            v = v_ref[rows, cols]
            qb = q.astype(BF16)
            kb = k.astype(BF16)
            zero = jnp.zeros_like(qb)
            s0 = lax.dot_general(jnp.where(head0, qb, zero), kb, nt, preferred_element_type=F32)
            s1 = lax.dot_general(jnp.where(head0, zero, qb), kb, nt, preferred_element_type=F32)
            probs = jnp.concatenate([s0 * dec_ref[2 * p], s1 * dec_ref[2 * p + 1]], axis=1).astype(BF16)
            v2 = jnp.concatenate([jnp.where(head0, v, zero), jnp.where(head0, zero, v)], axis=0)
            y = jnp.dot(probs, v2, preferred_element_type=F32)
            state = st_ref[p]
            y = y + jnp.dot(qb, state.astype(BF16), preferred_element_type=F32) * qdec_ref[p]
            vd = (v.astype(F32) * kvdec_ref[p]).astype(BF16)
            kv = lax.dot_general(kb, vd, tn, preferred_element_type=F32)
            st_ref[p] = state * cdec_ref[p] + kv * bd_ref[...]

            d = y - head_mean(y)
            yn = d * lax.rsqrt(head_mean(d * d) + EPS)
            g = g_ref[rows, cols].astype(F32)
            o_ref[rows, cols] = (g * _sigmoid(g) * yn).astype(o_ref.dtype)


def _ret_constants():
    f32 = F32
    log_gamma = jnp.log1p(-(2.0 ** (-5.0 - jnp.arange(RET_HEADS, dtype=f32))))
    idx = jnp.arange(RET_CHUNK, dtype=f32)
    diff = idx[:, None] - idx[None, :]
    causal = diff >= 0
    inner = jnp.where(causal[None], jnp.exp(jnp.where(causal, diff, 0.0)[None] * log_gamma[:, None, None]), 0.0)
    kv_decay = jnp.exp((RET_CHUNK - 1.0 - idx)[None, :] * log_gamma[:, None])
    q_decay = jnp.exp((idx + 1.0)[:, None] * log_gamma[None, :])
    chunk_decay = jnp.exp(RET_CHUNK * log_gamma)
    pairs = RET_HEADS // 2

    def by_lane(per_head):
        rows = per_head.shape[0]
        return jnp.repeat(per_head.reshape(rows, pairs, 2), RET_DK, axis=2).reshape(rows, pairs, LANES).transpose(1, 0, 2)

    qdec = by_lane(q_decay)
    kvdec = by_lane(kv_decay.T)
    cdec = by_lane(chunk_decay[None, :])
    lane_head = jnp.arange(LANES) // RET_DK
    bd = (lane_head[:, None] == lane_head[None, :]).astype(f32)
    return inner, qdec, kvdec, cdec, bd


def _ret(t, cos_r, sin_r):
    bsz, s, _ = t.shape
    tc = TILES["ret_rows"]
    w = RET_HEADS * RET_DK
    inner, qdec, kvdec, cdec, bd = _ret_constants()
    base = COL_RET // w
    col = lambda j: pl.BlockSpec((None, tc, w), lambda b, i: (b, i, base + j))
    tab = pl.BlockSpec((None, tc, LANES), lambda b, i: (b, i, 0))
    const = lambda shape: pl.BlockSpec(shape, lambda b, i: (0,) * len(shape))
    return pl.pallas_call(
        functools.partial(_ret_kernel, n_chunks=tc // RET_CHUNK),
        grid=(bsz, s // tc),
        in_specs=[col(0), col(1), col(2), col(3), tab, tab,
                  const(inner.shape), const(qdec.shape), const(kvdec.shape), const(cdec.shape), const(bd.shape)],
        out_specs=pl.BlockSpec((None, tc, w), lambda b, i: (b, i, 0)),
        out_shape=jax.ShapeDtypeStruct((bsz, s, w), BF16),
        scratch_shapes=[pltpu.VMEM((RET_HEADS // 2, LANES, LANES), F32)],
        compiler_params=_params("parallel", "arbitrary"),
        name="ret_mixer",
    )(t, t, t, t, cos_r, sin_r, inner, qdec, kvdec, cdec, bd)


def _mla_proj_kernel(t_ref, gq_ref, gkv_ref, wqt_ref, wk_ref, wvt_ref, cos_ref, sin_ref, cost_ref, sint_ref,
                     qt_ref, k_ref, vt_ref):
    ts = t_ref.shape[0]
    half = MLA_ROPE // 2
    nt = (((1,), (1,)), ((), ()))
    ckv = t_ref[:, 0:MLA_KV_RANK].astype(F32)
    kr = t_ref[:, MLA_KV_RANK:MLA_KV_RANK + LANES].astype(F32)
    cq = t_ref[:, MLA_PACK - MLA_Q_RANK:MLA_PACK].astype(F32)
    cqn = _rms(cq, gq_ref[...]).astype(BF16)
    ckvn = _rms(ckv, gkv_ref[...]).astype(BF16)

    qt = lax.dot_general(wqt_ref[...], cqn, nt, preferred_element_type=F32)
    cos_t = cost_ref[...]
    sin_t = sint_ref[...]
    scale = (MLA_NOPE + MLA_ROPE) ** -0.5 * LOG2_E
    for h in range(MLA_HEADS):
        base = h * LANES
        x1 = qt[base + MLA_NOPE:base + MLA_NOPE + half]
        x2 = qt[base + MLA_NOPE + half:base + MLA_NOPE + MLA_ROPE]
        blk = jnp.concatenate([qt[base:base + MLA_NOPE], x1 * cos_t - x2 * sin_t, x1 * sin_t + x2 * cos_t,
                               qt[base + MLA_NOPE + MLA_ROPE:base + LANES]], axis=0)
        qt_ref[base:base + LANES, :] = (blk * scale).astype(qt_ref.dtype)

    lane = lax.broadcasted_iota(jnp.int32, (ts, LANES), 1)
    swapped = jnp.where(lane < MLA_NOPE + half, pltpu.roll(kr, LANES - half, 1), pltpu.roll(kr, half, 1))
    k_rope = kr * cos_ref[...] + swapped * sin_ref[...]
    kn = jnp.dot(ckvn, wk_ref[...], preferred_element_type=F32)
    for h in range(MLA_HEADS):
        cols = slice(h * LANES, (h + 1) * LANES)
        k_ref[:, cols] = (kn[:, cols] + k_rope).astype(k_ref.dtype)

    vt = lax.dot_general(wvt_ref[...], ckvn, nt, preferred_element_type=F32)
    row = lax.broadcasted_iota(jnp.int32, vt.shape, 0)
    vt_ref[...] = jnp.where(row % LANES == MLA_V, 1.0, vt).astype(vt_ref.dtype)


def _mla_proj(l, t, gq, gkv, wqt, wk, wvt, cos_m, sin_m, cos_mt, sin_mt):
    bsz, s, _ = t.shape
    ts = TILES["mla_proj_rows"]
    hq = MLA_HEADS * LANES
    half = MLA_ROPE // 2
    tab = pl.BlockSpec((None, ts, LANES), lambda b, i: (b, i, 0))
    tab_t = pl.BlockSpec((None, half, ts), lambda b, i: (b, 0, i))
    rowmajor = pl.BlockSpec((None, ts, hq), lambda b, i: (b, i, 0))
    transposed = pl.BlockSpec((None, hq, ts), lambda b, i: (b, 0, i))
    return pl.pallas_call(
        _mla_proj_kernel,
        grid=(bsz, s // ts),
        in_specs=[pl.BlockSpec((None, ts, MLA_PACK), lambda b, i: (b, i, COL_MLA // MLA_PACK)),
                  _of_layer(l, 1, MLA_Q_RANK), _of_layer(l, 1, MLA_KV_RANK), _of_layer(l, *wqt.shape[1:]),
                  _of_layer(l, *wk.shape[1:]), _of_layer(l, *wvt.shape[1:]), tab, tab, tab_t, tab_t],
        out_specs=[transposed, rowmajor, transposed],
        out_shape=[jax.ShapeDtypeStruct((bsz, hq, s), BF16), jax.ShapeDtypeStruct((bsz, s, hq), BF16),
                   jax.ShapeDtypeStruct((bsz, hq, s), BF16)],
        compiler_params=_params("parallel", "parallel"),
        name="mla_proj",
    )(t, gq, gkv, wqt, wk, wvt, cos_m, sin_m, cos_mt, sin_mt)


def _flash_kernel(qt_ref, k_ref, vt_ref, o_ref, acc_ref, sa_ref, sb_ref, *, tq):
    g = pl.program_id(2)
    acc_ref[...] = jnp.zeros(acc_ref.shape, F32)

    def qk(j, dst_ref, which, masked):
        off = pl.multiple_of(j * tq, tq)
        for h in range(FLASH_HEADS):
            rows = slice(h * LANES, (h + 1) * LANES)
            s = jnp.dot(k_ref[pl.ds(off, tq), rows], qt_ref[rows, which * tq:(which + 1) * tq],
                        preferred_element_type=F32)
            if masked:
                key = lax.broadcasted_iota(jnp.int32, (tq, tq), 0)
                qry = lax.broadcasted_iota(jnp.int32, (tq, tq), 1)
                s = jnp.where(key <= qry, s, MASKED_SCORE)
            dst_ref[h] = s

    def softmax_pv(j, src_ref, which, ms):
        off = pl.multiple_of(j * tq, tq)
        new_m = []
        for h in range(FLASH_HEADS):
            s = src_ref[h]
            m_new = jnp.maximum(ms[h], jnp.max(s, axis=0, keepdims=True))
            alpha = jnp.exp2(ms[h] - m_new)
            p = jnp.exp2(s - m_new).astype(BF16)
            vt = vt_ref[h * LANES:h * LANES + V_ROWS, pl.ds(off, tq)]
            acc_ref[which, h] = alpha * acc_ref[which, h] + jnp.dot(vt, p, preferred_element_type=F32)
            new_m.append(m_new)
        return tuple(new_m)

    def sweep(which, diag, first_ref, other_ref):
        def pair(jj, ms):
            qk(2 * jj, other_ref, which, False)
            ms = softmax_pv(jnp.where(jj == 0, diag, 2 * jj - 1), first_ref, which, ms)
            qk(2 * jj + 1, first_ref, which, False)
            return softmax_pv(2 * jj, other_ref, which, ms)

        m0 = jnp.full((1, tq), MASKED_SCORE, F32)
        ms = lax.fori_loop(0, g, pair, (m0,) * FLASH_HEADS)
        return ms, jnp.where(g == 0, diag, 2 * g - 1)

    def write_out(which):
        for c in range(FLASH_HEADS // 2):
            outs = []
            for h in (2 * c, 2 * c + 1):
                acc = acc_ref[which, h]
                outs.append(acc[0:MLA_V, :] / acc[MLA_V:MLA_V + 1, :])
            o_ref[which * tq:(which + 1) * tq, c * LANES:(c + 1) * LANES] = (
                jnp.concatenate(outs, axis=0).T.astype(o_ref.dtype))

    qk(2 * g, sa_ref, 0, True)
    ms, left = sweep(0, 2 * g, sa_ref, sb_ref)
    qk(2 * g + 1, sb_ref, 1, True)
    softmax_pv(left, sa_ref, 0, ms)
    write_out(0)
    ms, left = sweep(1, 2 * g + 1, sb_ref, sa_ref)
    qk(2 * g, sa_ref, 1, False)
    ms = softmax_pv(left, sb_ref, 1, ms)
    softmax_pv(2 * g, sa_ref, 1, ms)
    write_out(1)


def _flash(qt, k, vt):
    bsz, s, _ = k.shape
    tq = TILES["flash_block"]
    hs = FLASH_HEADS
    return pl.pallas_call(
        functools.partial(_flash_kernel, tq=tq),
        grid=(bsz, MLA_HEADS // hs, s // (2 * tq)),
        in_specs=[
            pl.BlockSpec((None, hs * LANES, 2 * tq), lambda b, p, g: (b, p, g)),
            pl.BlockSpec((None, s, hs * LANES), lambda b, p, g: (b, 0, p)),
            pl.BlockSpec((None, hs * LANES, s), lambda b, p, g: (b, p, 0)),
        ],
        out_specs=pl.BlockSpec((None, 2 * tq, hs * MLA_V), lambda b, p, g: (b, g, p)),
        out_shape=jax.ShapeDtypeStruct((bsz, s, MLA_HEADS * MLA_V), BF16),
        scratch_shapes=[pltpu.VMEM((2, hs, V_ROWS, tq), F32), pltpu.VMEM((hs, tq, tq), F32),
                        pltpu.VMEM((hs, tq, tq), F32)],
        compiler_params=_params("parallel", "parallel", "arbitrary"),
        name="mla_flash",
    )(qt, k, vt)


def _merge_kernel(yl_ref, yr_ref, ym_ref, gt_ref, x_ref, wl_ref, wr_ref, wm_ref, wo_ref, gp_ref, gm_ref, o_ref):
    d = D_MODEL
    merged = _sigmoid(gt_ref[:, 0:d].astype(F32)) * jnp.dot(yl_ref[...], wl_ref[...], preferred_element_type=F32)
    merged = merged + _sigmoid(gt_ref[:, d:2 * d].astype(F32)) * jnp.dot(
        yr_ref[...], wr_ref[...], preferred_element_type=F32)
    merged = merged + _sigmoid(gt_ref[:, 2 * d:3 * d].astype(F32)) * jnp.dot(
        ym_ref[...], wm_ref[...], preferred_element_type=F32)
    y = jnp.dot(merged.astype(BF16), wo_ref[...], preferred_element_type=F32)
    o_ref[...] = x_ref[...] + gm_ref[...] * _rms(y, gp_ref[...])


def _merge(l, y_lru, y_ret, y_mla, t, x, wl, wr, wm, wo, g_post, mod, k_gate):
    bsz, s, d = x.shape
    tm = TILES["merge_rows"]
    w = y_lru.shape[-1]
    br = pl.BlockSpec((None, tm, w), lambda b, i: (b, i, 0))
    return pl.pallas_call(
        _merge_kernel,
        grid=(bsz, s // tm),
        in_specs=[br, br, br,
                  pl.BlockSpec((None, tm, 3 * d), lambda b, i: (b, i, COL_GATE // (3 * d))),
                  pl.BlockSpec((None, tm, d), lambda b, i: (b, i, 0)),
                  _of_layer(l, w, d), _of_layer(l, w, d), _of_layer(l, w, d), _of_layer(l, d, d),
                  _of_layer(l, 1, d), _mod_spec(l, k_gate, d)],
        out_specs=pl.BlockSpec((None, tm, d), lambda b, i: (b, i, 0)),
        out_shape=jax.ShapeDtypeStruct((bsz, s, d), F32),
        compiler_params=_params("parallel", "parallel"),
        name="mixer_merge",
    )(y_lru, y_ret, y_mla, t, x, wl, wr, wm, wo, g_post, mod)


def _ffn_kernel(x_ref, gpre_ref, sh_ref, sc_ref, wu_ref, cw_ref, cb_ref, wd_ref, gp_ref, gf_ref, o_ref,
                act_ref, halo_ref, *, tm, cw):
    @pl.when(pl.program_id(1) == 0)
    def _():
        halo_ref[...] = jnp.zeros(halo_ref.shape, F32)

    row = lax.broadcasted_iota(jnp.int32, (SUBLANES, cw), 0)
    x = x_ref[...]
    h = (_rms(x, gpre_ref[...]) * (1.0 + sc_ref[...]) + sh_ref[...]).astype(BF16)

    def conv(c0, gain):
        cols = slice(c0, c0 + cw)
        taps = cw_ref[:, cols] * gain
        bias = cb_ref[:, cols] * gain
        xv = jnp.dot(h, wu_ref[:, cols], preferred_element_type=F32)
        h1 = halo_ref[SUBLANES - 1:SUBLANES, cols]
        h2 = halo_ref[SUBLANES - 2:SUBLANES - 1, cols]
        halo_ref[:, cols] = xv[tm - SUBLANES:, :]
        r1 = pltpu.roll(xv, 1, 0)
        r2 = pltpu.roll(xv, 2, 0)
        xm1 = jnp.concatenate([jnp.where(row == 0, h1, r1[:SUBLANES]), r1[SUBLANES:]], axis=0)
        top2 = jnp.where(row == 0, h2, jnp.where(row == 1, h1, r2[:SUBLANES]))
        xm2 = jnp.concatenate([top2, r2[SUBLANES:]], axis=0)
        y = xv * taps[2:3] + bias
        y = y + xm2 * taps[0:1]
        return y + xm1 * taps[1:2]

    for c in range(D_FF // cw):
        hu = conv(c * cw, 0.5)
        g = conv(D_FF + c * cw, 1.0)
        th = jnp.tanh(g * (GELU_C1 + GELU_C2 * (g * g)))
        act_ref[:, c * cw:(c + 1) * cw] = ((hu * g) * (1.0 + th)).astype(BF16)
    y = jnp.dot(act_ref[...], wd_ref[...], preferred_element_type=F32)
    o_ref[...] = x + gf_ref[...] * _rms(y, gp_ref[...])


def _ffn(l, x, g_pre, mod, k_shift, k_scale, k_gate, wu, conv_w, conv_b, wd, g_post):
    bsz, s, d = x.shape
    tm = TILES["ffn_rows"]
    n = wu.shape[-1]
    single = dict(pipeline_mode=pl.Buffered(1))
    return pl.pallas_call(
        functools.partial(_ffn_kernel, tm=tm, cw=TILES["ffn_cols"]),
        grid=(bsz, s // tm),
        in_specs=[pl.BlockSpec((None, tm, d), lambda b, i: (b, i, 0)),
                  _of_layer(l, 1, d), _mod_spec(l, k_shift, d), _mod_spec(l, k_scale, d),
                  _of_layer(l, d, n, **single), _of_layer(l, FFN_CONV, n), _of_layer(l, 1, n),
                  _of_layer(l, D_FF, d, **single), _of_layer(l, 1, d), _mod_spec(l, k_gate, d)],
        out_specs=pl.BlockSpec((None, tm, d), lambda b, i: (b, i, 0)),
        out_shape=jax.ShapeDtypeStruct((bsz, s, d), F32),
        scratch_shapes=[pltpu.VMEM((tm, D_FF), BF16), pltpu.VMEM((SUBLANES, n), F32)],
        compiler_params=_params("parallel", "arbitrary"),
        name="ffn_fused",
    )(x, g_pre, mod, mod, wu, conv_w, conv_b, wd, g_post, mod)


def _pack_w_in_kernel(w_ref, o_ref, *, n_all):
    o_cq = 2 * LRU_WIDTH + 4 * RET_HEADS * RET_DK
    o_ckv = o_cq + MLA_Q_RANK
    o_kr = o_ckv + MLA_KV_RANK
    o_gate = o_kr + MLA_ROPE
    n_gate = n_all - o_gate
    rows = w_ref.shape[0]
    o_ref[:, 0:o_cq] = w_ref[:, 0:o_cq]
    o_ref[:, COL_MLA:COL_MLA + MLA_KV_RANK] = w_ref[:, o_ckv:o_kr]
    o_ref[:, N_IN_PACKED - MLA_Q_RANK:N_IN_PACKED] = w_ref[:, o_cq:o_ckv]
    tail = w_ref[:, o_kr:w_ref.shape[1]].astype(F32)
    o_ref[:, COL_GATE:COL_GATE + n_gate] = tail[:, MLA_ROPE:MLA_ROPE + n_gate].astype(BF16)
    kr = jnp.concatenate([jnp.zeros((rows, KR_LANE), F32), tail[:, :MLA_ROPE],
                          jnp.zeros((rows, LANES - KR_LANE - MLA_ROPE), F32)], axis=1)
    o_ref[:, COL_MLA + MLA_KV_RANK:COL_MLA + MLA_KV_RANK + LANES] = kr.astype(BF16)


def _pack_w_in(w, n_all):
    depth, d, n = w.shape
    tr = TILES["pack_rows"]
    return pl.pallas_call(
        functools.partial(_pack_w_in_kernel, n_all=n_all),
        grid=(depth, d // tr),
        in_specs=[pl.BlockSpec((None, tr, n), lambda l, i: (l, i, 0))],
        out_specs=pl.BlockSpec((None, tr, N_IN_PACKED), lambda l, i: (l, i, 0)),
        out_shape=jax.ShapeDtypeStruct((depth, d, N_IN_PACKED), BF16),
        compiler_params=_params("parallel", "parallel"),
        name="pack_w_in",
    )(w)


def _pad_heads(w, width):
    depth, r, _ = w.shape
    w4 = w.reshape(depth, r, MLA_HEADS, width)
    return jnp.pad(w4, ((0, 0), (0, 0), (0, 0), (0, LANES - width))).reshape(depth, r, MLA_HEADS * LANES)


def _pack_w_ukv(w):
    depth, r, _ = w.shape
    w4 = w.reshape(depth, r, MLA_HEADS, MLA_NOPE + MLA_V)
    k = _pad_heads(w4[..., :MLA_NOPE].reshape(depth, r, -1), MLA_NOPE)
    v = _pad_heads(w4[..., MLA_NOPE:].reshape(depth, r, -1), MLA_V)
    return k.astype(BF16), jnp.swapaxes(v, 1, 2).astype(BF16)


def _block_diag(w):
    depth, nb, n, _ = w.shape
    eye = jnp.eye(nb, dtype=w.dtype)
    return (eye[None, :, None, :, None] * w[:, :, :, None, :]).reshape(depth, nb * n, nb * n)


def kernel(x, c, positions, ada_w, ada_b, mix_pre_g, mix_post_g, w_in, lru_conv_w, lru_conv_b, lru_wa, lru_ba, lru_wx, lru_bx, lru_lambda, lru_wo, ret_wo, mla_q_norm_g, mla_w_uq, mla_kv_norm_g, mla_w_ukv, mla_wo, w_out, ffn_pre_g, ffn_post_g, ffn_w_up, ffn_conv_w, ffn_conv_b, ffn_w_down):
    bsz, s, d = x.shape
    depth = w_in.shape[0]
    row = lambda a: a.reshape(depth, 1, a.shape[-1])
    mod = _ada(c, ada_w, ada_b).reshape(depth, SUBLANES, 6, 1, d)
    cos_r, sin_r, cos_m, sin_m, cos_mt, sin_mt = _rope_tables(positions)

    n_in = w_in.shape[-1]
    w_in_p = _pack_w_in(jnp.concatenate(
        [w_in.astype(BF16), jnp.zeros((depth, d, N_IN_PACKED - n_in), BF16)], axis=-1), n_in)
    wbd = jnp.concatenate([_block_diag(lru_wa), _block_diag(lru_wx)], axis=-1).astype(BF16)
    bb = jnp.concatenate([lru_ba, lru_bx], axis=-1).reshape(depth, 1, 2 * LRU_WIDTH)
    w_uqt = jnp.swapaxes(_pad_heads(mla_w_uq, MLA_NOPE + MLA_ROPE), 1, 2).astype(BF16)
    w_uk, w_uvt = _pack_w_ukv(mla_w_ukv)
    wl, wr, wm, wo = (a.astype(BF16) for a in (lru_wo, ret_wo, mla_wo, w_out))
    wu, wd = ffn_w_up.astype(BF16), ffn_w_down.astype(BF16)

    for l in range(depth):
        t = _prenorm_matmul(l, x, row(mix_pre_g), mod, 0, 1, w_in_p, "mixer_in_proj")
        y_lru = _lru(l, t, lru_conv_w, row(lru_conv_b), wbd, bb, row(lru_lambda))
        y_ret = _ret(t, cos_r, sin_r)
        qt, k, vt = _mla_proj(l, t, row(mla_q_norm_g), row(mla_kv_norm_g), w_uqt, w_uk, w_uvt,
                              cos_m, sin_m, cos_mt, sin_mt)
        y_mla = _flash(qt, k, vt)
        x = _merge(l, y_lru, y_ret, y_mla, t, x, wl, wr, wm, wo, row(mix_post_g), mod, 2)
        x = _ffn(l, x, row(ffn_pre_g), mod, 3, 4, 5, wu, ffn_conv_w, row(ffn_conv_b), wd, row(ffn_post_g))
    return x
```
